```python
import jax, jax.numpy as jnp
from jax import lax
import numpy as np

D_MODEL = 2048
BATCH = 8
SEQ = 4096
DEPTH = 2

GRID_W = 64
CTX_LEN = 256
EPS = 1e-6
HEAD_DIM = 128
N_HEADS = 8
KV_HEADS = 2
GROUP = N_HEADS // KV_HEADS
Q_WIDTH = N_HEADS * HEAD_DIM
KV_WIDTH = KV_HEADS * HEAD_DIM
AXIS_DIM = HEAD_DIM // 2
ROPE_THETA = 10000.0
Q_BLOCK = 128
ATTN_SCALE = HEAD_DIM ** -0.5
CONV_C = D_MODEL // 2
CONV_K = 31
CONV_PAD = CONV_K // 2
IN_WIDTH = Q_WIDTH + 2 * KV_WIDTH + 2 * CONV_C
MIX_WIDTH = Q_WIDTH + CONV_C
IN_SPLITS = (Q_WIDTH, Q_WIDTH + KV_WIDTH, Q_WIDTH + 2 * KV_WIDTH)
POOL_WINDOWS = (2, 4, 8, 16)
N_POOL_GROUPS = len(POOL_WINDOWS)
POOL_GC = D_MODEL // N_POOL_GROUPS
N_EXPERTS = 64
N_EXPERT_GROUPS = 8
EXPERTS_PER_GROUP = N_EXPERTS // N_EXPERT_GROUPS
TOPK_GROUPS = 4
TOP_K = 8
EXPERT_FF = 512
SHARED_FF = 512
ROUTED_SCALE = 2.5
MOE_BLOCK = 128

kernel_name = "hybrid_dit_attn_conformer_pool_moe"


def rms_norm(x, g):
    xf = x.astype(jnp.float32)
    y = xf * lax.rsqrt(jnp.mean(xf * xf, axis=-1, keepdims=True) + EPS)
    return (y * g.astype(jnp.float32)).astype(x.dtype)


def layer_norm(x, g, b):
    xf = x.astype(jnp.float32)
    mu = jnp.mean(xf, axis=-1, keepdims=True)
    var = jnp.mean(jnp.square(xf - mu), axis=-1, keepdims=True)
    y = (xf - mu) * lax.rsqrt(var + EPS)
    return (y * g.astype(jnp.float32) + b.astype(jnp.float32)).astype(x.dtype)


def adaln(x, g, shift, scale):
    return rms_norm(x, g) * (1 + scale) + shift


def rope_tables(n_tok, dtype):
    rows = n_tok // GRID_W
    r, col = jnp.meshgrid(jnp.arange(rows), jnp.arange(GRID_W), indexing="ij")
    pos = jnp.stack([r.reshape(-1), col.reshape(-1)], axis=-1).astype(jnp.float32)
    inv = ROPE_THETA ** (-jnp.arange(0, AXIS_DIM, 2, dtype=jnp.float32) / AXIS_DIM)
    ang = pos[:, :, None] * inv
    ang = jnp.broadcast_to(ang[:, :, None, :], (n_tok, 2, 2, AXIS_DIM // 2)).reshape(n_tok, HEAD_DIM)
    return jnp.cos(ang).astype(dtype), jnp.sin(ang).astype(dtype)


def apply_rope(x, cos, sin):
    xs = x.reshape(x.shape[:-1] + (2, 2, AXIS_DIM // 2))
    rot = jnp.concatenate([-xs[..., 1:, :], xs[..., :1, :]], axis=-2).reshape(x.shape)
    return x * cos[:, None, :] + rot * sin[:, None, :]


def attend(q, k, v):
    b, n_q = q.shape[:2]
    qb = jnp.moveaxis(q.reshape(b, n_q // Q_BLOCK, Q_BLOCK, KV_HEADS, GROUP, HEAD_DIM), 1, 0)

    def one_block(q_blk):
        s = jnp.einsum("bqhgd,bkhd->bhgqk", q_blk, k, preferred_element_type=jnp.float32) * ATTN_SCALE
        p = jax.nn.softmax(s, axis=-1).astype(v.dtype)
        return jnp.einsum("bhgqk,bkhd->bqhgd", p, v)

    o = lax.map(one_block, qb)
    return jnp.moveaxis(o, 0, 1).reshape(b, n_q, Q_WIDTH)


def conformer_conv(u, conv_w, conv_b, cn_g, cn_b):
    a, gte = jnp.split(u, 2, axis=-1)
    z = a * jax.nn.sigmoid(gte)
    z = lax.conv_general_dilated(z, conv_w[:, None, :].astype(z.dtype), window_strides=(1,),
                                 padding=[(CONV_PAD, CONV_PAD)], dimension_numbers=("NWC", "WIO", "NWC"),
                                 feature_group_count=CONV_C) + conv_b
    return jax.nn.silu(layer_norm(z, cn_g, cn_b))


def hybrid_mixer(h, hc, w_in, q_g, k_g, conv_w, conv_b, cn_g, cn_b, w_out, cos, sin, need_ctx):
    b, s, _ = h.shape
    lc = hc.shape[1]
    q, k, v, u = jnp.split(h @ w_in, IN_SPLITS, axis=-1)
    q = apply_rope(rms_norm(q.reshape(b, s, N_HEADS, HEAD_DIM), q_g), cos, sin)
    k = apply_rope(rms_norm(k.reshape(b, s, KV_HEADS, HEAD_DIM), k_g), cos, sin)
    v = v.reshape(b, s, KV_HEADS, HEAD_DIM)
    if need_ctx:
        qc, kc, vc, uc = jnp.split(hc @ w_in, IN_SPLITS, axis=-1)
    else:
        kc, vc = jnp.split(hc @ w_in[:, Q_WIDTH:Q_WIDTH + 2 * KV_WIDTH], 2, axis=-1)
    kc = rms_norm(kc.reshape(b, lc, KV_HEADS, HEAD_DIM), k_g)
    vc = vc.reshape(b, lc, KV_HEADS, HEAD_DIM)
    attn = attend(q.reshape(b, s, KV_HEADS, GROUP, HEAD_DIM),
                  jnp.concatenate([k, kc], axis=1), jnp.concatenate([v, vc], axis=1))
    y = jnp.concatenate([attn, conformer_conv(u, conv_w, conv_b, cn_g, cn_b)], axis=-1) @ w_out
    if not need_ctx:
        return y, None
    qc = rms_norm(qc.reshape(b, lc, N_HEADS, HEAD_DIM), q_g)
    attn_c = attend(qc.reshape(b, lc, KV_HEADS, GROUP, HEAD_DIM), kc, vc)
    yc = jnp.concatenate([attn_c, conformer_conv(uc, conv_w, conv_b, cn_g, cn_b)], axis=-1) @ w_out
    return y, yc


def pool_mixer(h, w_pool, b_pool, scale):
    b, n, d = h.shape
    hf = h.astype(jnp.float32)
    cs = jnp.concatenate([jnp.zeros((b, 1, d), jnp.float32), jnp.cumsum(hf, axis=1)], axis=1)
    t = jnp.arange(n)
    outs = []
    for gi, w in enumerate(POOL_WINDOWS):
        lo = jnp.clip(t - w // 2, 0, n)
        hi = jnp.clip(t + w - w // 2, 0, n)
        csg = cs[:, :, gi * POOL_GC:(gi + 1) * POOL_GC]
        mean = (csg[:, hi] - csg[:, lo]) / (hi - lo).astype(jnp.float32)[:, None]
        outs.append(mean - hf[:, :, gi * POOL_GC:(gi + 1) * POOL_GC])
    p = jnp.stack(outs, axis=2).astype(h.dtype)
    y = jnp.einsum("bsgc,gcd->bsgd", p, w_pool) + b_pool.reshape(N_POOL_GROUPS, POOL_GC)
    return y.reshape(b, n, d) * scale


def moe_ffn(h, router_w, router_bias, w_gate, w_up, w_down, s_gate, s_up, s_down):
    n_tok, d = h.shape
    scores = jax.nn.sigmoid((h @ router_w).astype(jnp.float32))
    sel = scores + router_bias.astype(jnp.float32)
    grp_score = lax.top_k(sel.reshape(n_tok, N_EXPERT_GROUPS, EXPERTS_PER_GROUP), 2)[0].sum(-1)
    _, gidx = lax.top_k(grp_score, TOPK_GROUPS)
    gmask = jax.nn.one_hot(gidx, N_EXPERT_GROUPS, dtype=jnp.float32).sum(1) > 0
    sel = jnp.where(jnp.repeat(gmask, EXPERTS_PER_GROUP, axis=1), sel, -jnp.inf)
    _, eidx = lax.top_k(sel, TOP_K)
    wts = jnp.take_along_axis(scores, eidx, axis=1)
    wts = wts / jnp.sum(wts, axis=-1, keepdims=True) * ROUTED_SCALE

    n_asg = n_tok * TOP_K
    e_flat = eidx.reshape(-1)
    tok_flat = jnp.repeat(jnp.arange(n_tok, dtype=jnp.int32), TOP_K)
    order = jnp.argsort(e_flat)
    e_s, tok_s, g_s = e_flat[order], tok_flat[order], wts.reshape(-1)[order]
    counts = jnp.zeros((N_EXPERTS,), jnp.int32).at[e_flat].add(1)
    off = jnp.cumsum(counts) - counts
    pcounts = (counts + MOE_BLOCK - 1) // MOE_BLOCK * MOE_BLOCK
    pend = jnp.cumsum(pcounts)
    poff = pend - pcounts
    dest = poff[e_s] + (jnp.arange(n_asg, dtype=jnp.int32) - off[e_s])
    n_rows = n_asg + N_EXPERTS * MOE_BLOCK
    n_blocks = n_rows // MOE_BLOCK
    buf_tok = jnp.full((n_rows,), n_tok, jnp.int32).at[dest].set(tok_s)
    buf_g = jnp.zeros((n_rows,), jnp.float32).at[dest].set(g_s)
    blk_e = jnp.minimum(jnp.searchsorted(pend, jnp.arange(n_blocks, dtype=jnp.int32) * MOE_BLOCK, side="right"),
                        N_EXPERTS - 1).astype(jnp.int32)
    h_pad = jnp.concatenate([h, jnp.zeros((1, d), h.dtype)], axis=0)

    def expert_block(acc, blk):
        tok_b, g_b, e_b = blk
        xb = h_pad[tok_b]
        out = (jax.nn.silu(xb @ w_gate[e_b]) * (xb @ w_up[e_b])) @ w_down[e_b]
        return acc.at[tok_b].add(out * g_b[:, None].astype(out.dtype)), None

    routed, _ = lax.scan(expert_block, jnp.zeros((n_tok + 1, d), h.dtype),
                         (buf_tok.reshape(n_blocks, MOE_BLOCK), buf_g.reshape(n_blocks, MOE_BLOCK), blk_e))
    shared = (jax.nn.silu(h @ s_gate) * (h @ s_up)) @ s_down
    return routed[:n_tok] + shared


def setup_inputs(seed: int = 0) -> dict:
    key = jax.random.key(seed)
    ks = jax.random.split(key, 32)
    n_even = (DEPTH + 1) // 2
    n_odd = DEPTH // 2
    nrm = lambda k, shape, s: jax.random.normal(k, shape, jnp.float32) * s
    return {
        "x": nrm(ks[0], (BATCH, SEQ, D_MODEL), 1.0),
        "c": nrm(ks[1], (BATCH, D_MODEL), 1.0),
        "ctx": nrm(ks[2], (BATCH, CTX_LEN, D_MODEL), 1.0),
        "c_ctx": nrm(ks[3], (D_MODEL,), 1.0),
        "w_mod": nrm(ks[4], (DEPTH, D_MODEL, 6 * D_MODEL), 0.5 * D_MODEL ** -0.5),
        "b_mod": nrm(ks[5], (DEPTH, 6 * D_MODEL), 0.02),
        "norm1_g": 1.0 + nrm(ks[6], (DEPTH, D_MODEL), 0.1),
        "norm2_g": 1.0 + nrm(ks[7], (DEPTH, D_MODEL), 0.1),
        "mix_w_in": nrm(ks[8], (n_even, D_MODEL, IN_WIDTH), D_MODEL ** -0.5),
        "q_norm_g": 1.0 + nrm(ks[9], (n_even, HEAD_DIM), 0.1),
        "k_norm_g": 1.0 + nrm(ks[10], (n_even, HEAD_DIM), 0.1),
        "conv_w": nrm(ks[11], (n_even, CONV_K, CONV_C), CONV_K ** -0.5),
        "conv_b": nrm(ks[12], (n_even, CONV_C), 0.02),
        "conv_norm_g": 1.0 + nrm(ks[13], (n_even, CONV_C), 0.1),
        "conv_norm_b": nrm(ks[14], (n_even, CONV_C), 0.02),
        "mix_w_out": nrm(ks[15], (n_even, MIX_WIDTH, D_MODEL), MIX_WIDTH ** -0.5),
        "pool_w": nrm(ks[16], (n_odd, N_POOL_GROUPS, POOL_GC, POOL_GC), POOL_GC ** -0.5),
        "pool_b": nrm(ks[17], (n_odd, D_MODEL), 0.02),
        "pool_scale": 1.0 + nrm(ks[18], (n_odd, D_MODEL), 0.1),
        "router_w": nrm(ks[19], (DEPTH, D_MODEL, N_EXPERTS), D_MODEL ** -0.5),
        "router_bias": nrm(ks[20], (DEPTH, N_EXPERTS), 0.01),
        "moe_w_gate": nrm(ks[21], (DEPTH, N_EXPERTS, D_MODEL, EXPERT_FF), D_MODEL ** -0.5),
        "moe_w_up": nrm(ks[22], (DEPTH, N_EXPERTS, D_MODEL, EXPERT_FF), D_MODEL ** -0.5),
        "moe_w_down": nrm(ks[23], (DEPTH, N_EXPERTS, EXPERT_FF, D_MODEL), EXPERT_FF ** -0.5),
        "shared_w_gate": nrm(ks[24], (DEPTH, D_MODEL, SHARED_FF), D_MODEL ** -0.5),
        "shared_w_up": nrm(ks[25], (DEPTH, D_MODEL, SHARED_FF), D_MODEL ** -0.5),
        "shared_w_down": nrm(ks[26], (DEPTH, SHARED_FF, D_MODEL), SHARED_FF ** -0.5),
    }


def reference(x, c, ctx, c_ctx, w_mod, b_mod, norm1_g, norm2_g, mix_w_in, q_norm_g, k_norm_g, conv_w, conv_b,
              conv_norm_g, conv_norm_b, mix_w_out, pool_w, pool_b, pool_scale, router_w, router_bias,
              moe_w_gate, moe_w_up, moe_w_down, shared_w_gate, shared_w_up, shared_w_down):
    b, s, d = x.shape
    cos, sin = rope_tables(s, x.dtype)
    silu_c = jax.nn.silu(c)
    silu_cc = jax.nn.silu(c_ctx)
    for i in range(DEPTH):
        need_ctx = any(j % 2 == 0 for j in range(i + 1, DEPTH))
        sh1, sc1, g1, sh2, sc2, g2 = jnp.split(silu_c @ w_mod[i] + b_mod[i], 6, axis=-1)
        csh1, csc1, cg1, csh2, csc2, cg2 = jnp.split(silu_cc @ w_mod[i] + b_mod[i], 6, axis=-1)
        h = adaln(x, norm1_g[i], sh1[:, None], sc1[:, None])
        if i % 2 == 0:
            e = i // 2
            hc = adaln(ctx, norm1_g[i], csh1, csc1)
            y, yc = hybrid_mixer(h, hc, mix_w_in[e], q_norm_g[e], k_norm_g[e], conv_w[e], conv_b[e],
                                 conv_norm_g[e], conv_norm_b[e], mix_w_out[e], cos, sin, need_ctx)
        else:
            o = i // 2
            y = pool_mixer(h, pool_w[o], pool_b[o], pool_scale[o])
            yc = pool_mixer(adaln(ctx, norm1_g[i], csh1, csc1), pool_w[o], pool_b[o], pool_scale[o]) if need_ctx else None
        x = x + g1[:, None] * y
        h2 = adaln(x, norm2_g[i], sh2[:, None], sc2[:, None])
        x = x + g2[:, None] * moe_ffn(h2.reshape(b * s, d), router_w[i], router_bias[i], moe_w_gate[i], moe_w_up[i],
                                      moe_w_down[i], shared_w_gate[i], shared_w_up[i], shared_w_down[i]).reshape(b, s, d)
        if need_ctx:
            ctx = ctx + cg1 * yc
            hc2 = adaln(ctx, norm2_g[i], csh2, csc2)
            ctx = ctx + cg2 * moe_ffn(hc2.reshape(-1, d), router_w[i], router_bias[i], moe_w_gate[i], moe_w_up[i],
                                      moe_w_down[i], shared_w_gate[i], shared_w_up[i],
                                      shared_w_down[i]).reshape(ctx.shape)
    return x
```

```python
import functools

import jax
import jax.numpy as jnp
from jax import lax
from jax.experimental import pallas as pl
from jax.experimental.pallas import tpu as pltpu

HEAD_DIM = 128
KV_HEADS = 2
GRID_W = 64
ROPE_THETA = 10000.0
EPS = 1e-6
CONV_K = 31
CONV_PAD = CONV_K // 2
POOL_WINDOWS = (2, 4, 8, 16)
N_EXPERT_GROUPS = 8
TOPK_GROUPS = 4
TOP_K = 8
ROUTED_SCALE = 2.5

LANES = 128
SUBLANES = 8
HALO = 16
VMEM_LIMIT_BYTES = 56 * 1024 * 1024

F32 = jnp.float32
BF16 = jnp.bfloat16
HIGHEST = lax.Precision.HIGHEST


def _tile(n, pref):
    t = min(n, pref)
    while n % t:
        t //= 2
    return t


def _params(*sem):
    return pltpu.CompilerParams(dimension_semantics=sem, vmem_limit_bytes=VMEM_LIMIT_BYTES)


def _const_spec(shape):
    nd = len(shape)
    return pl.BlockSpec(shape, lambda *_: (0,) * nd, pipeline_mode=pl.Buffered(1))


def _sigmoid(x):
    return 1.0 / (1.0 + jnp.exp(-x))


def _adaln(x, g, shift, scale):
    ms = jnp.mean(x * x, axis=-1, keepdims=True)
    return (x * lax.rsqrt(ms + EPS) * g) * (1.0 + scale) + shift


def _mod_kernel(c_ref, w_ref, b_ref, o_ref):
    c = c_ref[...]
    a = c * _sigmoid(c)
    o_ref[...] = jnp.dot(a, w_ref[...], preferred_element_type=F32, precision=HIGHEST) + b_ref[...]


def _modulation(cc, w_mod, b_mod):
    n_layers, d, n = w_mod.shape
    r = cc.shape[0]
    tn = _tile(n, 1024)
    return pl.pallas_call(
        _mod_kernel,
        grid=(n_layers, n // tn),
        in_specs=[
            pl.BlockSpec((r, d), lambda l, j: (0, 0)),
            pl.BlockSpec((None, d, tn), lambda l, j: (l, 0, j)),
            pl.BlockSpec((None, 1, tn), lambda l, j: (l, 0, j)),
        ],
        out_specs=pl.BlockSpec((None, r, tn), lambda l, j: (l, 0, j)),
        out_shape=jax.ShapeDtypeStruct((n_layers, r, n), F32),
        compiler_params=_params("arbitrary", "arbitrary"),
        name="modulation",
    )(cc, w_mod, b_mod.reshape(n_layers, 1, n))


def _head_norm(x, g):
    ms = jnp.mean(x * x, axis=-1, keepdims=True)
    return x * lax.rsqrt(ms + EPS) * g


def _rope(x, cos, sin_signed, first_half):
    up = pltpu.roll(x, HEAD_DIM - HEAD_DIM // 4, 1)
    dn = pltpu.roll(x, HEAD_DIM // 4, 1)
    return x * cos + jnp.where(first_half, up, dn) * sin_signed


def _inproj_kernel(x_ref, sh_ref, sc_ref, g_ref, w_ref, qg_ref, kg_ref, cos_ref, sin_ref,
                   q_ref, k_ref, v_ref, z_ref, *, n_heads, conv_c):
    h = _adaln(x_ref[...], g_ref[...], sh_ref[...], sc_ref[...]).astype(BF16)
    acc = jnp.dot(h, w_ref[...], preferred_element_type=F32)
    cos = cos_ref[...]
    sin = sin_ref[...]
    lane = lax.broadcasted_iota(jnp.int32, cos.shape, 1)
    first_half = (lane % (HEAD_DIM // 2)) < (HEAD_DIM // 4)
    q_w = n_heads * HEAD_DIM
    kv_w = KV_HEADS * HEAD_DIM
    q_scale = HEAD_DIM ** -0.5
    for hd in range(n_heads):
        sl = slice(hd * HEAD_DIM, (hd + 1) * HEAD_DIM)
        qn = _head_norm(acc[:, sl], qg_ref[...])
        q_ref[:, sl] = (_rope(qn, cos, sin, first_half) * q_scale).astype(q_ref.dtype)
    for hd in range(KV_HEADS):
        sl = slice(hd * HEAD_DIM, (hd + 1) * HEAD_DIM)
        kn = _head_norm(acc[:, q_w + hd * HEAD_DIM:q_w + (hd + 1) * HEAD_DIM], kg_ref[...])
        k_ref[:, sl] = _rope(kn, cos, sin, first_half).astype(k_ref.dtype)
    v_ref[...] = acc[:, q_w + kv_w:q_w + 2 * kv_w].astype(v_ref.dtype)
    u0 = q_w + 2 * kv_w
    a = acc[:, u0:u0 + conv_c]
    gate = acc[:, u0 + conv_c:u0 + 2 * conv_c]
    z_ref[...] = a * _sigmoid(gate)


def _inproj(x, sh, sc, g, w_in, q_g, k_g, cos, sin_signed):
    b, s, d = x.shape
    in_w = w_in.shape[1]
    kv_w = KV_HEADS * HEAD_DIM
    conv_c = d // 2
    q_w = in_w - 2 * kv_w - 2 * conv_c
    n_heads = q_w // HEAD_DIM
    tm = _tile(s, 512)
    row = lambda bi, i: (bi, i, 0)
    vec = pl.BlockSpec((None, 1, d), lambda bi, i: (bi, 0, 0))
    tab = pl.BlockSpec((tm, HEAD_DIM), lambda bi, i: (i, 0))
    return pl.pallas_call(
        functools.partial(_inproj_kernel, n_heads=n_heads, conv_c=conv_c),
        grid=(b, s // tm),
        in_specs=[
            pl.BlockSpec((None, tm, d), row), vec, vec, _const_spec((1, d)), _const_spec((d, in_w)),
            _const_spec((1, HEAD_DIM)), _const_spec((1, HEAD_DIM)), tab, tab,
        ],
        out_specs=[
            pl.BlockSpec((None, tm, q_w), row), pl.BlockSpec((None, tm, kv_w), row),
            pl.BlockSpec((None, tm, kv_w), row), pl.BlockSpec((None, tm, conv_c), row),
        ],
        out_shape=[
            jax.ShapeDtypeStruct((b, s, q_w), BF16), jax.ShapeDtypeStruct((b, s, kv_w), BF16),
            jax.ShapeDtypeStruct((b, s, kv_w), BF16), jax.ShapeDtypeStruct((b, s, conv_c), F32),
        ],
        compiler_params=_params("arbitrary", "arbitrary"),
        name="inproj",
    )(x, sh, sc, g, w_in, q_g, k_g, cos, sin_signed)


def _ctx_kv_kernel(x_ref, sh_ref, sc_ref, g_ref, w_ref, kg_ref, k_ref, v_ref):
    h = _adaln(x_ref[...], g_ref[...], sh_ref[...], sc_ref[...]).astype(BF16)
    acc = jnp.dot(h, w_ref[...], preferred_element_type=F32)
    kv_w = KV_HEADS * HEAD_DIM
    for hd in range(KV_HEADS):
        sl = slice(hd * HEAD_DIM, (hd + 1) * HEAD_DIM)
        k_ref[:, sl] = _head_norm(acc[:, sl], kg_ref[...]).astype(k_ref.dtype)
    v_ref[...] = acc[:, kv_w:2 * kv_w].astype(v_ref.dtype)


def _ctx_kv(ctx, sh, sc, g, w_kv, k_g):
    b, lc, d = ctx.shape
    kv_w = KV_HEADS * HEAD_DIM
    row = lambda bi: (bi, 0, 0)
    return pl.pallas_call(
        _ctx_kv_kernel,
        grid=(b,),
        in_specs=[
            pl.BlockSpec((None, lc, d), row), _const_spec((1, d)), _const_spec((1, d)), _const_spec((1, d)),
            _const_spec((d, 2 * kv_w)), _const_spec((1, HEAD_DIM)),
        ],
        out_specs=[pl.BlockSpec((None, lc, kv_w), row), pl.BlockSpec((None, lc, kv_w), row)],
        out_shape=[jax.ShapeDtypeStruct((b, lc, kv_w), BF16), jax.ShapeDtypeStruct((b, lc, kv_w), BF16)],
        compiler_params=_params("arbitrary"),
        name="ctx_kv",
    )(ctx, sh, sc, g, w_kv, k_g)


def _attn_kernel(q_ref, k_ref, v_ref, kc_ref, vc_ref, o_ref):
    q = q_ref[...]
    nt = (((1,), (1,)), ((), ()))
    s1 = lax.dot_general(q, k_ref[...], nt, preferred_element_type=F32)
    s2 = lax.dot_general(q, kc_ref[...], nt, preferred_element_type=F32)
    m = jnp.maximum(jnp.max(s1, axis=-1, keepdims=True), jnp.max(s2, axis=-1, keepdims=True))
    p1 = jnp.exp(s1 - m)
    p2 = jnp.exp(s2 - m)
    l = jnp.sum(p1, axis=-1, keepdims=True) + jnp.sum(p2, axis=-1, keepdims=True)
    o = jnp.dot(p1.astype(BF16), v_ref[...], preferred_element_type=F32)
    o = o + jnp.dot(p2.astype(BF16), vc_ref[...], preferred_element_type=F32)
    o_ref[...] = (o / l).astype(o_ref.dtype)


def _attention(q, k, v, kc, vc):
    b, s, q_w = q.shape
    lc = kc.shape[1]
    group = q_w // (KV_HEADS * HEAD_DIM)
    tq = _tile(s, 512)
    kv_spec = lambda n: pl.BlockSpec((None, n, HEAD_DIM), lambda bi, h, g, i: (bi, 0, h))
    q_spec = pl.BlockSpec((None, tq, HEAD_DIM), lambda bi, h, g, i: (bi, i, h * group + g))
    return pl.pallas_call(
        _attn_kernel,
        grid=(b, KV_HEADS, group, s // tq),
        in_specs=[q_spec, kv_spec(s), kv_spec(s), kv_spec(lc), kv_spec(lc)],
        out_specs=q_spec,
        out_shape=jax.ShapeDtypeStruct((b, s, q_w), BF16),
        compiler_params=_params("arbitrary", "arbitrary", "arbitrary", "arbitrary"),
        name="attention",
    )(q, k, v, kc, vc)


def _conv_kernel(zc_ref, zp_ref, zn_ref, w_ref, b_ref, g_ref, beta_ref, o_ref, win_ref, conv_ref,
                 *, ts, n_tiles, rb, cb):
    i = pl.program_id(1)
    c = zc_ref.shape[1]
    win_ref[HALO:HALO + ts, :] = zc_ref[...]
    win_ref[0:HALO, :] = jnp.where(i > 0, zp_ref[...], 0.0)
    win_ref[HALO + ts:2 * HALO + ts, :] = jnp.where(i < n_tiles - 1, zn_ref[...], 0.0)
    base = HALO - CONV_PAD
    for r in range(ts // rb):
        for cc in range(c // cb):
            cs = slice(cc * cb, (cc + 1) * cb)
            acc = jnp.zeros((rb, cb), F32)
            for kk in range(CONV_K):
                r0 = r * rb + base + kk
                acc = acc + win_ref[r0:r0 + rb, cs] * w_ref[kk:kk + 1, cs]
            conv_ref[r * rb:(r + 1) * rb, cs] = acc + b_ref[:, cs]
    z = conv_ref[...]
    mu = jnp.mean(z, axis=-1, keepdims=True)
    zc = z - mu
    var = jnp.mean(zc * zc, axis=-1, keepdims=True)
    y = zc * lax.rsqrt(var + EPS) * g_ref[...] + beta_ref[...]
    o_ref[...] = (y * _sigmoid(y)).astype(o_ref.dtype)


def _conformer_conv(z, conv_w, conv_b, cn_g, cn_b):
    b, s, c = z.shape
    ts = _tile(s, 128)
    n_tiles = s // ts
    hb = ts // HALO
    n_hb = s // HALO
    cur = lambda bi, i: (bi, i, 0)
    prev = lambda bi, i: (bi, jnp.maximum(i * hb - 1, 0), 0)
    nxt = lambda bi, i: (bi, jnp.minimum((i + 1) * hb, n_hb - 1), 0)
    return pl.pallas_call(
        functools.partial(_conv_kernel, ts=ts, n_tiles=n_tiles, rb=_tile(ts, 32), cb=_tile(c, 512)),
        grid=(b, n_tiles),
        in_specs=[
            pl.BlockSpec((None, ts, c), cur), pl.BlockSpec((None, HALO, c), prev), pl.BlockSpec((None, HALO, c), nxt),
            _const_spec((CONV_K, c)), _const_spec((1, c)), _const_spec((1, c)), _const_spec((1, c)),
        ],
        out_specs=pl.BlockSpec((None, ts, c), cur),
        out_shape=jax.ShapeDtypeStruct((b, s, c), BF16),
        scratch_shapes=[pltpu.VMEM((ts + 2 * HALO, c), F32), pltpu.VMEM((ts, c), F32)],
        compiler_params=_params("arbitrary", "arbitrary"),
        name="conformer_conv",
    )(z, z, z, conv_w, conv_b, cn_g, cn_b)


def _finish_tile(xnew, g2n_ref, sh2_ref, sc2_ref, rwt_ref, x1_ref, h2_ref, lgt_ref):
    x1_ref[...] = xnew
    h2 = _adaln(xnew, g2n_ref[...], sh2_ref[...], sc2_ref[...])
    h2_ref[...] = h2
    lgt_ref[...] = lax.dot_general(rwt_ref[...], h2, (((1,), (1,)), ((), ())),
                                   preferred_element_type=F32, precision=HIGHEST)


def _outproj_kernel(a_ref, c_ref, x_ref, wa_ref, wc_ref, g1_ref, g2n_ref, sh2_ref, sc2_ref, rwt_ref,
                    x1_ref, h2_ref, lgt_ref):
    y = jnp.dot(a_ref[...], wa_ref[...], preferred_element_type=F32)
    y = y + jnp.dot(c_ref[...], wc_ref[...], preferred_element_type=F32)
    _finish_tile(x_ref[...] + g1_ref[...] * y, g2n_ref, sh2_ref, sc2_ref, rwt_ref, x1_ref, h2_ref, lgt_ref)


def _finish_specs(b, s, d, e, tm):
    nt = s // tm
    row = lambda bi, i: (bi, i, 0)
    out_specs = [pl.BlockSpec((None, tm, d), row), pl.BlockSpec((None, tm, d), row),
                 pl.BlockSpec((e, tm), lambda bi, i: (0, bi * nt + i))]
    out_shape = [jax.ShapeDtypeStruct((b, s, d), F32), jax.ShapeDtypeStruct((b, s, d), F32),
                 jax.ShapeDtypeStruct((e, b * s), F32)]
    return out_specs, out_shape


def _outproj(attn, cv, x, w_a, w_c, g1, g2n, sh2, sc2, rwt):
    b, s, d = x.shape
    e = rwt.shape[0]
    tm = _tile(s, 512)
    row = lambda bi, i: (bi, i, 0)
    vec = pl.BlockSpec((None, 1, d), lambda bi, i: (bi, 0, 0))
    out_specs, out_shape = _finish_specs(b, s, d, e, tm)
    return pl.pallas_call(
        _outproj_kernel,
        grid=(b, s // tm),
        in_specs=[
            pl.BlockSpec((None, tm, attn.shape[2]), row), pl.BlockSpec((None, tm, cv.shape[2]), row),
            pl.BlockSpec((None, tm, d), row), _const_spec(w_a.shape), _const_spec(w_c.shape),
            vec, _const_spec((1, d)), vec, vec, _const_spec((e, d)),
        ],
        out_specs=out_specs,
        out_shape=out_shape,
        compiler_params=_params("arbitrary", "arbitrary"),
        name="outproj",
    )(attn, cv, x, w_a, w_c, g1, g2n, sh2, sc2, rwt)


def _pool_kernel(xc_ref, xp_ref, xn_ref, g1n_ref, sh1_ref, sc1_ref, pw_ref, pb_ref, ps_ref, g1_ref,
                 g2n_ref, sh2_ref, sc2_ref, rwt_ref, x1_ref, h2_ref, lgt_ref, win_ref, *, tm, n_tiles, seq):
    i = pl.program_id(1)
    d = xc_ref.shape[1]
    gc = d // len(POOL_WINDOWS)
    norm = lambda x: _adaln(x, g1n_ref[...], sh1_ref[...], sc1_ref[...])
    win_ref[HALO:HALO + tm, :] = norm(xc_ref[...])
    win_ref[0:HALO, :] = jnp.where(i > 0, norm(xp_ref[...]), 0.0)
    win_ref[HALO + tm:2 * HALO + tm, :] = jnp.where(i < n_tiles - 1, norm(xn_ref[...]), 0.0)
    t = i * tm + lax.broadcasted_iota(jnp.int32, (tm, 1), 0)
    ys = []
    for gi, w in enumerate(POOL_WINDOWS):
        cs = slice(gi * gc, (gi + 1) * gc)
        tot = jnp.zeros((tm, gc), F32)
        for off in range(-(w // 2), w - w // 2):
            tot = tot + win_ref[HALO + off:HALO + off + tm, cs]
        lo = jnp.clip(t - w // 2, 0, seq)
        hi = jnp.clip(t + w - w // 2, 0, seq)
        p = tot / (hi - lo).astype(F32) - win_ref[HALO:HALO + tm, cs]
        ys.append(jnp.dot(p.astype(BF16), pw_ref[gi], preferred_element_type=F32))
    y = (jnp.concatenate(ys, axis=-1) + pb_ref[...]) * ps_ref[...]
    _finish_tile(xc_ref[...] + g1_ref[...] * y, g2n_ref, sh2_ref, sc2_ref, rwt_ref, x1_ref, h2_ref, lgt_ref)


def _pool_layer(x, g1n, sh1, sc1, pool_w, pool_b, pool_scale, g1, g2n, sh2, sc2, rwt):
    b, s, d = x.shape
    e = rwt.shape[0]
    tm = _tile(s, 256)
    n_tiles = s // tm
    hb = tm // HALO
    n_hb = s // HALO
    cur = lambda bi, i: (bi, i, 0)
    prev = lambda bi, i: (bi, jnp.maximum(i * hb - 1, 0), 0)
    nxt = lambda bi, i: (bi, jnp.minimum((i + 1) * hb, n_hb - 1), 0)
    vec = pl.BlockSpec((None, 1, d), lambda bi, i: (bi, 0, 0))
    cvec = _const_spec((1, d))
    out_specs, out_shape = _finish_specs(b, s, d, e, tm)
    return pl.pallas_call(
        functools.partial(_pool_kernel, tm=tm, n_tiles=n_tiles, seq=s),
        grid=(b, n_tiles),
        in_specs=[
            pl.BlockSpec((None, tm, d), cur), pl.BlockSpec((None, HALO, d), prev), pl.BlockSpec((None, HALO, d), nxt),
            cvec, vec, vec, _const_spec(pool_w.shape), cvec, cvec, vec, cvec, vec, vec, _const_spec((e, d)),
        ],
        out_specs=out_specs,
        out_shape=out_shape,
        scratch_shapes=[pltpu.VMEM((tm + 2 * HALO, d), F32)],
        compiler_params=_params("arbitrary", "arbitrary"),
        name="pool_mixer",
    )(x, x, x, g1n, sh1, sc1, pool_w, pool_b, pool_scale, g1, g2n, sh2, sc2, rwt)


def _route_kernel(lgt_ref, bias_ref, tri_ref, eid_ref, pos_ref, wts_ref, cnt_ref, carry_ref):
    i = pl.program_id(0)
    e, tn = lgt_ref.shape
    eg = e // N_EXPERT_GROUPS

    @pl.when(i == 0)
    def _():
        carry_ref[...] = jnp.zeros_like(carry_ref)

    scores = _sigmoid(lgt_ref[...])
    sel = scores + bias_ref[...]
    neg = jnp.float32(-jnp.inf)
    sub = lax.broadcasted_iota(jnp.int32, (eg, tn), 0)
    group_scores = []
    for g in range(N_EXPERT_GROUPS):
        blk = sel[g * eg:(g + 1) * eg, :]
        m1 = jnp.max(blk, axis=0, keepdims=True)
        first = jnp.min(jnp.where(blk == m1, sub, eg), axis=0, keepdims=True)
        m2 = jnp.max(jnp.where(sub == first, neg, blk), axis=0, keepdims=True)
        group_scores.append(m1 + m2)
    masked = []
    for g in range(N_EXPERT_GROUPS):
        rank = jnp.zeros((1, tn), jnp.int32)
        for g2 in range(N_EXPERT_GROUPS):
            if g2 == g:
                continue
            ahead = (group_scores[g2] >= group_scores[g]) if g2 < g else (group_scores[g2] > group_scores[g])
            rank = rank + ahead.astype(jnp.int32)
        masked.append(jnp.where(rank < TOPK_GROUPS, sel[g * eg:(g + 1) * eg, :], neg))
    selm = jnp.concatenate(masked, axis=0)

    eidx = lax.broadcasted_iota(jnp.int32, (e, tn), 0)
    erank = jnp.zeros((e, tn), jnp.int32)
    for e2 in range(e):
        rowv = selm[e2:e2 + 1, :]
        ahead = jnp.where(rowv > selm, 1, jnp.where((rowv == selm) & (eidx > e2), 1, 0))
        erank = erank + ahead
    chosen = erank < TOP_K

    m = chosen.astype(BF16)
    pos = jnp.dot(m, tri_ref[...], preferred_element_type=F32) + carry_ref[...]
    carry_ref[...] = carry_ref[...] + jnp.sum(chosen.astype(F32), axis=1, keepdims=True)
    cnt_ref[...] = carry_ref[...]

    wsum = jnp.sum(jnp.where(chosen, scores, 0.0), axis=0, keepdims=True)
    eidf = eidx.astype(F32)
    for j in range(TOP_K):
        hit = erank == j
        eid_ref[j:j + 1, :] = jnp.sum(jnp.where(hit, eidf, 0.0), axis=0, keepdims=True).astype(jnp.int32)
        pos_ref[j:j + 1, :] = jnp.sum(jnp.where(hit, pos, 0.0), axis=0, keepdims=True).astype(jnp.int32)
        wj = jnp.sum(jnp.where(hit, scores, 0.0), axis=0, keepdims=True)
        wts_ref[j:j + 1, :] = wj / wsum * ROUTED_SCALE


def _route(lgt, router_bias):
    e, t = lgt.shape
    tn = _tile(t, 512)
    tri = (lax.broadcasted_iota(jnp.int32, (tn, tn), 0) < lax.broadcasted_iota(jnp.int32, (tn, tn), 1)).astype(BF16)
    col = lambda i: (0, i)
    return pl.pallas_call(
        _route_kernel,
        grid=(t // tn,),
        in_specs=[pl.BlockSpec((e, tn), col), _const_spec((e, 1)), _const_spec((tn, tn))],
        out_specs=[pl.BlockSpec((TOP_K, tn), col), pl.BlockSpec((TOP_K, tn), col), pl.BlockSpec((TOP_K, tn), col),
                   pl.BlockSpec((e, 1), lambda i: (0, 0))],
        out_shape=[jax.ShapeDtypeStruct((TOP_K, t), jnp.int32), jax.ShapeDtypeStruct((TOP_K, t), jnp.int32),
                   jax.ShapeDtypeStruct((TOP_K, t), F32), jax.ShapeDtypeStruct((e, 1), F32)],
        scratch_shapes=[pltpu.VMEM((e, 1), F32)],
        compiler_params=_params("arbitrary"),
        name="route",
    )(lgt, router_bias.reshape(e, 1), tri)


def _dest_kernel(poff_ref, eid_ref, pos_ref, dest_ref):
    eid = eid_ref[...]
    acc = pos_ref[...]
    for e in range(poff_ref.shape[0]):
        acc = acc + jnp.where(eid == e, poff_ref[e], 0)
    dest_ref[...] = acc


def _dest_rows(poff, eid, pos):
    k, t = eid.shape
    tn = _tile(t, 2048)
    col = lambda i, *_: (0, i)
    return pl.pallas_call(
        _dest_kernel,
        grid_spec=pltpu.PrefetchScalarGridSpec(
            num_scalar_prefetch=1, grid=(t // tn,),
            in_specs=[pl.BlockSpec((k, tn), col), pl.BlockSpec((k, tn), col)],
            out_specs=pl.BlockSpec((k, tn), col)),
        out_shape=jax.ShapeDtypeStruct((k, t), jnp.int32),
        compiler_params=_params("arbitrary"),
        name="dest_rows",
    )(poff, eid, pos)


def _dispatch_kernel(pend_ref, pcnt_ref, dest_ref, h_ref, xs_ref, zero_ref, sem, zsem, *, tt, rows):
    i = pl.program_id(0)
    n_exp = pend_ref.shape[0]

    def zero_copy(e):
        start = pl.multiple_of(pend_ref[e] - rows, rows)
        return pltpu.make_async_copy(zero_ref, xs_ref.at[pl.ds(start, rows)], zsem)

    @pl.when(i == 0)
    def _():
        zero_ref[...] = jnp.zeros_like(zero_ref)

        def start_zero(e, c):
            @pl.when(pcnt_ref[e] > 0)
            def _():
                zero_copy(e).start()
            return c

        def wait_zero(e, c):
            @pl.when(pcnt_ref[e] > 0)
            def _():
                zero_copy(e).wait()
            return c

        lax.fori_loop(0, n_exp, start_zero, 0)
        lax.fori_loop(0, n_exp, wait_zero, 0)

    def row_copy(t, j):
        return pltpu.make_async_copy(h_ref.at[pl.ds(i * tt + t, 1)], xs_ref.at[pl.ds(dest_ref[j, t], 1)], sem)

    def issue(t, c):
        for j in range(TOP_K):
            row_copy(t, j).start()
        return c

    def drain(t, c):
        for j in range(TOP_K):
            row_copy(t, j).wait()
        return c

    lax.fori_loop(0, tt, issue, 0)
    lax.fori_loop(0, tt, drain, 0)


def _dispatch(h2, dest, pend, pcnt, n_rows, rows):
    t, d = h2.shape
    tt = _tile(t, 256)
    return pl.pallas_call(
        functools.partial(_dispatch_kernel, tt=tt, rows=rows),
        grid_spec=pltpu.PrefetchScalarGridSpec(
            num_scalar_prefetch=2, grid=(t // tt,),
            in_specs=[pl.BlockSpec((TOP_K, tt), lambda i, *_: (0, i), memory_space=pltpu.SMEM),
                      pl.BlockSpec(memory_space=pl.ANY)],
            out_specs=pl.BlockSpec(memory_space=pl.ANY),
            scratch_shapes=[pltpu.VMEM((rows, d), F32), pltpu.SemaphoreType.DMA, pltpu.SemaphoreType.DMA]),
        out_shape=jax.ShapeDtypeStruct((n_rows, d), F32),
        compiler_params=_params("arbitrary"),
        name="dispatch",
    )(pend, pcnt, dest, h2)


def _expert_kernel(blk_e_ref, nreal_ref, x_ref, wg_ref, wu_ref, wd_ref, y_ref, wgb_ref, wub_ref, wdb_ref):
    b = pl.program_id(0)
    changed = jnp.logical_or(b == 0, blk_e_ref[b] != blk_e_ref[jnp.maximum(b - 1, 0)])

    @pl.when(jnp.logical_and(changed, b < nreal_ref[0]))
    def _():
        wgb_ref[...] = wg_ref[...].astype(BF16)
        wub_ref[...] = wu_ref[...].astype(BF16)
        wdb_ref[...] = wd_ref[...].astype(BF16)

    @pl.when(b < nreal_ref[0])
    def _():
        x = x_ref[...].astype(BF16)
        g = jnp.dot(x, wgb_ref[...], preferred_element_type=F32)
        u = jnp.dot(x, wub_ref[...], preferred_element_type=F32)
        a = (g * _sigmoid(g) * u).astype(BF16)
        y_ref[...] = jnp.dot(a, wdb_ref[...], preferred_element_type=F32)


def _experts(xs, blk_e, n_real, w_gate, w_up, w_down, layer, rows):
    n_rows, d = xs.shape
    ff = w_gate.shape[3]
    nb = n_rows // rows
    xrow = lambda b, be, nr: (jnp.minimum(b, jnp.maximum(nr[0] - 1, 0)), 0)
    wsel = lambda b, be, nr: (layer, be[b], 0, 0)
    return pl.pallas_call(
        _expert_kernel,
        grid_spec=pltpu.PrefetchScalarGridSpec(
            num_scalar_prefetch=2, grid=(nb,),
            in_specs=[pl.BlockSpec((rows, d), xrow), pl.BlockSpec((None, None, d, ff), wsel),
                      pl.BlockSpec((None, None, d, ff), wsel), pl.BlockSpec((None, None, ff, d), wsel)],
            out_specs=pl.BlockSpec((rows, d), xrow),
            scratch_shapes=[pltpu.VMEM((d, ff), BF16), pltpu.VMEM((d, ff), BF16), pltpu.VMEM((ff, d), BF16)]),
        out_shape=jax.ShapeDtypeStruct((n_rows, d), F32),
        compiler_params=_params("arbitrary"),
        name="experts",
    )(blk_e, n_real, xs, w_gate, w_up, w_down)


def _combine_kernel(dest_ref, ys_ref, wt_ref, x1_ref, h_ref, sg_ref, su_ref, sd_ref, g2_ref, o_ref, gbuf_ref, sem,
                    *, tt):
    def row_copy(t, j):
        return pltpu.make_async_copy(ys_ref.at[pl.ds(dest_ref[j, t], 1)], gbuf_ref.at[j, pl.ds(t, 1)], sem)

    def issue(t, c):
        for j in range(TOP_K):
            row_copy(t, j).start()
        return c

    def drain(t, c):
        for j in range(TOP_K):
            row_copy(t, j).wait()
        return c

    lax.fori_loop(0, tt, issue, 0)
    h = h_ref[...].astype(BF16)
    g = jnp.dot(h, sg_ref[...], preferred_element_type=F32)
    u = jnp.dot(h, su_ref[...], preferred_element_type=F32)
    shared = jnp.dot((g * _sigmoid(g) * u).astype(BF16), sd_ref[...], preferred_element_type=F32)
    lax.fori_loop(0, tt, drain, 0)
    wt = wt_ref[...]
    routed = gbuf_ref[0] * wt[:, 0:1]
    for j in range(1, TOP_K):
        routed = routed + gbuf_ref[j] * wt[:, j:j + 1]
    o_ref[...] = x1_ref[...] + g2_ref[...] * (routed + shared)


def _combine(ys, dest, wts_t, x1, h2, s_gate, s_up, s_down, g2, seq):
    t, d = x1.shape
    sf = s_gate.shape[1]
    tt = _tile(seq, 128)
    row = lambda i, *_: (i, 0)
    return pl.pallas_call(
        functools.partial(_combine_kernel, tt=tt),
        grid=(t // tt,),
        in_specs=[
            pl.BlockSpec((TOP_K, tt), lambda i: (0, i), memory_space=pltpu.SMEM),
            pl.BlockSpec(memory_space=pl.ANY),
            pl.BlockSpec((tt, TOP_K), row), pl.BlockSpec((tt, d), row), pl.BlockSpec((tt, d), row),
            _const_spec((d, sf)), _const_spec((d, sf)), _const_spec((sf, d)),
            pl.BlockSpec((None, 1, d), lambda i: ((i * tt) // seq, 0, 0)),
        ],
        out_specs=pl.BlockSpec((tt, d), row),
        out_shape=jax.ShapeDtypeStruct((t, d), F32),
        scratch_shapes=[pltpu.VMEM((TOP_K, tt, d), F32), pltpu.SemaphoreType.DMA],
        compiler_params=_params("arbitrary"),
        name="combine",
    )(dest, ys, wts_t, x1, h2, s_gate, s_up, s_down, g2)


def _moe(x1, h2, lgt, g2, router_bias, w_gate, w_up, w_down, layer, s_gate, s_up, s_down):
    b, s, d = x1.shape
    t = b * s
    e = lgt.shape[0]
    rows = _tile(t, 256)
    eid, pos, wts, cnt = _route(lgt, router_bias)
    counts = cnt[:, 0].astype(jnp.int32)
    pcnt = (counts + rows - 1) // rows * rows
    pend = jnp.cumsum(pcnt)
    poff = pend - pcnt
    nb = (t * TOP_K + e * (rows - 1)) // rows
    n_real = (pend[-1] // rows).astype(jnp.int32).reshape(1)
    blk_e = jnp.minimum(jnp.searchsorted(pend, jnp.arange(nb, dtype=jnp.int32) * rows, side="right"),
                        e - 1).astype(jnp.int32)
    dest = _dest_rows(poff.astype(jnp.int32), eid, pos)
    h2f = h2.reshape(t, d)
    xs = _dispatch(h2f, dest, pend.astype(jnp.int32), pcnt.astype(jnp.int32), nb * rows, rows)
    ys = _experts(xs, blk_e, n_real, w_gate, w_up, w_down, layer, rows)
    out = _combine(ys, dest, wts.T, x1.reshape(t, d), h2f, s_gate.astype(BF16), s_up.astype(BF16),
                   s_down.astype(BF16), g2, s)
    return out.reshape(b, s, d)


def _rope_tables(n_tok):
    axis_dim = HEAD_DIM // 2
    rows = n_tok // GRID_W
    r, col = jnp.meshgrid(jnp.arange(rows), jnp.arange(GRID_W), indexing="ij")
    pos = jnp.stack([r.reshape(-1), col.reshape(-1)], axis=-1).astype(F32)
    inv = ROPE_THETA ** (-jnp.arange(0, axis_dim, 2, dtype=F32) / axis_dim)
    ang = pos[:, :, None] * inv
    ang = jnp.broadcast_to(ang[:, :, None, :], (n_tok, 2, 2, axis_dim // 2)).reshape(n_tok, HEAD_DIM)
    sign = jnp.where((jnp.arange(HEAD_DIM) % axis_dim) < axis_dim // 2, -1.0, 1.0).astype(F32)
    return jnp.cos(ang), jnp.sin(ang) * sign


def kernel(x, c, ctx, c_ctx, w_mod, b_mod, norm1_g, norm2_g, mix_w_in, q_norm_g, k_norm_g, conv_w, conv_b,
           conv_norm_g, conv_norm_b, mix_w_out, pool_w, pool_b, pool_scale, router_w, router_bias,
           moe_w_gate, moe_w_up, moe_w_down, shared_w_gate, shared_w_up, shared_w_down):
    b, s, d = x.shape
    depth = w_mod.shape[0]
    assert depth == 2, "layer schedule below is written for an attention layer followed by a pooling layer"
    kv_w = KV_HEADS * HEAD_DIM
    q_w = mix_w_in.shape[2] - 2 * kv_w - d

    pad = (-(b + 1)) % SUBLANES
    cc = jnp.concatenate([c, c_ctx[None, :], jnp.zeros((pad, d), F32)], axis=0)
    mods = _modulation(cc, w_mod, b_mod)

    def mod(layer, k, rows=slice(0, b)):
        return mods[layer, rows, k * d:(k + 1) * d][:, None, :]

    vec = lambda a: a.reshape(1, -1)
    rwt = jnp.swapaxes(router_w, 1, 2)
    cos, sin_signed = _rope_tables(s)

    w_in = mix_w_in[0].astype(BF16)
    q, k, v, z = _inproj(x, mod(0, 0), mod(0, 1), vec(norm1_g[0]), w_in, vec(q_norm_g[0]), vec(k_norm_g[0]),
                         cos, sin_signed)
    ctx_row = slice(b, b + 1)
    kc, vc = _ctx_kv(ctx, mods[0, ctx_row, 0:d], mods[0, ctx_row, d:2 * d], vec(norm1_g[0]),
                     w_in[:, q_w:q_w + 2 * kv_w], vec(k_norm_g[0]))
    attn = _attention(q, k, v, kc, vc)
    cv = _conformer_conv(z, conv_w[0], vec(conv_b[0]), vec(conv_norm_g[0]), vec(conv_norm_b[0]))
    w_out = mix_w_out[0].astype(BF16)
    x1, h2, lgt = _outproj(attn, cv, x, w_out[:q_w], w_out[q_w:], mod(0, 2), vec(norm2_g[0]), mod(0, 3), mod(0, 4),
                           rwt[0])
    x = _moe(x1, h2, lgt, mod(0, 5), router_bias[0], moe_w_gate, moe_w_up, moe_w_down, 0,
             shared_w_gate[0], shared_w_up[0], shared_w_down[0])

    x1, h2, lgt = _pool_layer(x, vec(norm1_g[1]), mod(1, 0), mod(1, 1), pool_w[0].astype(BF16), vec(pool_b[0]),
                              vec(pool_scale[0]), mod(1, 2), vec(norm2_g[1]), mod(1, 3), mod(1, 4), rwt[1])
    x = _moe(x1, h2, lgt, mod(1, 5), router_bias[1], moe_w_gate, moe_w_up, moe_w_down, 1,
             shared_w_gate[1], shared_w_up[1], shared_w_down[1])
    return x
```

```python
import functools

import jax
import jax.numpy as jnp
from jax import lax
from jax.experimental import pallas as pl
from jax.experimental.pallas import tpu as pltpu

HEAD_DIM = 128
KV_HEADS = 2
GRID_W = 64
ROPE_THETA = 10000.0
EPS = 1e-6
CONV_K = 31
CONV_PAD = CONV_K // 2
POOL_WINDOWS = (2, 4, 8, 16)
N_EXPERT_GROUPS = 8
TOPK_GROUPS = 4
TOP_K = 8
ROUTED_SCALE = 2.5

LANES = 128
SUBLANES = 8
HALO = 16
VMEM_LIMIT_BYTES = 56 * 1024 * 1024

F32 = jnp.float32
BF16 = jnp.bfloat16
HIGHEST = lax.Precision.HIGHEST


def _tile(n, pref):
    t = min(n, pref)
    while n % t:
        t //= 2
    return t


def _params(*sem):
    return pltpu.CompilerParams(dimension_semantics=sem, vmem_limit_bytes=VMEM_LIMIT_BYTES)


def _const_spec(shape):
    nd = len(shape)
    return pl.BlockSpec(shape, lambda *_: (0,) * nd, pipeline_mode=pl.Buffered(1))


def _sigmoid(x):
    return 1.0 / (1.0 + jnp.exp(-x))


def _adaln(x, g, shift, scale):
    ms = jnp.mean(x * x, axis=-1, keepdims=True)
    return (x * lax.rsqrt(ms + EPS) * g) * (1.0 + scale) + shift


def _mod_kernel(c_ref, w_ref, b_ref, o_ref):
    c = c_ref[...]
    a = c * _sigmoid(c)
    o_ref[...] = jnp.dot(a, w_ref[...], preferred_element_type=F32, precision=HIGHEST) + b_ref[...]


def _modulation(cc, w_mod, b_mod):
    n_layers, d, n = w_mod.shape
    r = cc.shape[0]
    tn = _tile(n, 1024)
    return pl.pallas_call(
        _mod_kernel,
        grid=(n_layers, n // tn),
        in_specs=[
            pl.BlockSpec((r, d), lambda l, j: (0, 0)),
            pl.BlockSpec((None, d, tn), lambda l, j: (l, 0, j)),
            pl.BlockSpec((None, 1, tn), lambda l, j: (l, 0, j)),
        ],
        out_specs=pl.BlockSpec((None, r, tn), lambda l, j: (l, 0, j)),
        out_shape=jax.ShapeDtypeStruct((n_layers, r, n), F32),
        compiler_params=_params("arbitrary", "arbitrary"),
        name="modulation",
    )(cc, w_mod, b_mod.reshape(n_layers, 1, n))


def _head_norm(x, g):
    ms = jnp.mean(x * x, axis=-1, keepdims=True)
    return x * lax.rsqrt(ms + EPS) * g


def _rope(x, cos, sin_signed, first_half):
    up = pltpu.roll(x, HEAD_DIM - HEAD_DIM // 4, 1)
    dn = pltpu.roll(x, HEAD_DIM // 4, 1)
    return x * cos + jnp.where(first_half, up, dn) * sin_signed


def _inproj_kernel(x_ref, sh_ref, sc_ref, g_ref, w_ref, qg_ref, kg_ref, cos_ref, sin_ref,
                   q_ref, k_ref, v_ref, z_ref, *, n_heads, conv_c):
    h = _adaln(x_ref[...], g_ref[...], sh_ref[...], sc_ref[...]).astype(BF16)
    acc = jnp.dot(h, w_ref[...], preferred_element_type=F32)
    cos = cos_ref[...]
    sin = sin_ref[...]
    lane = lax.broadcasted_iota(jnp.int32, cos.shape, 1)
    first_half = (lane % (HEAD_DIM // 2)) < (HEAD_DIM // 4)
    q_w = n_heads * HEAD_DIM
    kv_w = KV_HEADS * HEAD_DIM
    q_scale = HEAD_DIM ** -0.5
    for hd in range(n_heads):
        sl = slice(hd * HEAD_DIM, (hd + 1) * HEAD_DIM)
        qn = _head_norm(acc[:, sl], qg_ref[...])
        q_ref[:, sl] = (_rope(qn, cos, sin, first_half) * q_scale).astype(q_ref.dtype)
    for hd in range(KV_HEADS):
        sl = slice(hd * HEAD_DIM, (hd + 1) * HEAD_DIM)
        kn = _head_norm(acc[:, q_w + hd * HEAD_DIM:q_w + (hd + 1) * HEAD_DIM], kg_ref[...])
        k_ref[:, sl] = _rope(kn, cos, sin, first_half).astype(k_ref.dtype)
    v_ref[...] = acc[:, q_w + kv_w:q_w + 2 * kv_w].astype(v_ref.dtype)
    u0 = q_w + 2 * kv_w
    a = acc[:, u0:u0 + conv_c]
    gate = acc[:, u0 + conv_c:u0 + 2 * conv_c]
    z_ref[...] = a * _sigmoid(gate)


def _inproj(x, sh, sc, g, w_in, q_g, k_g, cos, sin_signed):
    b, s, d = x.shape
    in_w = w_in.shape[1]
    kv_w = KV_HEADS * HEAD_DIM
    conv_c = d // 2
    q_w = in_w - 2 * kv_w - 2 * conv_c
    n_heads = q_w // HEAD_DIM
    tm = _tile(s, 512)
    row = lambda bi, i: (bi, i, 0)
    vec = pl.BlockSpec((None, 1, d), lambda bi, i: (bi, 0, 0))
    tab = pl.BlockSpec((tm, HEAD_DIM), lambda bi, i: (i, 0))
    return pl.pallas_call(
        functools.partial(_inproj_kernel, n_heads=n_heads, conv_c=conv_c),
        grid=(b, s // tm),
        in_specs=[
            pl.BlockSpec((None, tm, d), row), vec, vec, _const_spec((1, d)), _const_spec((d, in_w)),
            _const_spec((1, HEAD_DIM)), _const_spec((1, HEAD_DIM)), tab, tab,
        ],
        out_specs=[
            pl.BlockSpec((None, tm, q_w), row), pl.BlockSpec((None, tm, kv_w), row),
            pl.BlockSpec((None, tm, kv_w), row), pl.BlockSpec((None, tm, conv_c), row),
        ],
        out_shape=[
            jax.ShapeDtypeStruct((b, s, q_w), BF16), jax.ShapeDtypeStruct((b, s, kv_w), BF16),
            jax.ShapeDtypeStruct((b, s, kv_w), BF16), jax.ShapeDtypeStruct((b, s, conv_c), F32),
        ],
        compiler_params=_params("arbitrary", "arbitrary"),
        name="inproj",
    )(x, sh, sc, g, w_in, q_g, k_g, cos, sin_signed)


def _ctx_kv_kernel(x_ref, sh_ref, sc_ref, g_ref, w_ref, kg_ref, k_ref, v_ref):
    h = _adaln(x_ref[...], g_ref[...], sh_ref[...], sc_ref[...]).astype(BF16)
    acc = jnp.dot(h, w_ref[...], preferred_element_type=F32)
    kv_w = KV_HEADS * HEAD_DIM
    for hd in range(KV_HEADS):
        sl = slice(hd * HEAD_DIM, (hd + 1) * HEAD_DIM)
        k_ref[:, sl] = _head_norm(acc[:, sl], kg_ref[...]).astype(k_ref.dtype)
    v_ref[...] = acc[:, kv_w:2 * kv_w].astype(v_ref.dtype)


def _ctx_kv(ctx, sh, sc, g, w_kv, k_g):
    b, lc, d = ctx.shape
    kv_w = KV_HEADS * HEAD_DIM
    row = lambda bi: (bi, 0, 0)
    return pl.pallas_call(
        _ctx_kv_kernel,
        grid=(b,),
        in_specs=[
            pl.BlockSpec((None, lc, d), row), _const_spec((1, d)), _const_spec((1, d)), _const_spec((1, d)),
            _const_spec((d, 2 * kv_w)), _const_spec((1, HEAD_DIM)),
        ],
        out_specs=[pl.BlockSpec((None, lc, kv_w), row), pl.BlockSpec((None, lc, kv_w), row)],
        out_shape=[jax.ShapeDtypeStruct((b, lc, kv_w), BF16), jax.ShapeDtypeStruct((b, lc, kv_w), BF16)],
        compiler_params=_params("arbitrary"),
        name="ctx_kv",
    )(ctx, sh, sc, g, w_kv, k_g)


def _attn_kernel(q_ref, k_ref, v_ref, kc_ref, vc_ref, o_ref):
    q = q_ref[...]
    nt = (((1,), (1,)), ((), ()))
    s1 = lax.dot_general(q, k_ref[...], nt, preferred_element_type=F32)
    s2 = lax.dot_general(q, kc_ref[...], nt, preferred_element_type=F32)
    m = jnp.maximum(jnp.max(s1, axis=-1, keepdims=True), jnp.max(s2, axis=-1, keepdims=True))
    p1 = jnp.exp(s1 - m)
    p2 = jnp.exp(s2 - m)
    l = jnp.sum(p1, axis=-1, keepdims=True) + jnp.sum(p2, axis=-1, keepdims=True)
    o = jnp.dot(p1.astype(BF16), v_ref[...], preferred_element_type=F32)
    o = o + jnp.dot(p2.astype(BF16), vc_ref[...], preferred_element_type=F32)
    o_ref[...] = (o / l).astype(o_ref.dtype)


def _attention(q, k, v, kc, vc):
    b, s, q_w = q.shape
    lc = kc.shape[1]
    group = q_w // (KV_HEADS * HEAD_DIM)
    tq = _tile(s, 512)
    kv_spec = lambda n: pl.BlockSpec((None, n, HEAD_DIM), lambda bi, h, g, i: (bi, 0, h))
    q_spec = pl.BlockSpec((None, tq, HEAD_DIM), lambda bi, h, g, i: (bi, i, h * group + g))
    return pl.pallas_call(
        _attn_kernel,
        grid=(b, KV_HEADS, group, s // tq),
        in_specs=[q_spec, kv_spec(s), kv_spec(s), kv_spec(lc), kv_spec(lc)],
        out_specs=q_spec,
        out_shape=jax.ShapeDtypeStruct((b, s, q_w), BF16),
        compiler_params=_params("arbitrary", "arbitrary", "arbitrary", "arbitrary"),
        name="attention",
    )(q, k, v, kc, vc)


def _conv_kernel(zc_ref, zp_ref, zn_ref, w_ref, b_ref, g_ref, beta_ref, o_ref, win_ref, conv_ref,
                 *, ts, n_tiles, rb, cb):
    i = pl.program_id(1)
    c = zc_ref.shape[1]
    win_ref[HALO:HALO + ts, :] = zc_ref[...]
    win_ref[0:HALO, :] = jnp.where(i > 0, zp_ref[...], 0.0)
    win_ref[HALO + ts:2 * HALO + ts, :] = jnp.where(i < n_tiles - 1, zn_ref[...], 0.0)
    base = HALO - CONV_PAD
    for r in range(ts // rb):
        for cc in range(c // cb):
            cs = slice(cc * cb, (cc + 1) * cb)
            acc = jnp.zeros((rb, cb), F32)
            for kk in range(CONV_K):
                r0 = r * rb + base + kk
                acc = acc + win_ref[r0:r0 + rb, cs] * w_ref[kk:kk + 1, cs]
            conv_ref[r * rb:(r + 1) * rb, cs] = acc + b_ref[:, cs]
    z = conv_ref[...]
    mu = jnp.mean(z, axis=-1, keepdims=True)
    zc = z - mu
    var = jnp.mean(zc * zc, axis=-1, keepdims=True)
    y = zc * lax.rsqrt(var + EPS) * g_ref[...] + beta_ref[...]
    o_ref[...] = (y * _sigmoid(y)).astype(o_ref.dtype)


def _conformer_conv(z, conv_w, conv_b, cn_g, cn_b):
    b, s, c = z.shape
    ts = _tile(s, 128)
    n_tiles = s // ts
    hb = ts // HALO
    n_hb = s // HALO
    cur = lambda bi, i: (bi, i, 0)
    prev = lambda bi, i: (bi, jnp.maximum(i * hb - 1, 0), 0)
    nxt = lambda bi, i: (bi, jnp.minimum((i + 1) * hb, n_hb - 1), 0)
    return pl.pallas_call(
        functools.partial(_conv_kernel, ts=ts, n_tiles=n_tiles, rb=_tile(ts, 32), cb=_tile(c, 512)),
        grid=(b, n_tiles),
        in_specs=[
            pl.BlockSpec((None, ts, c), cur), pl.BlockSpec((None, HALO, c), prev), pl.BlockSpec((None, HALO, c), nxt),
            _const_spec((CONV_K, c)), _const_spec((1, c)), _const_spec((1, c)), _const_spec((1, c)),
        ],
        out_specs=pl.BlockSpec((None, ts, c), cur),
        out_shape=jax.ShapeDtypeStruct((b, s, c), BF16),
        scratch_shapes=[pltpu.VMEM((ts + 2 * HALO, c), F32), pltpu.VMEM((ts, c), F32)],
        compiler_params=_params("arbitrary", "arbitrary"),
        name="conformer_conv",
    )(z, z, z, conv_w, conv_b, cn_g, cn_b)


def _finish_tile(xnew, g2n_ref, sh2_ref, sc2_ref, rwt_ref, x1_ref, h2_ref, lgt_ref):
    x1_ref[...] = xnew
    h2 = _adaln(xnew, g2n_ref[...], sh2_ref[...], sc2_ref[...])
    h2_ref[...] = h2
    lgt_ref[...] = lax.dot_general(rwt_ref[...], h2, (((1,), (1,)), ((), ())),
                                   preferred_element_type=F32, precision=HIGHEST)


def _outproj_kernel(a_ref, c_ref, x_ref, wa_ref, wc_ref, g1_ref, g2n_ref, sh2_ref, sc2_ref, rwt_ref,
                    x1_ref, h2_ref, lgt_ref):
    y = jnp.dot(a_ref[...], wa_ref[...], preferred_element_type=F32)
    y = y + jnp.dot(c_ref[...], wc_ref[...], preferred_element_type=F32)
    _finish_tile(x_ref[...] + g1_ref[...] * y, g2n_ref, sh2_ref, sc2_ref, rwt_ref, x1_ref, h2_ref, lgt_ref)


def _finish_specs(b, s, d, e, tm):
    nt = s // tm
    row = lambda bi, i: (bi, i, 0)
    out_specs = [pl.BlockSpec((None, tm, d), row), pl.BlockSpec((None, tm, d), row),
                 pl.BlockSpec((e, tm), lambda bi, i: (0, bi * nt + i))]
    out_shape = [jax.ShapeDtypeStruct((b, s, d), F32), jax.ShapeDtypeStruct((b, s, d), F32),
                 jax.ShapeDtypeStruct((e, b * s), F32)]
    return out_specs, out_shape


def _outproj(attn, cv, x, w_a, w_c, g1, g2n, sh2, sc2, rwt):
    b, s, d = x.shape
    e = rwt.shape[0]
    tm = _tile(s, 512)
    row = lambda bi, i: (bi, i, 0)
    vec = pl.BlockSpec((None, 1, d), lambda bi, i: (bi, 0, 0))
    out_specs, out_shape = _finish_specs(b, s, d, e, tm)
    return pl.pallas_call(
        _outproj_kernel,
        grid=(b, s // tm),
        in_specs=[
            pl.BlockSpec((None, tm, attn.shape[2]), row), pl.BlockSpec((None, tm, cv.shape[2]), row),
            pl.BlockSpec((None, tm, d), row), _const_spec(w_a.shape), _const_spec(w_c.shape),
            vec, _const_spec((1, d)), vec, vec, _const_spec((e, d)),
        ],
        out_specs=out_specs,
        out_shape=out_shape,
        compiler_params=_params("arbitrary", "arbitrary"),
        name="outproj",
    )(attn, cv, x, w_a, w_c, g1, g2n, sh2, sc2, rwt)


def _pool_kernel(xc_ref, xp_ref, xn_ref, g1n_ref, sh1_ref, sc1_ref, pw_ref, pb_ref, ps_ref, g1_ref,
                 g2n_ref, sh2_ref, sc2_ref, rwt_ref, x1_ref, h2_ref, lgt_ref, win_ref, *, tm, n_tiles, seq):
    i = pl.program_id(1)
    d = xc_ref.shape[1]
    gc = d // len(POOL_WINDOWS)
    norm = lambda x: _adaln(x, g1n_ref[...], sh1_ref[...], sc1_ref[...])
    win_ref[HALO:HALO + tm, :] = norm(xc_ref[...])
    win_ref[0:HALO, :] = jnp.where(i > 0, norm(xp_ref[...]), 0.0)
    win_ref[HALO + tm:2 * HALO + tm, :] = jnp.where(i < n_tiles - 1, norm(xn_ref[...]), 0.0)
    t = i * tm + lax.broadcasted_iota(jnp.int32, (tm, 1), 0)
    ys = []
    for gi, w in enumerate(POOL_WINDOWS):
        cs = slice(gi * gc, (gi + 1) * gc)
        tot = jnp.zeros((tm, gc), F32)
        for off in range(-(w // 2), w - w // 2):
            tot = tot + win_ref[HALO + off:HALO + off + tm, cs]
        lo = jnp.clip(t - w // 2, 0, seq)
        hi = jnp.clip(t + w - w // 2, 0, seq)
        p = tot / (hi - lo).astype(F32) - win_ref[HALO:HALO + tm, cs]
        ys.append(jnp.dot(p.astype(BF16), pw_ref[gi], preferred_element_type=F32))
    y = (jnp.concatenate(ys, axis=-1) + pb_ref[...]) * ps_ref[...]
    _finish_tile(xc_ref[...] + g1_ref[...] * y, g2n_ref, sh2_ref, sc2_ref, rwt_ref, x1_ref, h2_ref, lgt_ref)


def _pool_layer(x, g1n, sh1, sc1, pool_w, pool_b, pool_scale, g1, g2n, sh2, sc2, rwt):
    b, s, d = x.shape
    e = rwt.shape[0]
    tm = _tile(s, 256)
    n_tiles = s // tm
    hb = tm // HALO
    n_hb = s // HALO
    cur = lambda bi, i: (bi, i, 0)
    prev = lambda bi, i: (bi, jnp.maximum(i * hb - 1, 0), 0)
    nxt = lambda bi, i: (bi, jnp.minimum((i + 1) * hb, n_hb - 1), 0)
    vec = pl.BlockSpec((None, 1, d), lambda bi, i: (bi, 0, 0))
    cvec = _const_spec((1, d))
    out_specs, out_shape = _finish_specs(b, s, d, e, tm)
    return pl.pallas_call(
        functools.partial(_pool_kernel, tm=tm, n_tiles=n_tiles, seq=s),
        grid=(b, n_tiles),
        in_specs=[
            pl.BlockSpec((None, tm, d), cur), pl.BlockSpec((None, HALO, d), prev), pl.BlockSpec((None, HALO, d), nxt),
            cvec, vec, vec, _const_spec(pool_w.shape), cvec, cvec, vec, cvec, vec, vec, _const_spec((e, d)),
        ],
        out_specs=out_specs,
        out_shape=out_shape,
        scratch_shapes=[pltpu.VMEM((tm + 2 * HALO, d), F32)],
        compiler_params=_params("arbitrary", "arbitrary"),
        name="pool_mixer",
    )(x, x, x, g1n, sh1, sc1, pool_w, pool_b, pool_scale, g1, g2n, sh2, sc2, rwt)


def _route_kernel(lgt_ref, bias_ref, tri_ref, eid_ref, pos_ref, wts_ref, cnt_ref, carry_ref):
    i = pl.program_id(0)
    e, tn = lgt_ref.shape
    eg = e // N_EXPERT_GROUPS

    @pl.when(i == 0)
    def _():
        carry_ref[...] = jnp.zeros_like(carry_ref)

    scores = _sigmoid(lgt_ref[...])
    sel = scores + bias_ref[...]
    neg = jnp.float32(-jnp.inf)
    sub = lax.broadcasted_iota(jnp.int32, (eg, tn), 0)
    group_scores = []
    for g in range(N_EXPERT_GROUPS):
        blk = sel[g * eg:(g + 1) * eg, :]
        m1 = jnp.max(blk, axis=0, keepdims=True)
        first = jnp.min(jnp.where(blk == m1, sub, eg), axis=0, keepdims=True)
        m2 = jnp.max(jnp.where(sub == first, neg, blk), axis=0, keepdims=True)
        group_scores.append(m1 + m2)
    masked = []
    for g in range(N_EXPERT_GROUPS):
        rank = jnp.zeros((1, tn), jnp.int32)
        for g2 in range(N_EXPERT_GROUPS):
            if g2 == g:
                continue
            ahead = (group_scores[g2] >= group_scores[g]) if g2 < g else (group_scores[g2] > group_scores[g])
            rank = rank + ahead.astype(jnp.int32)
        masked.append(jnp.where(rank < TOPK_GROUPS, sel[g * eg:(g + 1) * eg, :], neg))
    selm = jnp.concatenate(masked, axis=0)

    eidx = lax.broadcasted_iota(jnp.int32, (e, tn), 0)
    erank = jnp.zeros((e, tn), jnp.int32)
    for e2 in range(e):
        rowv = selm[e2:e2 + 1, :]
        ahead = jnp.where(rowv > selm, 1, jnp.where((rowv == selm) & (eidx > e2), 1, 0))
        erank = erank + ahead
    chosen = erank < TOP_K

    m = chosen.astype(BF16)
    pos = jnp.dot(m, tri_ref[...], preferred_element_type=F32) + carry_ref[...]
    carry_ref[...] = carry_ref[...] + jnp.sum(chosen.astype(F32), axis=1, keepdims=True)
    cnt_ref[...] = carry_ref[...]

    wsum = jnp.sum(jnp.where(chosen, scores, 0.0), axis=0, keepdims=True)
    eidf = eidx.astype(F32)
    for j in range(TOP_K):
        hit = erank == j
        eid_ref[j:j + 1, :] = jnp.sum(jnp.where(hit, eidf, 0.0), axis=0, keepdims=True).astype(jnp.int32)
        pos_ref[j:j + 1, :] = jnp.sum(jnp.where(hit, pos, 0.0), axis=0, keepdims=True).astype(jnp.int32)
        wj = jnp.sum(jnp.where(hit, scores, 0.0), axis=0, keepdims=True)
        wts_ref[j:j + 1, :] = wj / wsum * ROUTED_SCALE


def _route(lgt, router_bias):
    e, t = lgt.shape
    tn = _tile(t, 512)
    tri = (lax.broadcasted_iota(jnp.int32, (tn, tn), 0) < lax.broadcasted_iota(jnp.int32, (tn, tn), 1)).astype(BF16)
    col = lambda i: (0, i)
    return pl.pallas_call(
        _route_kernel,
        grid=(t // tn,),
        in_specs=[pl.BlockSpec((e, tn), col), _const_spec((e, 1)), _const_spec((tn, tn))],
        out_specs=[pl.BlockSpec((TOP_K, tn), col), pl.BlockSpec((TOP_K, tn), col), pl.BlockSpec((TOP_K, tn), col),
                   pl.BlockSpec((e, 1), lambda i: (0, 0))],
        out_shape=[jax.ShapeDtypeStruct((TOP_K, t), jnp.int32), jax.ShapeDtypeStruct((TOP_K, t), jnp.int32),
                   jax.ShapeDtypeStruct((TOP_K, t), F32), jax.ShapeDtypeStruct((e, 1), F32)],
        scratch_shapes=[pltpu.VMEM((e, 1), F32)],
        compiler_params=_params("arbitrary"),
        name="route",
    )(lgt, router_bias.reshape(e, 1), tri)


def _dest_kernel(poff_ref, eid_ref, pos_ref, dest_ref):
    eid = eid_ref[...]
    acc = pos_ref[...]
    for e in range(poff_ref.shape[0]):
        acc = acc + jnp.where(eid == e, poff_ref[e], 0)
    dest_ref[...] = acc


def _dest_rows(poff, eid, pos):
    k, t = eid.shape
    tn = _tile(t, 2048)
    col = lambda i, *_: (0, i)
    return pl.pallas_call(
        _dest_kernel,
        grid_spec=pltpu.PrefetchScalarGridSpec(
            num_scalar_prefetch=1, grid=(t // tn,),
            in_specs=[pl.BlockSpec((k, tn), col), pl.BlockSpec((k, tn), col)],
            out_specs=pl.BlockSpec((k, tn), col)),
        out_shape=jax.ShapeDtypeStruct((k, t), jnp.int32),
        compiler_params=_params("arbitrary"),
        name="dest_rows",
    )(poff, eid, pos)


def _dispatch_kernel(pend_ref, pcnt_ref, dest_ref, h_ref, xs_ref, zero_ref, sem, zsem, *, tt, rows):
    i = pl.program_id(0)
    n_exp = pend_ref.shape[0]

    def zero_copy(e):
        start = pl.multiple_of(pend_ref[e] - rows, rows)
        return pltpu.make_async_copy(zero_ref, xs_ref.at[pl.ds(start, rows)], zsem)

    @pl.when(i == 0)
    def _():
        zero_ref[...] = jnp.zeros_like(zero_ref)

        def start_zero(e, c):
            @pl.when(pcnt_ref[e] > 0)
            def _():
                zero_copy(e).start()
            return c

        def wait_zero(e, c):
            @pl.when(pcnt_ref[e] > 0)
            def _():
                zero_copy(e).wait()
            return c

        lax.fori_loop(0, n_exp, start_zero, 0)
        lax.fori_loop(0, n_exp, wait_zero, 0)

    def row_copy(t, j):
        return pltpu.make_async_copy(h_ref.at[pl.ds(t, 1)], xs_ref.at[pl.ds(dest_ref[j, t], 1)], sem)

    def issue(t, c):
        for j in range(TOP_K):
            row_copy(t, j).start()
        return c

    def drain(t, c):
        for j in range(TOP_K):
            row_copy(t, j).wait()
        return c

    lax.fori_loop(0, tt, issue, 0)
    lax.fori_loop(0, tt, drain, 0)


def _dispatch(h2, dest, pend, pcnt, n_rows, rows):
    t, d = h2.shape
    tt = _tile(t, 256)
    return pl.pallas_call(
        functools.partial(_dispatch_kernel, tt=tt, rows=rows),
        grid_spec=pltpu.PrefetchScalarGridSpec(
            num_scalar_prefetch=2, grid=(t // tt,),
            in_specs=[pl.BlockSpec((TOP_K, tt), lambda i, *_: (0, i), memory_space=pltpu.SMEM),
                      pl.BlockSpec((tt, d), lambda i, *_: (i, 0))],
            out_specs=pl.BlockSpec(memory_space=pl.ANY),
            scratch_shapes=[pltpu.VMEM((rows, d), F32), pltpu.SemaphoreType.DMA, pltpu.SemaphoreType.DMA]),
        out_shape=jax.ShapeDtypeStruct((n_rows, d), F32),
        compiler_params=_params("arbitrary"),
        name="dispatch",
    )(pend, pcnt, dest, h2)


def _expert_kernel(blk_e_ref, nreal_ref, x_ref, wg_ref, wu_ref, wd_ref, y_ref, wgb_ref, wub_ref, wdb_ref):
    b = pl.program_id(0)
    changed = jnp.logical_or(b == 0, blk_e_ref[b] != blk_e_ref[jnp.maximum(b - 1, 0)])

    @pl.when(jnp.logical_and(changed, b < nreal_ref[0]))
    def _():
        wgb_ref[...] = wg_ref[...].astype(BF16)
        wub_ref[...] = wu_ref[...].astype(BF16)
        wdb_ref[...] = wd_ref[...].astype(BF16)

    @pl.when(b < nreal_ref[0])
    def _():
        x = x_ref[...].astype(BF16)
        g = jnp.dot(x, wgb_ref[...], preferred_element_type=F32)
        u = jnp.dot(x, wub_ref[...], preferred_element_type=F32)
        a = (g * _sigmoid(g) * u).astype(BF16)
        y_ref[...] = jnp.dot(a, wdb_ref[...], preferred_element_type=F32)


def _experts(xs, blk_e, n_real, w_gate, w_up, w_down, layer, rows):
    n_rows, d = xs.shape
    ff = w_gate.shape[3]
    nb = n_rows // rows
    xrow = lambda b, be, nr: (jnp.minimum(b, jnp.maximum(nr[0] - 1, 0)), 0)
    wsel = lambda b, be, nr: (layer, be[b], 0, 0)
    return pl.pallas_call(
        _expert_kernel,
        grid_spec=pltpu.PrefetchScalarGridSpec(
            num_scalar_prefetch=2, grid=(nb,),
            in_specs=[pl.BlockSpec((rows, d), xrow), pl.BlockSpec((None, None, d, ff), wsel),
                      pl.BlockSpec((None, None, d, ff), wsel), pl.BlockSpec((None, None, ff, d), wsel)],
            out_specs=pl.BlockSpec((rows, d), xrow),
            scratch_shapes=[pltpu.VMEM((d, ff), BF16), pltpu.VMEM((d, ff), BF16), pltpu.VMEM((ff, d), BF16)]),
        out_shape=jax.ShapeDtypeStruct((n_rows, d), F32),
        compiler_params=_params("arbitrary"),
        name="experts",
    )(blk_e, n_real, xs, w_gate, w_up, w_down)


def _combine_kernel(dest_ref, ys_ref, wt_ref, x1_ref, h_ref, sg_ref, su_ref, sd_ref, g2_ref, o_ref, gbuf_ref, sem,
                    *, tt):
    def row_copy(t, j):
        return pltpu.make_async_copy(ys_ref.at[pl.ds(dest_ref[j, t], 1)], gbuf_ref.at[j, pl.ds(t, 1)], sem)

    def issue(t, c):
        for j in range(TOP_K):
            row_copy(t, j).start()
        return c

    def drain(t, c):
        for j in range(TOP_K):
            row_copy(t, j).wait()
        return c

    lax.fori_loop(0, tt, issue, 0)
    h = h_ref[...].astype(BF16)
    g = jnp.dot(h, sg_ref[...], preferred_element_type=F32)
    u = jnp.dot(h, su_ref[...], preferred_element_type=F32)
    shared = jnp.dot((g * _sigmoid(g) * u).astype(BF16), sd_ref[...], preferred_element_type=F32)
    lax.fori_loop(0, tt, drain, 0)
    wt = wt_ref[...]
    routed = gbuf_ref[0] * wt[:, 0:1]
    for j in range(1, TOP_K):
        routed = routed + gbuf_ref[j] * wt[:, j:j + 1]
    o_ref[...] = x1_ref[...] + g2_ref[...] * (routed + shared)


def _combine(ys, dest, wts_t, x1, h2, s_gate, s_up, s_down, g2, seq):
    t, d = x1.shape
    sf = s_gate.shape[1]
    tt = _tile(seq, 128)
    row = lambda i, *_: (i, 0)
    return pl.pallas_call(
        functools.partial(_combine_kernel, tt=tt),
        grid=(t // tt,),
        in_specs=[
            pl.BlockSpec((TOP_K, tt), lambda i: (0, i), memory_space=pltpu.SMEM),
            pl.BlockSpec(memory_space=pl.ANY),
            pl.BlockSpec((tt, TOP_K), row), pl.BlockSpec((tt, d), row), pl.BlockSpec((tt, d), row),
            _const_spec((d, sf)), _const_spec((d, sf)), _const_spec((sf, d)),
            pl.BlockSpec((None, 1, d), lambda i: ((i * tt) // seq, 0, 0)),
        ],
        out_specs=pl.BlockSpec((tt, d), row),
        out_shape=jax.ShapeDtypeStruct((t, d), F32),
        scratch_shapes=[pltpu.VMEM((TOP_K, tt, d), F32), pltpu.SemaphoreType.DMA],
        compiler_params=_params("arbitrary"),
        name="combine",
    )(dest, ys, wts_t, x1, h2, s_gate, s_up, s_down, g2)


def _moe(x1, h2, lgt, g2, router_bias, w_gate, w_up, w_down, layer, s_gate, s_up, s_down):
    b, s, d = x1.shape
    t = b * s
    e = lgt.shape[0]
    rows = _tile(t, 256)
    eid, pos, wts, cnt = _route(lgt, router_bias)
    counts = cnt[:, 0].astype(jnp.int32)
    pcnt = (counts + rows - 1) // rows * rows
    pend = jnp.cumsum(pcnt)
    poff = pend - pcnt
    nb = (t * TOP_K + e * (rows - 1)) // rows
    n_real = (pend[-1] // rows).astype(jnp.int32).reshape(1)
    starts = jnp.arange(nb, dtype=jnp.int32) * rows
    blk_e = jnp.minimum(jnp.sum((pend[None, :] <= starts[:, None]).astype(jnp.int32), axis=1), e - 1)
    dest = _dest_rows(poff.astype(jnp.int32), eid, pos)
    h2f = h2.reshape(t, d)
    xs = _dispatch(h2f, dest, pend.astype(jnp.int32), pcnt.astype(jnp.int32), nb * rows, rows)
    ys = _experts(xs, blk_e, n_real, w_gate, w_up, w_down, layer, rows)
    out = _combine(ys, dest, wts.T, x1.reshape(t, d), h2f, s_gate.astype(BF16), s_up.astype(BF16),
                   s_down.astype(BF16), g2, s)
    return out.reshape(b, s, d)


def _rope_tables(n_tok):
    axis_dim = HEAD_DIM // 2
    rows = n_tok // GRID_W
    r, col = jnp.meshgrid(jnp.arange(rows), jnp.arange(GRID_W), indexing="ij")
    pos = jnp.stack([r.reshape(-1), col.reshape(-1)], axis=-1).astype(F32)
    inv = ROPE_THETA ** (-jnp.arange(0, axis_dim, 2, dtype=F32) / axis_dim)
    ang = pos[:, :, None] * inv
    ang = jnp.broadcast_to(ang[:, :, None, :], (n_tok, 2, 2, axis_dim // 2)).reshape(n_tok, HEAD_DIM)
    sign = jnp.where((jnp.arange(HEAD_DIM) % axis_dim) < axis_dim // 2, -1.0, 1.0).astype(F32)
    return jnp.cos(ang), jnp.sin(ang) * sign


def kernel(x, c, ctx, c_ctx, w_mod, b_mod, norm1_g, norm2_g, mix_w_in, q_norm_g, k_norm_g, conv_w, conv_b,
           conv_norm_g, conv_norm_b, mix_w_out, pool_w, pool_b, pool_scale, router_w, router_bias,
           moe_w_gate, moe_w_up, moe_w_down, shared_w_gate, shared_w_up, shared_w_down):
    b, s, d = x.shape
    depth = w_mod.shape[0]
    assert depth == 2, "layer schedule below is written for an attention layer followed by a pooling layer"
    kv_w = KV_HEADS * HEAD_DIM
    q_w = mix_w_in.shape[2] - 2 * kv_w - d

    pad = (-(b + 1)) % SUBLANES
    cc = jnp.concatenate([c, c_ctx[None, :], jnp.zeros((pad, d), F32)], axis=0)
    mods = _modulation(cc, w_mod, b_mod)

    def mod(layer, k, rows=slice(0, b)):
        return mods[layer, rows, k * d:(k + 1) * d][:, None, :]

    vec = lambda a: a.reshape(1, -1)
    rwt = jnp.swapaxes(router_w, 1, 2)
    cos, sin_signed = _rope_tables(s)

    w_in = mix_w_in[0].astype(BF16)
    q, k, v, z = _inproj(x, mod(0, 0), mod(0, 1), vec(norm1_g[0]), w_in, vec(q_norm_g[0]), vec(k_norm_g[0]),
                         cos, sin_signed)
    ctx_row = slice(b, b + 1)
    kc, vc = _ctx_kv(ctx, mods[0, ctx_row, 0:d], mods[0, ctx_row, d:2 * d], vec(norm1_g[0]),
                     w_in[:, q_w:q_w + 2 * kv_w], vec(k_norm_g[0]))
    attn = _attention(q, k, v, kc, vc)
    cv = _conformer_conv(z, conv_w[0], vec(conv_b[0]), vec(conv_norm_g[0]), vec(conv_norm_b[0]))
    w_out = mix_w_out[0].astype(BF16)
    x1, h2, lgt = _outproj(attn, cv, x, w_out[:q_w], w_out[q_w:], mod(0, 2), vec(norm2_g[0]), mod(0, 3), mod(0, 4),
                           rwt[0])
    x = _moe(x1, h2, lgt, mod(0, 5), router_bias[0], moe_w_gate, moe_w_up, moe_w_down, 0,
             shared_w_gate[0], shared_w_up[0], shared_w_down[0])

    x1, h2, lgt = _pool_layer(x, vec(norm1_g[1]), mod(1, 0), mod(1, 1), pool_w[0].astype(BF16), vec(pool_b[0]),
                              vec(pool_scale[0]), mod(1, 2), vec(norm2_g[1]), mod(1, 3), mod(1, 4), rwt[1])
    x = _moe(x1, h2, lgt, mod(1, 5), router_bias[1], moe_w_gate, moe_w_up, moe_w_down, 1,
             shared_w_gate[1], shared_w_up[1], shared_w_down[1])
    return x
```

```python
import functools

import jax
import jax.numpy as jnp
from jax import lax
from jax.experimental import pallas as pl
from jax.experimental.pallas import tpu as pltpu

HEAD_DIM = 128
KV_HEADS = 2
GRID_W = 64
ROPE_THETA = 10000.0
EPS = 1e-6
CONV_K = 31
CONV_PAD = CONV_K // 2
POOL_WINDOWS = (2, 4, 8, 16)
N_EXPERT_GROUPS = 8
TOPK_GROUPS = 4
TOP_K = 8
ROUTED_SCALE = 2.5

LANES = 128
SUBLANES = 8
HALO = 16
VMEM_LIMIT_BYTES = 56 * 1024 * 1024

F32 = jnp.float32
BF16 = jnp.bfloat16
HIGHEST = lax.Precision.HIGHEST


def _tile(n, pref):
    t = min(n, pref)
    while n % t:
        t //= 2
    return t


def _params(*sem):
    return pltpu.CompilerParams(dimension_semantics=sem, vmem_limit_bytes=VMEM_LIMIT_BYTES)


def _const_spec(shape):
    nd = len(shape)
    return pl.BlockSpec(shape, lambda *_: (0,) * nd, pipeline_mode=pl.Buffered(1))


def _sigmoid(x):
    return 1.0 / (1.0 + jnp.exp(-x))


def _adaln(x, g, shift, scale):
    ms = jnp.mean(x * x, axis=-1, keepdims=True)
    return (x * lax.rsqrt(ms + EPS) * g) * (1.0 + scale) + shift


def _mod_kernel(c_ref, w_ref, b_ref, o_ref):
    c = c_ref[...]
    a = c * _sigmoid(c)
    o_ref[...] = jnp.dot(a, w_ref[...], preferred_element_type=F32, precision=HIGHEST) + b_ref[...]


def _modulation(cc, w_mod, b_mod):
    n_layers, d, n = w_mod.shape
    r = cc.shape[0]
    tn = _tile(n, 1024)
    return pl.pallas_call(
        _mod_kernel,
        grid=(n_layers, n // tn),
        in_specs=[
            pl.BlockSpec((r, d), lambda l, j: (0, 0)),
            pl.BlockSpec((None, d, tn), lambda l, j: (l, 0, j)),
            pl.BlockSpec((None, 1, tn), lambda l, j: (l, 0, j)),
        ],
        out_specs=pl.BlockSpec((None, r, tn), lambda l, j: (l, 0, j)),
        out_shape=jax.ShapeDtypeStruct((n_layers, r, n), F32),
        compiler_params=_params("arbitrary", "arbitrary"),
        name="modulation",
    )(cc, w_mod, b_mod.reshape(n_layers, 1, n))


def _head_norm(x, g):
    ms = jnp.mean(x * x, axis=-1, keepdims=True)
    return x * lax.rsqrt(ms + EPS) * g


def _rope(x, cos, sin_signed, first_half):
    up = pltpu.roll(x, HEAD_DIM - HEAD_DIM // 4, 1)
    dn = pltpu.roll(x, HEAD_DIM // 4, 1)
    return x * cos + jnp.where(first_half, up, dn) * sin_signed


def _inproj_kernel(x_ref, sh_ref, sc_ref, g_ref, w_ref, qg_ref, kg_ref, cos_ref, sin_ref,
                   q_ref, k_ref, v_ref, z_ref, *, n_heads, conv_c):
    h = _adaln(x_ref[...], g_ref[...], sh_ref[...], sc_ref[...]).astype(BF16)
    acc = jnp.dot(h, w_ref[...], preferred_element_type=F32)
    cos = cos_ref[...]
    sin = sin_ref[...]
    lane = lax.broadcasted_iota(jnp.int32, cos.shape, 1)
    first_half = (lane % (HEAD_DIM // 2)) < (HEAD_DIM // 4)
    q_w = n_heads * HEAD_DIM
    kv_w = KV_HEADS * HEAD_DIM
    q_scale = HEAD_DIM ** -0.5
    for hd in range(n_heads):
        sl = slice(hd * HEAD_DIM, (hd + 1) * HEAD_DIM)
        qn = _head_norm(acc[:, sl], qg_ref[...])
        q_ref[:, sl] = (_rope(qn, cos, sin, first_half) * q_scale).astype(q_ref.dtype)
    for hd in range(KV_HEADS):
        sl = slice(hd * HEAD_DIM, (hd + 1) * HEAD_DIM)
        kn = _head_norm(acc[:, q_w + hd * HEAD_DIM:q_w + (hd + 1) * HEAD_DIM], kg_ref[...])
        k_ref[:, sl] = _rope(kn, cos, sin, first_half).astype(k_ref.dtype)
    v_ref[...] = acc[:, q_w + kv_w:q_w + 2 * kv_w].astype(v_ref.dtype)
    u0 = q_w + 2 * kv_w
    a = acc[:, u0:u0 + conv_c]
    gate = acc[:, u0 + conv_c:u0 + 2 * conv_c]
    z_ref[...] = a * _sigmoid(gate)


def _inproj(x, sh, sc, g, w_in, q_g, k_g, cos, sin_signed):
    b, s, d = x.shape
    in_w = w_in.shape[1]
    kv_w = KV_HEADS * HEAD_DIM
    conv_c = d // 2
    q_w = in_w - 2 * kv_w - 2 * conv_c
    n_heads = q_w // HEAD_DIM
    tm = _tile(s, 512)
    row = lambda bi, i: (bi, i, 0)
    vec = pl.BlockSpec((None, 1, d), lambda bi, i: (bi, 0, 0))
    tab = pl.BlockSpec((tm, HEAD_DIM), lambda bi, i: (i, 0))
    return pl.pallas_call(
        functools.partial(_inproj_kernel, n_heads=n_heads, conv_c=conv_c),
        grid=(b, s // tm),
        in_specs=[
            pl.BlockSpec((None, tm, d), row), vec, vec, _const_spec((1, d)), _const_spec((d, in_w)),
            _const_spec((1, HEAD_DIM)), _const_spec((1, HEAD_DIM)), tab, tab,
        ],
        out_specs=[
            pl.BlockSpec((None, tm, q_w), row), pl.BlockSpec((None, tm, kv_w), row),
            pl.BlockSpec((None, tm, kv_w), row), pl.BlockSpec((None, tm, conv_c), row),
        ],
        out_shape=[
            jax.ShapeDtypeStruct((b, s, q_w), BF16), jax.ShapeDtypeStruct((b, s, kv_w), BF16),
            jax.ShapeDtypeStruct((b, s, kv_w), BF16), jax.ShapeDtypeStruct((b, s, conv_c), F32),
        ],
        compiler_params=_params("arbitrary", "arbitrary"),
        name="inproj",
    )(x, sh, sc, g, w_in, q_g, k_g, cos, sin_signed)


def _ctx_kv_kernel(x_ref, sh_ref, sc_ref, g_ref, w_ref, kg_ref, k_ref, v_ref):
    h = _adaln(x_ref[...], g_ref[...], sh_ref[...], sc_ref[...]).astype(BF16)
    acc = jnp.dot(h, w_ref[...], preferred_element_type=F32)
    kv_w = KV_HEADS * HEAD_DIM
    for hd in range(KV_HEADS):
        sl = slice(hd * HEAD_DIM, (hd + 1) * HEAD_DIM)
        k_ref[:, sl] = _head_norm(acc[:, sl], kg_ref[...]).astype(k_ref.dtype)
    v_ref[...] = acc[:, kv_w:2 * kv_w].astype(v_ref.dtype)


def _ctx_kv(ctx, sh, sc, g, w_kv, k_g):
    b, lc, d = ctx.shape
    kv_w = KV_HEADS * HEAD_DIM
    row = lambda bi: (bi, 0, 0)
    return pl.pallas_call(
        _ctx_kv_kernel,
        grid=(b,),
        in_specs=[
            pl.BlockSpec((None, lc, d), row), _const_spec((1, d)), _const_spec((1, d)), _const_spec((1, d)),
            _const_spec((d, 2 * kv_w)), _const_spec((1, HEAD_DIM)),
        ],
        out_specs=[pl.BlockSpec((None, lc, kv_w), row), pl.BlockSpec((None, lc, kv_w), row)],
        out_shape=[jax.ShapeDtypeStruct((b, lc, kv_w), BF16), jax.ShapeDtypeStruct((b, lc, kv_w), BF16)],
        compiler_params=_params("arbitrary"),
        name="ctx_kv",
    )(ctx, sh, sc, g, w_kv, k_g)


def _attn_kernel(q_ref, k_ref, v_ref, kc_ref, vc_ref, o_ref):
    q = q_ref[...]
    nt = (((1,), (1,)), ((), ()))
    s1 = lax.dot_general(q, k_ref[...], nt, preferred_element_type=F32)
    s2 = lax.dot_general(q, kc_ref[...], nt, preferred_element_type=F32)
    m = jnp.maximum(jnp.max(s1, axis=-1, keepdims=True), jnp.max(s2, axis=-1, keepdims=True))
    p1 = jnp.exp(s1 - m)
    p2 = jnp.exp(s2 - m)
    l = jnp.sum(p1, axis=-1, keepdims=True) + jnp.sum(p2, axis=-1, keepdims=True)
    o = jnp.dot(p1.astype(BF16), v_ref[...], preferred_element_type=F32)
    o = o + jnp.dot(p2.astype(BF16), vc_ref[...], preferred_element_type=F32)
    o_ref[...] = (o / l).astype(o_ref.dtype)


def _attention(q, k, v, kc, vc):
    b, s, q_w = q.shape
    lc = kc.shape[1]
    group = q_w // (KV_HEADS * HEAD_DIM)
    tq = _tile(s, 512)
    kv_spec = lambda n: pl.BlockSpec((None, n, HEAD_DIM), lambda bi, h, g, i: (bi, 0, h))
    q_spec = pl.BlockSpec((None, tq, HEAD_DIM), lambda bi, h, g, i: (bi, i, h * group + g))
    return pl.pallas_call(
        _attn_kernel,
        grid=(b, KV_HEADS, group, s // tq),
        in_specs=[q_spec, kv_spec(s), kv_spec(s), kv_spec(lc), kv_spec(lc)],
        out_specs=q_spec,
        out_shape=jax.ShapeDtypeStruct((b, s, q_w), BF16),
        compiler_params=_params("arbitrary", "arbitrary", "arbitrary", "arbitrary"),
        name="attention",
    )(q, k, v, kc, vc)


def _conv_kernel(zc_ref, zp_ref, zn_ref, w_ref, b_ref, g_ref, beta_ref, o_ref, win_ref, conv_ref,
                 *, ts, n_tiles, rb, cb):
    i = pl.program_id(1)
    c = zc_ref.shape[1]
    win_ref[HALO:HALO + ts, :] = zc_ref[...]
    win_ref[0:HALO, :] = jnp.where(i > 0, zp_ref[...], 0.0)
    win_ref[HALO + ts:2 * HALO + ts, :] = jnp.where(i < n_tiles - 1, zn_ref[...], 0.0)
    base = HALO - CONV_PAD
    for r in range(ts // rb):
        for cc in range(c // cb):
            cs = slice(cc * cb, (cc + 1) * cb)
            acc = jnp.zeros((rb, cb), F32)
            for kk in range(CONV_K):
                r0 = r * rb + base + kk
                acc = acc + win_ref[r0:r0 + rb, cs] * w_ref[kk:kk + 1, cs]
            conv_ref[r * rb:(r + 1) * rb, cs] = acc + b_ref[:, cs]
    z = conv_ref[...]
    mu = jnp.mean(z, axis=-1, keepdims=True)
    zc = z - mu
    var = jnp.mean(zc * zc, axis=-1, keepdims=True)
    y = zc * lax.rsqrt(var + EPS) * g_ref[...] + beta_ref[...]
    o_ref[...] = (y * _sigmoid(y)).astype(o_ref.dtype)


def _conformer_conv(z, conv_w, conv_b, cn_g, cn_b):
    b, s, c = z.shape
    ts = _tile(s, 128)
    n_tiles = s // ts
    hb = ts // HALO
    n_hb = s // HALO
    cur = lambda bi, i: (bi, i, 0)
    prev = lambda bi, i: (bi, jnp.maximum(i * hb - 1, 0), 0)
    nxt = lambda bi, i: (bi, jnp.minimum((i + 1) * hb, n_hb - 1), 0)
    return pl.pallas_call(
        functools.partial(_conv_kernel, ts=ts, n_tiles=n_tiles, rb=_tile(ts, 32), cb=_tile(c, 512)),
        grid=(b, n_tiles),
        in_specs=[
            pl.BlockSpec((None, ts, c), cur), pl.BlockSpec((None, HALO, c), prev), pl.BlockSpec((None, HALO, c), nxt),
            _const_spec((CONV_K, c)), _const_spec((1, c)), _const_spec((1, c)), _const_spec((1, c)),
        ],
        out_specs=pl.BlockSpec((None, ts, c), cur),
        out_shape=jax.ShapeDtypeStruct((b, s, c), BF16),
        scratch_shapes=[pltpu.VMEM((ts + 2 * HALO, c), F32), pltpu.VMEM((ts, c), F32)],
        compiler_params=_params("arbitrary", "arbitrary"),
        name="conformer_conv",
    )(z, z, z, conv_w, conv_b, cn_g, cn_b)


def _token_sublanes(d):
    assert d % (2 * LANES) == 0
    return d // (2 * LANES)


def _pack_pair(lo, hi):
    lo_bits = lax.bitcast_convert_type(lo.astype(BF16).astype(F32), jnp.uint32) >> 16
    hi_bits = lax.bitcast_convert_type(hi.astype(BF16).astype(F32), jnp.uint32) & jnp.uint32(0xFFFF0000)
    return lo_bits | hi_bits


def _unpack_pair(w):
    lo = lax.bitcast_convert_type(w << 16, F32)
    hi = lax.bitcast_convert_type(w & jnp.uint32(0xFFFF0000), F32)
    return lo, hi


def _store_token_tiles(v, ref):
    m, d = v.shape
    sub = _token_sublanes(d)
    for s in range(sub):
        lo = v[:, s * LANES:(s + 1) * LANES]
        hi = v[:, d // 2 + s * LANES:d // 2 + (s + 1) * LANES]
        ref[pl.ds(s, m, stride=sub), :] = _pack_pair(lo, hi)


def _load_token_tiles(ref, start, m, sub):
    los, his = [], []
    for s in range(sub):
        lo, hi = _unpack_pair(ref[pl.ds(start + s, m, stride=sub), :])
        los.append(lo)
        his.append(hi)
    return los, his


def _finish_tile(xnew, g2n_ref, sh2_ref, sc2_ref, rwt_ref, x1_ref, h2_ref, h2p_ref, lgt_ref):
    x1_ref[...] = xnew
    h2 = _adaln(xnew, g2n_ref[...], sh2_ref[...], sc2_ref[...])
    h2_ref[...] = h2.astype(h2_ref.dtype)
    _store_token_tiles(h2, h2p_ref)
    lgt_ref[...] = lax.dot_general(rwt_ref[...], h2, (((1,), (1,)), ((), ())),
                                   preferred_element_type=F32, precision=HIGHEST)


def _outproj_kernel(a_ref, c_ref, x_ref, wa_ref, wc_ref, g1_ref, g2n_ref, sh2_ref, sc2_ref, rwt_ref,
                    x1_ref, h2_ref, h2p_ref, lgt_ref):
    y = jnp.dot(a_ref[...], wa_ref[...], preferred_element_type=F32)
    y = y + jnp.dot(c_ref[...], wc_ref[...], preferred_element_type=F32)
    _finish_tile(x_ref[...] + g1_ref[...] * y, g2n_ref, sh2_ref, sc2_ref, rwt_ref, x1_ref, h2_ref, h2p_ref, lgt_ref)


def _finish_specs(b, s, d, e, tm):
    nt = s // tm
    sub = _token_sublanes(d)
    row = lambda bi, i: (bi, i, 0)
    flat = lambda bi, i: (bi * nt + i, 0)
    out_specs = [pl.BlockSpec((None, tm, d), row), pl.BlockSpec((None, tm, d), row),
                 pl.BlockSpec((tm * sub, LANES), flat), pl.BlockSpec((e, tm), lambda bi, i: (0, bi * nt + i))]
    out_shape = [jax.ShapeDtypeStruct((b, s, d), F32), jax.ShapeDtypeStruct((b, s, d), BF16),
                 jax.ShapeDtypeStruct((b * s * sub, LANES), jnp.uint32), jax.ShapeDtypeStruct((e, b * s), F32)]
    return out_specs, out_shape


def _outproj(attn, cv, x, w_a, w_c, g1, g2n, sh2, sc2, rwt):
    b, s, d = x.shape
    e = rwt.shape[0]
    tm = _tile(s, 512)
    row = lambda bi, i: (bi, i, 0)
    vec = pl.BlockSpec((None, 1, d), lambda bi, i: (bi, 0, 0))
    out_specs, out_shape = _finish_specs(b, s, d, e, tm)
    return pl.pallas_call(
        _outproj_kernel,
        grid=(b, s // tm),
        in_specs=[
            pl.BlockSpec((None, tm, attn.shape[2]), row), pl.BlockSpec((None, tm, cv.shape[2]), row),
            pl.BlockSpec((None, tm, d), row), _const_spec(w_a.shape), _const_spec(w_c.shape),
            vec, _const_spec((1, d)), vec, vec, _const_spec((e, d)),
        ],
        out_specs=out_specs,
        out_shape=out_shape,
        compiler_params=_params("arbitrary", "arbitrary"),
        name="outproj",
    )(attn, cv, x, w_a, w_c, g1, g2n, sh2, sc2, rwt)


def _pool_kernel(xc_ref, xp_ref, xn_ref, g1n_ref, sh1_ref, sc1_ref, pw_ref, pb_ref, ps_ref, g1_ref,
                 g2n_ref, sh2_ref, sc2_ref, rwt_ref, x1_ref, h2_ref, h2p_ref, lgt_ref, win_ref, *, tm, n_tiles, seq):
    i = pl.program_id(1)
    d = xc_ref.shape[1]
    gc = d // len(POOL_WINDOWS)
    norm = lambda x: _adaln(x, g1n_ref[...], sh1_ref[...], sc1_ref[...])
    win_ref[HALO:HALO + tm, :] = norm(xc_ref[...])
    win_ref[0:HALO, :] = jnp.where(i > 0, norm(xp_ref[...]), 0.0)
    win_ref[HALO + tm:2 * HALO + tm, :] = jnp.where(i < n_tiles - 1, norm(xn_ref[...]), 0.0)
    t = i * tm + lax.broadcasted_iota(jnp.int32, (tm, 1), 0)
    ys = []
    for gi, w in enumerate(POOL_WINDOWS):
        cs = slice(gi * gc, (gi + 1) * gc)
        tot = jnp.zeros((tm, gc), F32)
        for off in range(-(w // 2), w - w // 2):
            tot = tot + win_ref[HALO + off:HALO + off + tm, cs]
        lo = jnp.clip(t - w // 2, 0, seq)
        hi = jnp.clip(t + w - w // 2, 0, seq)
        p = tot / (hi - lo).astype(F32) - win_ref[HALO:HALO + tm, cs]
        ys.append(jnp.dot(p.astype(BF16), pw_ref[gi], preferred_element_type=F32))
    y = (jnp.concatenate(ys, axis=-1) + pb_ref[...]) * ps_ref[...]
    _finish_tile(xc_ref[...] + g1_ref[...] * y, g2n_ref, sh2_ref, sc2_ref, rwt_ref, x1_ref, h2_ref, h2p_ref, lgt_ref)


def _pool_layer(x, g1n, sh1, sc1, pool_w, pool_b, pool_scale, g1, g2n, sh2, sc2, rwt):
    b, s, d = x.shape
    e = rwt.shape[0]
    tm = _tile(s, 256)
    n_tiles = s // tm
    hb = tm // HALO
    n_hb = s // HALO
    cur = lambda bi, i: (bi, i, 0)
    prev = lambda bi, i: (bi, jnp.maximum(i * hb - 1, 0), 0)
    nxt = lambda bi, i: (bi, jnp.minimum((i + 1) * hb, n_hb - 1), 0)
    vec = pl.BlockSpec((None, 1, d), lambda bi, i: (bi, 0, 0))
    cvec = _const_spec((1, d))
    out_specs, out_shape = _finish_specs(b, s, d, e, tm)
    return pl.pallas_call(
        functools.partial(_pool_kernel, tm=tm, n_tiles=n_tiles, seq=s),
        grid=(b, n_tiles),
        in_specs=[
            pl.BlockSpec((None, tm, d), cur), pl.BlockSpec((None, HALO, d), prev), pl.BlockSpec((None, HALO, d), nxt),
            cvec, vec, vec, _const_spec(pool_w.shape), cvec, cvec, vec, cvec, vec, vec, _const_spec((e, d)),
        ],
        out_specs=out_specs,
        out_shape=out_shape,
        scratch_shapes=[pltpu.VMEM((tm + 2 * HALO, d), F32)],
        compiler_params=_params("arbitrary", "arbitrary"),
        name="pool_mixer",
    )(x, x, x, g1n, sh1, sc1, pool_w, pool_b, pool_scale, g1, g2n, sh2, sc2, rwt)


def _route_kernel(lgt_ref, bias_ref, tri_ref, eid_ref, pos_ref, wts_ref, cnt_ref, carry_ref):
    i = pl.program_id(0)
    e, tn = lgt_ref.shape
    eg = e // N_EXPERT_GROUPS

    @pl.when(i == 0)
    def _():
        carry_ref[...] = jnp.zeros_like(carry_ref)

    scores = _sigmoid(lgt_ref[...])
    sel = scores + bias_ref[...]
    neg = jnp.float32(-jnp.inf)
    sub = lax.broadcasted_iota(jnp.int32, (eg, tn), 0)
    group_scores = []
    for g in range(N_EXPERT_GROUPS):
        blk = sel[g * eg:(g + 1) * eg, :]
        m1 = jnp.max(blk, axis=0, keepdims=True)
        first = jnp.min(jnp.where(blk == m1, sub, eg), axis=0, keepdims=True)
        m2 = jnp.max(jnp.where(sub == first, neg, blk), axis=0, keepdims=True)
        group_scores.append(m1 + m2)
    masked = []
    for g in range(N_EXPERT_GROUPS):
        rank = jnp.zeros((1, tn), jnp.int32)
        for g2 in range(N_EXPERT_GROUPS):
            if g2 == g:
                continue
            ahead = (group_scores[g2] >= group_scores[g]) if g2 < g else (group_scores[g2] > group_scores[g])
            rank = rank + ahead.astype(jnp.int32)
        masked.append(jnp.where(rank < TOPK_GROUPS, sel[g * eg:(g + 1) * eg, :], neg))
    selm = jnp.concatenate(masked, axis=0)

    eidx = lax.broadcasted_iota(jnp.int32, (e, tn), 0)
    erank = jnp.zeros((e, tn), jnp.int32)
    for e2 in range(e):
        rowv = selm[e2:e2 + 1, :]
        ahead = jnp.where(rowv > selm, 1, jnp.where((rowv == selm) & (eidx > e2), 1, 0))
        erank = erank + ahead
    chosen = erank < TOP_K

    m = chosen.astype(BF16)
    pos = jnp.dot(m, tri_ref[...], preferred_element_type=F32) + carry_ref[...]
    carry_ref[...] = carry_ref[...] + jnp.sum(chosen.astype(F32), axis=1, keepdims=True)
    cnt_ref[...] = carry_ref[...]

    wsum = jnp.sum(jnp.where(chosen, scores, 0.0), axis=0, keepdims=True)
    eidf = eidx.astype(F32)
    for j in range(TOP_K):
        hit = erank == j
        eid_ref[j:j + 1, :] = jnp.sum(jnp.where(hit, eidf, 0.0), axis=0, keepdims=True).astype(jnp.int32)
        pos_ref[j:j + 1, :] = jnp.sum(jnp.where(hit, pos, 0.0), axis=0, keepdims=True).astype(jnp.int32)
        wj = jnp.sum(jnp.where(hit, scores, 0.0), axis=0, keepdims=True)
        wts_ref[j:j + 1, :] = wj / wsum * ROUTED_SCALE


def _route(lgt, router_bias):
    e, t = lgt.shape
    tn = _tile(t, 512)
    tri = (lax.broadcasted_iota(jnp.int32, (tn, tn), 0) < lax.broadcasted_iota(jnp.int32, (tn, tn), 1)).astype(BF16)
    col = lambda i: (0, i)
    return pl.pallas_call(
        _route_kernel,
        grid=(t // tn,),
        in_specs=[pl.BlockSpec((e, tn), col), _const_spec((e, 1)), _const_spec((tn, tn))],
        out_specs=[pl.BlockSpec((TOP_K, tn), col), pl.BlockSpec((TOP_K, tn), col), pl.BlockSpec((TOP_K, tn), col),
                   pl.BlockSpec((e, 1), lambda i: (0, 0))],
        out_shape=[jax.ShapeDtypeStruct((TOP_K, t), jnp.int32), jax.ShapeDtypeStruct((TOP_K, t), jnp.int32),
                   jax.ShapeDtypeStruct((TOP_K, t), F32), jax.ShapeDtypeStruct((e, 1), F32)],
        scratch_shapes=[pltpu.VMEM((e, 1), F32)],
        compiler_params=_params("arbitrary"),
        name="route",
    )(lgt, router_bias.reshape(e, 1), tri)


def _dest_kernel(poff_ref, eid_ref, pos_ref, dest_ref, *, sub):
    eid = eid_ref[...]
    acc = pos_ref[...]
    for e in range(poff_ref.shape[0]):
        acc = acc + jnp.where(eid == e, poff_ref[e], 0)
    dest_ref[...] = acc * sub


def _dest_rows(poff, eid, pos, sub):
    k, t = eid.shape
    tn = _tile(t, 2048)
    col = lambda i, *_: (0, i)
    return pl.pallas_call(
        functools.partial(_dest_kernel, sub=sub),
        grid_spec=pltpu.PrefetchScalarGridSpec(
            num_scalar_prefetch=1, grid=(t // tn,),
            in_specs=[pl.BlockSpec((k, tn), col), pl.BlockSpec((k, tn), col)],
            out_specs=pl.BlockSpec((k, tn), col)),
        out_shape=jax.ShapeDtypeStruct((k, t), jnp.int32),
        compiler_params=_params("arbitrary"),
        name="dest_rows",
    )(poff, eid, pos)


def _dispatch_kernel(pend_ref, pcnt_ref, dest_ref, h_ref, xs_ref, zero_ref, sem, zsem, *, tt, rows, sub):
    i = pl.program_id(0)
    n_exp = pend_ref.shape[0]

    def zero_copy(e):
        start = pl.multiple_of((pend_ref[e] - rows) * sub, rows * sub)
        return pltpu.make_async_copy(zero_ref, xs_ref.at[pl.ds(start, rows * sub)], zsem)

    @pl.when(i == 0)
    def _():
        zero_ref[...] = jnp.zeros_like(zero_ref)

        def start_zero(e, c):
            @pl.when(pcnt_ref[e] > 0)
            def _():
                zero_copy(e).start()
            return c

        def wait_zero(e, c):
            @pl.when(pcnt_ref[e] > 0)
            def _():
                zero_copy(e).wait()
            return c

        lax.fori_loop(0, n_exp, start_zero, 0)
        lax.fori_loop(0, n_exp, wait_zero, 0)

    def row_copy(t, j):
        src = h_ref.at[pl.ds(pl.multiple_of(t * sub, sub), sub)]
        dst = xs_ref.at[pl.ds(pl.multiple_of(dest_ref[j, t], sub), sub)]
        return pltpu.make_async_copy(src, dst, sem)

    def issue(t, c):
        for j in range(TOP_K):
            row_copy(t, j).start()
        return c

    def drain(t, c):
        for j in range(TOP_K):
            row_copy(t, j).wait()
        return c

    lax.fori_loop(0, tt, issue, 0)
    lax.fori_loop(0, tt, drain, 0)


def _dispatch(h2p, dest, pend, pcnt, n_rows, rows, sub):
    t = h2p.shape[0] // sub
    tt = _tile(t, 256)
    return pl.pallas_call(
        functools.partial(_dispatch_kernel, tt=tt, rows=rows, sub=sub),
        grid_spec=pltpu.PrefetchScalarGridSpec(
            num_scalar_prefetch=2, grid=(t // tt,),
            in_specs=[pl.BlockSpec((TOP_K, tt), lambda i, *_: (0, i), memory_space=pltpu.SMEM),
                      pl.BlockSpec((tt * sub, LANES), lambda i, *_: (i, 0))],
            out_specs=pl.BlockSpec(memory_space=pl.ANY),
            scratch_shapes=[pltpu.VMEM((rows * sub, LANES), jnp.uint32), pltpu.SemaphoreType.DMA,
                            pltpu.SemaphoreType.DMA]),
        out_shape=jax.ShapeDtypeStruct((n_rows * sub, LANES), jnp.uint32),
        compiler_params=_params("arbitrary"),
        name="dispatch",
    )(pend, pcnt, dest, h2p)


def _expert_kernel(blk_e_ref, nreal_ref, x_ref, wg_ref, wu_ref, wd_ref, y_ref, wgb_ref, wub_ref, wdb_ref,
                   *, rows, sub):
    b = pl.program_id(0)
    changed = jnp.logical_or(b == 0, blk_e_ref[b] != blk_e_ref[jnp.maximum(b - 1, 0)])

    @pl.when(jnp.logical_and(changed, b < nreal_ref[0]))
    def _():
        wgb_ref[...] = wg_ref[...].astype(BF16)
        wub_ref[...] = wu_ref[...].astype(BF16)
        wdb_ref[...] = wd_ref[...].astype(BF16)

    @pl.when(b < nreal_ref[0])
    def _():
        los, his = _load_token_tiles(x_ref, 0, rows, sub)
        x = jnp.concatenate([p.astype(BF16) for p in los + his], axis=1)
        g = jnp.dot(x, wgb_ref[...], preferred_element_type=F32)
        u = jnp.dot(x, wub_ref[...], preferred_element_type=F32)
        a = (g * _sigmoid(g) * u).astype(BF16)
        _store_token_tiles(jnp.dot(a, wdb_ref[...], preferred_element_type=F32), y_ref)


def _experts(xs, blk_e, n_real, w_gate, w_up, w_down, layer, rows, sub):
    d, ff = w_gate.shape[2:]
    nb = xs.shape[0] // (rows * sub)
    xrow = lambda b, be, nr: (jnp.minimum(b, jnp.maximum(nr[0] - 1, 0)), 0)
    wsel = lambda b, be, nr: (layer, be[b], 0, 0)
    return pl.pallas_call(
        functools.partial(_expert_kernel, rows=rows, sub=sub),
        grid_spec=pltpu.PrefetchScalarGridSpec(
            num_scalar_prefetch=2, grid=(nb,),
            in_specs=[pl.BlockSpec((rows * sub, LANES), xrow), pl.BlockSpec((None, None, d, ff), wsel),
                      pl.BlockSpec((None, None, d, ff), wsel), pl.BlockSpec((None, None, ff, d), wsel)],
            out_specs=pl.BlockSpec((rows * sub, LANES), xrow),
            scratch_shapes=[pltpu.VMEM((d, ff), BF16), pltpu.VMEM((d, ff), BF16), pltpu.VMEM((ff, d), BF16)]),
        out_shape=jax.ShapeDtypeStruct(xs.shape, jnp.uint32),
        compiler_params=_params("arbitrary"),
        name="experts",
    )(blk_e, n_real, xs, w_gate, w_up, w_down)


def _combine_kernel(dest_ref, ys_ref, wt_ref, x1_ref, h_ref, sg_ref, su_ref, sd_ref, g2_ref, o_ref, gbuf_ref, sem,
                    *, tt, sub):
    def row_copy(t, j):
        src = ys_ref.at[pl.ds(pl.multiple_of(dest_ref[j, t], sub), sub)]
        dst = gbuf_ref.at[pl.ds(pl.multiple_of((j * tt + t) * sub, sub), sub)]
        return pltpu.make_async_copy(src, dst, sem)

    def issue(t, c):
        for j in range(TOP_K):
            row_copy(t, j).start()
        return c

    def drain(t, c):
        for j in range(TOP_K):
            row_copy(t, j).wait()
        return c

    lax.fori_loop(0, tt, issue, 0)
    h = h_ref[...]
    g = jnp.dot(h, sg_ref[...], preferred_element_type=F32)
    u = jnp.dot(h, su_ref[...], preferred_element_type=F32)
    shared = jnp.dot((g * _sigmoid(g) * u).astype(BF16), sd_ref[...], preferred_element_type=F32)
    lax.fori_loop(0, tt, drain, 0)
    wt = wt_ref[...]
    acc = [jnp.zeros((tt, LANES), F32) for _ in range(2 * sub)]
    for j in range(TOP_K):
        wj = jnp.broadcast_to(wt[:, j:j + 1], (tt, LANES))
        los, his = _load_token_tiles(gbuf_ref, j * tt * sub, tt, sub)
        acc = [a + p * wj for a, p in zip(acc, los + his)]
    routed = jnp.concatenate(acc, axis=1)
    o_ref[...] = x1_ref[...] + g2_ref[...] * (routed + shared)


def _combine(ys, dest, wts_t, x1, h2, s_gate, s_up, s_down, g2, seq, sub):
    t, d = x1.shape
    sf = s_gate.shape[1]
    tt = _tile(seq, 128)
    row = lambda i, *_: (i, 0)
    return pl.pallas_call(
        functools.partial(_combine_kernel, tt=tt, sub=sub),
        grid=(t // tt,),
        in_specs=[
            pl.BlockSpec((TOP_K, tt), lambda i: (0, i), memory_space=pltpu.SMEM),
            pl.BlockSpec(memory_space=pl.ANY),
            pl.BlockSpec((tt, TOP_K), row), pl.BlockSpec((tt, d), row), pl.BlockSpec((tt, d), row),
            _const_spec((d, sf)), _const_spec((d, sf)), _const_spec((sf, d)),
            pl.BlockSpec((None, 1, d), lambda i: ((i * tt) // seq, 0, 0)),
        ],
        out_specs=pl.BlockSpec((tt, d), row),
        out_shape=jax.ShapeDtypeStruct((t, d), F32),
        scratch_shapes=[pltpu.VMEM((TOP_K * tt * sub, LANES), jnp.uint32), pltpu.SemaphoreType.DMA],
        compiler_params=_params("arbitrary"),
        name="combine",
    )(dest, ys, wts_t, x1, h2, s_gate, s_up, s_down, g2)


def _moe(x1, h2, h2p, lgt, g2, router_bias, w_gate, w_up, w_down, layer, s_gate, s_up, s_down):
    b, s, d = x1.shape
    t = b * s
    e = lgt.shape[0]
    sub = _token_sublanes(d)
    rows = _tile(t, 256)
    eid, pos, wts, cnt = _route(lgt, router_bias)
    counts = cnt[:, 0].astype(jnp.int32)
    pcnt = (counts + rows - 1) // rows * rows
    pend = jnp.cumsum(pcnt)
    poff = pend - pcnt
    nb = (t * TOP_K + e * (rows - 1)) // rows
    n_real = (pend[-1] // rows).astype(jnp.int32).reshape(1)
    starts = jnp.arange(nb, dtype=jnp.int32) * rows
    blk_e = jnp.minimum(jnp.sum((pend[None, :] <= starts[:, None]).astype(jnp.int32), axis=1), e - 1)
    dest = _dest_rows(poff.astype(jnp.int32), eid, pos, sub)
    xs = _dispatch(h2p, dest, pend.astype(jnp.int32), pcnt.astype(jnp.int32), nb * rows, rows, sub)
    ys = _experts(xs, blk_e, n_real, w_gate, w_up, w_down, layer, rows, sub)
    out = _combine(ys, dest, wts.T, x1.reshape(t, d), h2.reshape(t, d), s_gate.astype(BF16), s_up.astype(BF16),
                   s_down.astype(BF16), g2, s, sub)
    return out.reshape(b, s, d)


def _rope_tables(n_tok):
    axis_dim = HEAD_DIM // 2
    rows = n_tok // GRID_W
    r, col = jnp.meshgrid(jnp.arange(rows), jnp.arange(GRID_W), indexing="ij")
    pos = jnp.stack([r.reshape(-1), col.reshape(-1)], axis=-1).astype(F32)
    inv = ROPE_THETA ** (-jnp.arange(0, axis_dim, 2, dtype=F32) / axis_dim)
    ang = pos[:, :, None] * inv
    ang = jnp.broadcast_to(ang[:, :, None, :], (n_tok, 2, 2, axis_dim // 2)).reshape(n_tok, HEAD_DIM)
    sign = jnp.where((jnp.arange(HEAD_DIM) % axis_dim) < axis_dim // 2, -1.0, 1.0).astype(F32)
    return jnp.cos(ang), jnp.sin(ang) * sign


def kernel(x, c, ctx, c_ctx, w_mod, b_mod, norm1_g, norm2_g, mix_w_in, q_norm_g, k_norm_g, conv_w, conv_b,
           conv_norm_g, conv_norm_b, mix_w_out, pool_w, pool_b, pool_scale, router_w, router_bias,
           moe_w_gate, moe_w_up, moe_w_down, shared_w_gate, shared_w_up, shared_w_down):
    b, s, d = x.shape
    depth = w_mod.shape[0]
    assert depth == 2, "layer schedule below is written for an attention layer followed by a pooling layer"
    kv_w = KV_HEADS * HEAD_DIM
    q_w = mix_w_in.shape[2] - 2 * kv_w - d

    pad = (-(b + 1)) % SUBLANES
    cc = jnp.concatenate([c, c_ctx[None, :], jnp.zeros((pad, d), F32)], axis=0)
    mods = _modulation(cc, w_mod, b_mod)

    def mod(layer, k, rows=slice(0, b)):
        return mods[layer, rows, k * d:(k + 1) * d][:, None, :]

    vec = lambda a: a.reshape(1, -1)
    rwt = jnp.swapaxes(router_w, 1, 2)
    cos, sin_signed = _rope_tables(s)

    w_in = mix_w_in[0].astype(BF16)
    q, k, v, z = _inproj(x, mod(0, 0), mod(0, 1), vec(norm1_g[0]), w_in, vec(q_norm_g[0]), vec(k_norm_g[0]),
                         cos, sin_signed)
    ctx_row = slice(b, b + 1)
    kc, vc = _ctx_kv(ctx, mods[0, ctx_row, 0:d], mods[0, ctx_row, d:2 * d], vec(norm1_g[0]),
                     w_in[:, q_w:q_w + 2 * kv_w], vec(k_norm_g[0]))
    attn = _attention(q, k, v, kc, vc)
    cv = _conformer_conv(z, conv_w[0], vec(conv_b[0]), vec(conv_norm_g[0]), vec(conv_norm_b[0]))
    w_out = mix_w_out[0].astype(BF16)
    x1, h2, h2p, lgt = _outproj(attn, cv, x, w_out[:q_w], w_out[q_w:], mod(0, 2), vec(norm2_g[0]), mod(0, 3),
                                mod(0, 4), rwt[0])
    x = _moe(x1, h2, h2p, lgt, mod(0, 5), router_bias[0], moe_w_gate, moe_w_up, moe_w_down, 0,
             shared_w_gate[0], shared_w_up[0], shared_w_down[0])

    x1, h2, h2p, lgt = _pool_layer(x, vec(norm1_g[1]), mod(1, 0), mod(1, 1), pool_w[0].astype(BF16), vec(pool_b[0]),
                                   vec(pool_scale[0]), mod(1, 2), vec(norm2_g[1]), mod(1, 3), mod(1, 4), rwt[1])
    x = _moe(x1, h2, h2p, lgt, mod(1, 5), router_bias[1], moe_w_gate, moe_w_up, moe_w_down, 1,
             shared_w_gate[1], shared_w_up[1], shared_w_down[1])
    return x
```

```python
import functools

import jax
import jax.numpy as jnp
from jax import lax
from jax.experimental import pallas as pl
from jax.experimental.pallas import tpu as pltpu

HEAD_DIM = 128
KV_HEADS = 2
GRID_W = 64
ROPE_THETA = 10000.0
EPS = 1e-6
CONV_K = 31
CONV_PAD = CONV_K // 2
POOL_WINDOWS = (2, 4, 8, 16)
N_EXPERT_GROUPS = 8
TOPK_GROUPS = 4
TOP_K = 8
ROUTED_SCALE = 2.5

LANES = 128
SUBLANES = 8
HALO = 16
MOE_BLOCK_ROWS = 512
VMEM_LIMIT_BYTES = 56 * 1024 * 1024

F32 = jnp.float32
BF16 = jnp.bfloat16
HIGHEST = lax.Precision.HIGHEST


def _tile(n, pref):
    t = min(n, pref)
    while n % t:
        t //= 2
    return t


def _params(*sem):
    return pltpu.CompilerParams(dimension_semantics=sem, vmem_limit_bytes=VMEM_LIMIT_BYTES)


def _const_spec(shape):
    nd = len(shape)
    return pl.BlockSpec(shape, lambda *_: (0,) * nd, pipeline_mode=pl.Buffered(1))


def _sigmoid(x):
    return 1.0 / (1.0 + jnp.exp(-x))


def _adaln(x, g, shift, scale):
    ms = jnp.mean(x * x, axis=-1, keepdims=True)
    return (x * lax.rsqrt(ms + EPS) * g) * (1.0 + scale) + shift


def _mod_kernel(c_ref, w_ref, b_ref, o_ref):
    c = c_ref[...]
    a = c * _sigmoid(c)
    o_ref[...] = jnp.dot(a, w_ref[...], preferred_element_type=F32, precision=HIGHEST) + b_ref[...]


def _modulation(cc, w_mod, b_mod):
    n_layers, d, n = w_mod.shape
    r = cc.shape[0]
    tn = _tile(n, 1024)
    return pl.pallas_call(
        _mod_kernel,
        grid=(n_layers, n // tn),
        in_specs=[
            pl.BlockSpec((r, d), lambda l, j: (0, 0)),
            pl.BlockSpec((None, d, tn), lambda l, j: (l, 0, j)),
            pl.BlockSpec((None, 1, tn), lambda l, j: (l, 0, j)),
        ],
        out_specs=pl.BlockSpec((None, r, tn), lambda l, j: (l, 0, j)),
        out_shape=jax.ShapeDtypeStruct((n_layers, r, n), F32),
        compiler_params=_params("arbitrary", "arbitrary"),
        name="modulation",
    )(cc, w_mod, b_mod.reshape(n_layers, 1, n))


def _head_norm(x, g):
    ms = jnp.mean(x * x, axis=-1, keepdims=True)
    return x * lax.rsqrt(ms + EPS) * g


def _rope(x, cos, sin_signed, first_half):
    up = pltpu.roll(x, HEAD_DIM - HEAD_DIM // 4, 1)
    dn = pltpu.roll(x, HEAD_DIM // 4, 1)
    return x * cos + jnp.where(first_half, up, dn) * sin_signed


def _inproj_kernel(x_ref, sh_ref, sc_ref, g_ref, w_ref, qg_ref, kg_ref, cos_ref, sin_ref,
                   q_ref, k_ref, v_ref, z_ref, *, n_heads, conv_c):
    h = _adaln(x_ref[...], g_ref[...], sh_ref[...], sc_ref[...]).astype(BF16)
    acc = jnp.dot(h, w_ref[...], preferred_element_type=F32)
    cos = cos_ref[...]
    sin = sin_ref[...]
    lane = lax.broadcasted_iota(jnp.int32, cos.shape, 1)
    first_half = (lane % (HEAD_DIM // 2)) < (HEAD_DIM // 4)
    q_w = n_heads * HEAD_DIM
    kv_w = KV_HEADS * HEAD_DIM
    q_scale = HEAD_DIM ** -0.5
    for hd in range(n_heads):
        sl = slice(hd * HEAD_DIM, (hd + 1) * HEAD_DIM)
        qn = _head_norm(acc[:, sl], qg_ref[...])
        q_ref[:, sl] = (_rope(qn, cos, sin, first_half) * q_scale).astype(q_ref.dtype)
    for hd in range(KV_HEADS):
        sl = slice(hd * HEAD_DIM, (hd + 1) * HEAD_DIM)
        kn = _head_norm(acc[:, q_w + hd * HEAD_DIM:q_w + (hd + 1) * HEAD_DIM], kg_ref[...])
        k_ref[:, sl] = _rope(kn, cos, sin, first_half).astype(k_ref.dtype)
    v_ref[...] = acc[:, q_w + kv_w:q_w + 2 * kv_w].astype(v_ref.dtype)
    u0 = q_w + 2 * kv_w
    a = acc[:, u0:u0 + conv_c]
    gate = acc[:, u0 + conv_c:u0 + 2 * conv_c]
    z_ref[...] = a * _sigmoid(gate)


def _inproj(x, sh, sc, g, w_in, q_g, k_g, cos, sin_signed):
    b, s, d = x.shape
    in_w = w_in.shape[1]
    kv_w = KV_HEADS * HEAD_DIM
    conv_c = d // 2
    q_w = in_w - 2 * kv_w - 2 * conv_c
    n_heads = q_w // HEAD_DIM
    tm = _tile(s, 512)
    row = lambda bi, i: (bi, i, 0)
    vec = pl.BlockSpec((None, 1, d), lambda bi, i: (bi, 0, 0))
    tab = pl.BlockSpec((tm, HEAD_DIM), lambda bi, i: (i, 0))
    return pl.pallas_call(
        functools.partial(_inproj_kernel, n_heads=n_heads, conv_c=conv_c),
        grid=(b, s // tm),
        in_specs=[
            pl.BlockSpec((None, tm, d), row), vec, vec, _const_spec((1, d)), _const_spec((d, in_w)),
            _const_spec((1, HEAD_DIM)), _const_spec((1, HEAD_DIM)), tab, tab,
        ],
        out_specs=[
            pl.BlockSpec((None, tm, q_w), row), pl.BlockSpec((None, tm, kv_w), row),
            pl.BlockSpec((None, tm, kv_w), row), pl.BlockSpec((None, tm, conv_c), row),
        ],
        out_shape=[
            jax.ShapeDtypeStruct((b, s, q_w), BF16), jax.ShapeDtypeStruct((b, s, kv_w), BF16),
            jax.ShapeDtypeStruct((b, s, kv_w), BF16), jax.ShapeDtypeStruct((b, s, conv_c), F32),
        ],
        compiler_params=_params("arbitrary", "arbitrary"),
        name="inproj",
    )(x, sh, sc, g, w_in, q_g, k_g, cos, sin_signed)


def _ctx_kv_kernel(x_ref, sh_ref, sc_ref, g_ref, w_ref, kg_ref, k_ref, v_ref):
    h = _adaln(x_ref[...], g_ref[...], sh_ref[...], sc_ref[...]).astype(BF16)
    acc = jnp.dot(h, w_ref[...], preferred_element_type=F32)
    kv_w = KV_HEADS * HEAD_DIM
    for hd in range(KV_HEADS):
        sl = slice(hd * HEAD_DIM, (hd + 1) * HEAD_DIM)
        k_ref[:, sl] = _head_norm(acc[:, sl], kg_ref[...]).astype(k_ref.dtype)
    v_ref[...] = acc[:, kv_w:2 * kv_w].astype(v_ref.dtype)


def _ctx_kv(ctx, sh, sc, g, w_kv, k_g):
    b, lc, d = ctx.shape
    kv_w = KV_HEADS * HEAD_DIM
    row = lambda bi: (bi, 0, 0)
    return pl.pallas_call(
        _ctx_kv_kernel,
        grid=(b,),
        in_specs=[
            pl.BlockSpec((None, lc, d), row), _const_spec((1, d)), _const_spec((1, d)), _const_spec((1, d)),
            _const_spec((d, 2 * kv_w)), _const_spec((1, HEAD_DIM)),
        ],
        out_specs=[pl.BlockSpec((None, lc, kv_w), row), pl.BlockSpec((None, lc, kv_w), row)],
        out_shape=[jax.ShapeDtypeStruct((b, lc, kv_w), BF16), jax.ShapeDtypeStruct((b, lc, kv_w), BF16)],
        compiler_params=_params("arbitrary"),
        name="ctx_kv",
    )(ctx, sh, sc, g, w_kv, k_g)


def _attn_kernel(q_ref, k_ref, v_ref, kc_ref, vc_ref, o_ref, *, group, tk):
    tq = q_ref.shape[0]
    rows = group * tq
    q = jnp.concatenate([q_ref[:, g * HEAD_DIM:(g + 1) * HEAD_DIM] for g in range(group)], axis=0)
    nt = (((1,), (1,)), ((), ()))
    chunks = [(k_ref, v_ref, c * tk, tk) for c in range(k_ref.shape[0] // tk)]
    chunks.append((kc_ref, vc_ref, 0, kc_ref.shape[0]))
    m = jnp.full((rows, 1), -jnp.inf, F32)
    l = jnp.zeros((rows, 1), F32)
    acc = jnp.zeros((rows, HEAD_DIM), F32)
    for kr, vr, start, size in chunks:
        s = lax.dot_general(q, kr[start:start + size, :], nt, preferred_element_type=F32)
        m_new = jnp.maximum(m, jnp.max(s, axis=-1, keepdims=True))
        alpha = jnp.exp(m - m_new)
        p = jnp.exp(s - m_new)
        l = alpha * l + jnp.sum(p, axis=-1, keepdims=True)
        acc = alpha * acc + jnp.dot(p.astype(BF16), vr[start:start + size, :], preferred_element_type=F32)
        m = m_new
    o = acc / l
    for g in range(group):
        o_ref[:, g * HEAD_DIM:(g + 1) * HEAD_DIM] = o[g * tq:(g + 1) * tq, :].astype(o_ref.dtype)


def _attention(q, k, v, kc, vc):
    b, s, q_w = q.shape
    lc = kc.shape[1]
    group = q_w // (KV_HEADS * HEAD_DIM)
    tq = _tile(s, 256)
    kv_spec = lambda n: pl.BlockSpec((None, n, HEAD_DIM), lambda bi, h, i: (bi, 0, h))
    q_spec = pl.BlockSpec((None, tq, group * HEAD_DIM), lambda bi, h, i: (bi, i, h))
    return pl.pallas_call(
        functools.partial(_attn_kernel, group=group, tk=_tile(s, 512)),
        grid=(b, KV_HEADS, s // tq),
        in_specs=[q_spec, kv_spec(s), kv_spec(s), kv_spec(lc), kv_spec(lc)],
        out_specs=q_spec,
        out_shape=jax.ShapeDtypeStruct((b, s, q_w), BF16),
        compiler_params=_params("arbitrary", "arbitrary", "arbitrary"),
        name="attention",
    )(q, k, v, kc, vc)


def _conv_kernel(zc_ref, zp_ref, zn_ref, w_ref, b_ref, g_ref, beta_ref, o_ref, win_ref, conv_ref,
                 *, ts, n_tiles, rb, cb):
    i = pl.program_id(1)
    c = zc_ref.shape[1]
    win_ref[HALO:HALO + ts, :] = zc_ref[...]
    win_ref[0:HALO, :] = jnp.where(i > 0, zp_ref[...], 0.0)
    win_ref[HALO + ts:2 * HALO + ts, :] = jnp.where(i < n_tiles - 1, zn_ref[...], 0.0)
    base = HALO - CONV_PAD
    for r in range(ts // rb):
        for cc in range(c // cb):
            cs = slice(cc * cb, (cc + 1) * cb)
            acc = jnp.zeros((rb, cb), F32)
            for kk in range(CONV_K):
                r0 = r * rb + base + kk
                acc = acc + win_ref[r0:r0 + rb, cs] * w_ref[kk:kk + 1, cs]
            conv_ref[r * rb:(r + 1) * rb, cs] = acc + b_ref[:, cs]
    z = conv_ref[...]
    mu = jnp.mean(z, axis=-1, keepdims=True)
    zc = z - mu
    var = jnp.mean(zc * zc, axis=-1, keepdims=True)
    y = zc * lax.rsqrt(var + EPS) * g_ref[...] + beta_ref[...]
    o_ref[...] = (y * _sigmoid(y)).astype(o_ref.dtype)


def _conformer_conv(z, conv_w, conv_b, cn_g, cn_b):
    b, s, c = z.shape
    ts = _tile(s, 128)
    n_tiles = s // ts
    hb = ts // HALO
    n_hb = s // HALO
    cur = lambda bi, i: (bi, i, 0)
    prev = lambda bi, i: (bi, jnp.maximum(i * hb - 1, 0), 0)
    nxt = lambda bi, i: (bi, jnp.minimum((i + 1) * hb, n_hb - 1), 0)
    return pl.pallas_call(
        functools.partial(_conv_kernel, ts=ts, n_tiles=n_tiles, rb=_tile(ts, 32), cb=_tile(c, 512)),
        grid=(b, n_tiles),
        in_specs=[
            pl.BlockSpec((None, ts, c), cur), pl.BlockSpec((None, HALO, c), prev), pl.BlockSpec((None, HALO, c), nxt),
            _const_spec((CONV_K, c)), _const_spec((1, c)), _const_spec((1, c)), _const_spec((1, c)),
        ],
        out_specs=pl.BlockSpec((None, ts, c), cur),
        out_shape=jax.ShapeDtypeStruct((b, s, c), BF16),
        scratch_shapes=[pltpu.VMEM((ts + 2 * HALO, c), F32), pltpu.VMEM((ts, c), F32)],
        compiler_params=_params("arbitrary", "arbitrary"),
        name="conformer_conv",
    )(z, z, z, conv_w, conv_b, cn_g, cn_b)


def _token_sublanes(d):
    assert d % (2 * LANES) == 0
    return d // (2 * LANES)


def _pack_pair(lo, hi):
    lo_bits = lax.bitcast_convert_type(lo.astype(BF16).astype(F32), jnp.uint32) >> 16
    hi_bits = lax.bitcast_convert_type(hi.astype(BF16).astype(F32), jnp.uint32) & jnp.uint32(0xFFFF0000)
    return lo_bits | hi_bits


def _unpack_pair(w):
    lo = lax.bitcast_convert_type(w << 16, F32)
    hi = lax.bitcast_convert_type(w & jnp.uint32(0xFFFF0000), F32)
    return lo, hi


def _store_token_tiles(v, ref):
    m, d = v.shape
    sub = _token_sublanes(d)
    for s in range(sub):
        lo = v[:, s * LANES:(s + 1) * LANES]
        hi = v[:, d // 2 + s * LANES:d // 2 + (s + 1) * LANES]
        ref[pl.ds(s, m, stride=sub), :] = _pack_pair(lo, hi)


def _load_token_tiles(ref, start, m, sub):
    los, his = [], []
    for s in range(sub):
        lo, hi = _unpack_pair(ref[pl.ds(start + s, m, stride=sub), :])
        los.append(lo)
        his.append(hi)
    return los, his


def _finish_tile(xnew, g2n_ref, sh2_ref, sc2_ref, rwt_ref, x1_ref, h2_ref, h2p_ref, lgt_ref):
    x1_ref[...] = xnew
    h2 = _adaln(xnew, g2n_ref[...], sh2_ref[...], sc2_ref[...])
    h2_ref[...] = h2.astype(h2_ref.dtype)
    _store_token_tiles(h2, h2p_ref)
    lgt_ref[...] = lax.dot_general(rwt_ref[...], h2, (((1,), (1,)), ((), ())),
                                   preferred_element_type=F32, precision=HIGHEST)


def _outproj_kernel(a_ref, c_ref, x_ref, wa_ref, wc_ref, g1_ref, g2n_ref, sh2_ref, sc2_ref, rwt_ref,
                    x1_ref, h2_ref, h2p_ref, lgt_ref):
    y = jnp.dot(a_ref[...], wa_ref[...], preferred_element_type=F32)
    y = y + jnp.dot(c_ref[...], wc_ref[...], preferred_element_type=F32)
    _finish_tile(x_ref[...] + g1_ref[...] * y, g2n_ref, sh2_ref, sc2_ref, rwt_ref, x1_ref, h2_ref, h2p_ref, lgt_ref)


def _finish_specs(b, s, d, e, tm):
    nt = s // tm
    sub = _token_sublanes(d)
    row = lambda bi, i: (bi, i, 0)
    flat = lambda bi, i: (bi * nt + i, 0)
    out_specs = [pl.BlockSpec((None, tm, d), row), pl.BlockSpec((None, tm, d), row),
                 pl.BlockSpec((tm * sub, LANES), flat), pl.BlockSpec((e, tm), lambda bi, i: (0, bi * nt + i))]
    out_shape = [jax.ShapeDtypeStruct((b, s, d), F32), jax.ShapeDtypeStruct((b, s, d), BF16),
                 jax.ShapeDtypeStruct((b * s * sub, LANES), jnp.uint32), jax.ShapeDtypeStruct((e, b * s), F32)]
    return out_specs, out_shape


def _outproj(attn, cv, x, w_a, w_c, g1, g2n, sh2, sc2, rwt):
    b, s, d = x.shape
    e = rwt.shape[0]
    tm = _tile(s, 512)
    row = lambda bi, i: (bi, i, 0)
    vec = pl.BlockSpec((None, 1, d), lambda bi, i: (bi, 0, 0))
    out_specs, out_shape = _finish_specs(b, s, d, e, tm)
    return pl.pallas_call(
        _outproj_kernel,
        grid=(b, s // tm),
        in_specs=[
            pl.BlockSpec((None, tm, attn.shape[2]), row), pl.BlockSpec((None, tm, cv.shape[2]), row),
            pl.BlockSpec((None, tm, d), row), _const_spec(w_a.shape), _const_spec(w_c.shape),
            vec, _const_spec((1, d)), vec, vec, _const_spec((e, d)),
        ],
        out_specs=out_specs,
        out_shape=out_shape,
        compiler_params=_params("arbitrary", "arbitrary"),
        name="outproj",
    )(attn, cv, x, w_a, w_c, g1, g2n, sh2, sc2, rwt)


def _pool_kernel(xc_ref, xp_ref, xn_ref, g1n_ref, sh1_ref, sc1_ref, pw_ref, pb_ref, ps_ref, g1_ref,
                 g2n_ref, sh2_ref, sc2_ref, rwt_ref, x1_ref, h2_ref, h2p_ref, lgt_ref, win_ref, *, tm, n_tiles, seq):
    i = pl.program_id(1)
    d = xc_ref.shape[1]
    gc = d // len(POOL_WINDOWS)
    norm = lambda x: _adaln(x, g1n_ref[...], sh1_ref[...], sc1_ref[...])
    win_ref[HALO:HALO + tm, :] = norm(xc_ref[...])
    win_ref[0:HALO, :] = jnp.where(i > 0, norm(xp_ref[...]), 0.0)
    win_ref[HALO + tm:2 * HALO + tm, :] = jnp.where(i < n_tiles - 1, norm(xn_ref[...]), 0.0)
    t = i * tm + lax.broadcasted_iota(jnp.int32, (tm, 1), 0)
    ys = []
    for gi, w in enumerate(POOL_WINDOWS):
        cs = slice(gi * gc, (gi + 1) * gc)
        tot = jnp.zeros((tm, gc), F32)
        for off in range(-(w // 2), w - w // 2):
            tot = tot + win_ref[HALO + off:HALO + off + tm, cs]
        lo = jnp.clip(t - w // 2, 0, seq)
        hi = jnp.clip(t + w - w // 2, 0, seq)
        p = tot / (hi - lo).astype(F32) - win_ref[HALO:HALO + tm, cs]
        ys.append(jnp.dot(p.astype(BF16), pw_ref[gi], preferred_element_type=F32))
    y = (jnp.concatenate(ys, axis=-1) + pb_ref[...]) * ps_ref[...]
    _finish_tile(xc_ref[...] + g1_ref[...] * y, g2n_ref, sh2_ref, sc2_ref, rwt_ref, x1_ref, h2_ref, h2p_ref, lgt_ref)


def _pool_layer(x, g1n, sh1, sc1, pool_w, pool_b, pool_scale, g1, g2n, sh2, sc2, rwt):
    b, s, d = x.shape
    e = rwt.shape[0]
    tm = _tile(s, 256)
    n_tiles = s // tm
    hb = tm // HALO
    n_hb = s // HALO
    cur = lambda bi, i: (bi, i, 0)
    prev = lambda bi, i: (bi, jnp.maximum(i * hb - 1, 0), 0)
    nxt = lambda bi, i: (bi, jnp.minimum((i + 1) * hb, n_hb - 1), 0)
    vec = pl.BlockSpec((None, 1, d), lambda bi, i: (bi, 0, 0))
    cvec = _const_spec((1, d))
    out_specs, out_shape = _finish_specs(b, s, d, e, tm)
    return pl.pallas_call(
        functools.partial(_pool_kernel, tm=tm, n_tiles=n_tiles, seq=s),
        grid=(b, n_tiles),
        in_specs=[
            pl.BlockSpec((None, tm, d), cur), pl.BlockSpec((None, HALO, d), prev), pl.BlockSpec((None, HALO, d), nxt),
            cvec, vec, vec, _const_spec(pool_w.shape), cvec, cvec, vec, cvec, vec, vec, _const_spec((e, d)),
        ],
        out_specs=out_specs,
        out_shape=out_shape,
        scratch_shapes=[pltpu.VMEM((tm + 2 * HALO, d), F32)],
        compiler_params=_params("arbitrary", "arbitrary"),
        name="pool_mixer",
    )(x, x, x, g1n, sh1, sc1, pool_w, pool_b, pool_scale, g1, g2n, sh2, sc2, rwt)


def _route_kernel(lgt_ref, bias_ref, tri_ref, eid_ref, pos_ref, wts_ref, cnt_ref, carry_ref):
    i = pl.program_id(0)
    e, tn = lgt_ref.shape
    eg = e // N_EXPERT_GROUPS

    @pl.when(i == 0)
    def _():
        carry_ref[...] = jnp.zeros_like(carry_ref)

    scores = _sigmoid(lgt_ref[...])
    sel = scores + bias_ref[...]
    neg = jnp.float32(-jnp.inf)
    sub = lax.broadcasted_iota(jnp.int32, (eg, tn), 0)
    group_scores = []
    for g in range(N_EXPERT_GROUPS):
        blk = sel[g * eg:(g + 1) * eg, :]
        m1 = jnp.max(blk, axis=0, keepdims=True)
        first = jnp.min(jnp.where(blk == m1, sub, eg), axis=0, keepdims=True)
        m2 = jnp.max(jnp.where(sub == first, neg, blk), axis=0, keepdims=True)
        group_scores.append(m1 + m2)
    masked = []
    for g in range(N_EXPERT_GROUPS):
        rank = jnp.zeros((1, tn), jnp.int32)
        for g2 in range(N_EXPERT_GROUPS):
            if g2 == g:
                continue
            ahead = (group_scores[g2] >= group_scores[g]) if g2 < g else (group_scores[g2] > group_scores[g])
            rank = rank + ahead.astype(jnp.int32)
        masked.append(jnp.where(rank < TOPK_GROUPS, sel[g * eg:(g + 1) * eg, :], neg))
    selm = jnp.concatenate(masked, axis=0)

    eidx = lax.broadcasted_iota(jnp.int32, (e, tn), 0)
    erank = jnp.zeros((e, tn), jnp.int32)
    for e2 in range(e):
        rowv = selm[e2:e2 + 1, :]
        ahead = jnp.where(rowv > selm, 1, jnp.where((rowv == selm) & (eidx > e2), 1, 0))
        erank = erank + ahead
    chosen = erank < TOP_K

    m = chosen.astype(BF16)
    pos = jnp.dot(m, tri_ref[...], preferred_element_type=F32) + carry_ref[...]
    carry_ref[...] = carry_ref[...] + jnp.sum(chosen.astype(F32), axis=1, keepdims=True)
    cnt_ref[...] = carry_ref[...]

    wsum = jnp.sum(jnp.where(chosen, scores, 0.0), axis=0, keepdims=True)
    eidf = eidx.astype(F32)
    for j in range(TOP_K):
        hit = erank == j
        eid_ref[j:j + 1, :] = jnp.sum(jnp.where(hit, eidf, 0.0), axis=0, keepdims=True).astype(jnp.int32)
        pos_ref[j:j + 1, :] = jnp.sum(jnp.where(hit, pos, 0.0), axis=0, keepdims=True).astype(jnp.int32)
        wj = jnp.sum(jnp.where(hit, scores, 0.0), axis=0, keepdims=True)
        wts_ref[j:j + 1, :] = wj / wsum * ROUTED_SCALE


def _route(lgt, router_bias):
    e, t = lgt.shape
    tn = _tile(t, 512)
    tri = (lax.broadcasted_iota(jnp.int32, (tn, tn), 0) < lax.broadcasted_iota(jnp.int32, (tn, tn), 1)).astype(BF16)
    col = lambda i: (0, i)
    return pl.pallas_call(
        _route_kernel,
        grid=(t // tn,),
        in_specs=[pl.BlockSpec((e, tn), col), _const_spec((e, 1)), _const_spec((tn, tn))],
        out_specs=[pl.BlockSpec((TOP_K, tn), col), pl.BlockSpec((TOP_K, tn), col), pl.BlockSpec((TOP_K, tn), col),
                   pl.BlockSpec((e, 1), lambda i: (0, 0))],
        out_shape=[jax.ShapeDtypeStruct((TOP_K, t), jnp.int32), jax.ShapeDtypeStruct((TOP_K, t), jnp.int32),
                   jax.ShapeDtypeStruct((TOP_K, t), F32), jax.ShapeDtypeStruct((e, 1), F32)],
        scratch_shapes=[pltpu.VMEM((e, 1), F32)],
        compiler_params=_params("arbitrary"),
        name="route",
    )(lgt, router_bias.reshape(e, 1), tri)


def _dest_kernel(poff_ref, eid_ref, pos_ref, dest_ref, *, sub):
    eid = eid_ref[...]
    acc = pos_ref[...]
    for e in range(poff_ref.shape[0]):
        acc = acc + jnp.where(eid == e, poff_ref[e], 0)
    dest_ref[...] = acc * sub


def _dest_rows(poff, eid, pos, sub):
    k, t = eid.shape
    tn = _tile(t, 2048)
    col = lambda i, *_: (0, i)
    return pl.pallas_call(
        functools.partial(_dest_kernel, sub=sub),
        grid_spec=pltpu.PrefetchScalarGridSpec(
            num_scalar_prefetch=1, grid=(t // tn,),
            in_specs=[pl.BlockSpec((k, tn), col), pl.BlockSpec((k, tn), col)],
            out_specs=pl.BlockSpec((k, tn), col)),
        out_shape=jax.ShapeDtypeStruct((k, t), jnp.int32),
        compiler_params=_params("arbitrary"),
        name="dest_rows",
    )(poff, eid, pos)


def _dispatch_kernel(pend_ref, pcnt_ref, dest_ref, h_ref, xs_ref, zero_ref, sem, zsem, *, tt, rows, sub):
    i = pl.program_id(0)
    n_exp = pend_ref.shape[0]

    def zero_copy(e):
        start = pl.multiple_of((pend_ref[e] - rows) * sub, rows * sub)
        return pltpu.make_async_copy(zero_ref, xs_ref.at[pl.ds(start, rows * sub)], zsem)

    @pl.when(i == 0)
    def _():
        zero_ref[...] = jnp.zeros_like(zero_ref)

        def start_zero(e, c):
            @pl.when(pcnt_ref[e] > 0)
            def _():
                zero_copy(e).start()
            return c

        def wait_zero(e, c):
            @pl.when(pcnt_ref[e] > 0)
            def _():
                zero_copy(e).wait()
            return c

        lax.fori_loop(0, n_exp, start_zero, 0)
        lax.fori_loop(0, n_exp, wait_zero, 0)

    def row_copy(t, j):
        src = h_ref.at[pl.ds(pl.multiple_of(t * sub, sub), sub)]
        dst = xs_ref.at[pl.ds(pl.multiple_of(dest_ref[j, t], sub), sub)]
        return pltpu.make_async_copy(src, dst, sem)

    def issue(t, c):
        for j in range(TOP_K):
            row_copy(t, j).start(priority=j % 2)
        return c

    def drain(t, c):
        for j in range(TOP_K):
            row_copy(t, j).wait()
        return c

    lax.fori_loop(0, tt, issue, 0)
    lax.fori_loop(0, tt, drain, 0)


def _dispatch(h2p, dest, pend, pcnt, n_rows, rows, sub):
    t = h2p.shape[0] // sub
    tt = _tile(t, 256)
    return pl.pallas_call(
        functools.partial(_dispatch_kernel, tt=tt, rows=rows, sub=sub),
        grid_spec=pltpu.PrefetchScalarGridSpec(
            num_scalar_prefetch=2, grid=(t // tt,),
            in_specs=[pl.BlockSpec((TOP_K, tt), lambda i, *_: (0, i), memory_space=pltpu.SMEM),
                      pl.BlockSpec((tt * sub, LANES), lambda i, *_: (i, 0))],
            out_specs=pl.BlockSpec(memory_space=pl.ANY),
            scratch_shapes=[pltpu.VMEM((rows * sub, LANES), jnp.uint32), pltpu.SemaphoreType.DMA,
                            pltpu.SemaphoreType.DMA]),
        out_shape=jax.ShapeDtypeStruct((n_rows * sub, LANES), jnp.uint32),
        compiler_params=_params("arbitrary"),
        name="dispatch",
    )(pend, pcnt, dest, h2p)


def _expert_kernel(blk_e_ref, nreal_ref, x_ref, wg_ref, wu_ref, wd_ref, y_ref, wgb_ref, wub_ref, wdb_ref,
                   *, rows, sub):
    b = pl.program_id(0)
    changed = jnp.logical_or(b == 0, blk_e_ref[b] != blk_e_ref[jnp.maximum(b - 1, 0)])

    @pl.when(jnp.logical_and(changed, b < nreal_ref[0]))
    def _():
        wgb_ref[...] = wg_ref[...].astype(BF16)
        wub_ref[...] = wu_ref[...].astype(BF16)
        wdb_ref[...] = wd_ref[...].astype(BF16)

    @pl.when(b < nreal_ref[0])
    def _():
        los, his = _load_token_tiles(x_ref, 0, rows, sub)
        x = jnp.concatenate([p.astype(BF16) for p in los + his], axis=1)
        g = jnp.dot(x, wgb_ref[...], preferred_element_type=F32)
        u = jnp.dot(x, wub_ref[...], preferred_element_type=F32)
        a = (g * _sigmoid(g) * u).astype(BF16)
        _store_token_tiles(jnp.dot(a, wdb_ref[...], preferred_element_type=F32), y_ref)


def _experts(xs, blk_e, n_real, w_gate, w_up, w_down, layer, rows, sub):
    d, ff = w_gate.shape[2:]
    nb = xs.shape[0] // (rows * sub)
    xrow = lambda b, be, nr: (jnp.minimum(b, jnp.maximum(nr[0] - 1, 0)), 0)
    wsel = lambda b, be, nr: (layer, be[b], 0, 0)
    return pl.pallas_call(
        functools.partial(_expert_kernel, rows=rows, sub=sub),
        grid_spec=pltpu.PrefetchScalarGridSpec(
            num_scalar_prefetch=2, grid=(nb,),
            in_specs=[pl.BlockSpec((rows * sub, LANES), xrow), pl.BlockSpec((None, None, d, ff), wsel),
                      pl.BlockSpec((None, None, d, ff), wsel), pl.BlockSpec((None, None, ff, d), wsel)],
            out_specs=pl.BlockSpec((rows * sub, LANES), xrow),
            scratch_shapes=[pltpu.VMEM((d, ff), BF16), pltpu.VMEM((d, ff), BF16), pltpu.VMEM((ff, d), BF16)]),
        out_shape=jax.ShapeDtypeStruct(xs.shape, jnp.uint32),
        compiler_params=_params("arbitrary"),
        name="experts",
    )(blk_e, n_real, xs, w_gate, w_up, w_down)


def _combine_kernel(dest_ref, ys_ref, wt_ref, x1_ref, h_ref, sg_ref, su_ref, sd_ref, g2_ref, o_ref, gbuf_ref, sem,
                    *, tt, sub):
    def row_copy(t, j):
        src = ys_ref.at[pl.ds(pl.multiple_of(dest_ref[j, t], sub), sub)]
        dst = gbuf_ref.at[pl.ds(pl.multiple_of((j * tt + t) * sub, sub), sub)]
        return pltpu.make_async_copy(src, dst, sem)

    def issue(t, c):
        for j in range(TOP_K):
            row_copy(t, j).start(priority=j % 2)
        return c

    def drain(t, c):
        for j in range(TOP_K):
            row_copy(t, j).wait()
        return c

    lax.fori_loop(0, tt, issue, 0)
    h = h_ref[...]
    g = jnp.dot(h, sg_ref[...], preferred_element_type=F32)
    u = jnp.dot(h, su_ref[...], preferred_element_type=F32)
    shared = jnp.dot((g * _sigmoid(g) * u).astype(BF16), sd_ref[...], preferred_element_type=F32)
    lax.fori_loop(0, tt, drain, 0)
    wt = wt_ref[...]
    acc = [jnp.zeros((tt, LANES), F32) for _ in range(2 * sub)]
    for j in range(TOP_K):
        wj = jnp.broadcast_to(wt[:, j:j + 1], (tt, LANES))
        los, his = _load_token_tiles(gbuf_ref, j * tt * sub, tt, sub)
        acc = [a + p * wj for a, p in zip(acc, los + his)]
    routed = jnp.concatenate(acc, axis=1)
    o_ref[...] = x1_ref[...] + g2_ref[...] * (routed + shared)


def _combine(ys, dest, wts_t, x1, h2, s_gate, s_up, s_down, g2, seq, sub):
    t, d = x1.shape
    sf = s_gate.shape[1]
    tt = _tile(seq, 128)
    row = lambda i, *_: (i, 0)
    return pl.pallas_call(
        functools.partial(_combine_kernel, tt=tt, sub=sub),
        grid=(t // tt,),
        in_specs=[
            pl.BlockSpec((TOP_K, tt), lambda i: (0, i), memory_space=pltpu.SMEM),
            pl.BlockSpec(memory_space=pl.ANY),
            pl.BlockSpec((tt, TOP_K), row), pl.BlockSpec((tt, d), row), pl.BlockSpec((tt, d), row),
            _const_spec((d, sf)), _const_spec((d, sf)), _const_spec((sf, d)),
            pl.BlockSpec((None, 1, d), lambda i: ((i * tt) // seq, 0, 0)),
        ],
        out_specs=pl.BlockSpec((tt, d), row),
        out_shape=jax.ShapeDtypeStruct((t, d), F32),
        scratch_shapes=[pltpu.VMEM((TOP_K * tt * sub, LANES), jnp.uint32), pltpu.SemaphoreType.DMA],
        compiler_params=_params("arbitrary"),
        name="combine",
    )(dest, ys, wts_t, x1, h2, s_gate, s_up, s_down, g2)


def _moe(x1, h2, h2p, lgt, g2, router_bias, w_gate, w_up, w_down, layer, s_gate, s_up, s_down):
    b, s, d = x1.shape
    t = b * s
    e = lgt.shape[0]
    sub = _token_sublanes(d)
    rows = _tile(t, MOE_BLOCK_ROWS)
    eid, pos, wts, cnt = _route(lgt, router_bias)
    counts = cnt[:, 0].astype(jnp.int32)
    pcnt = (counts + rows - 1) // rows * rows
    pend = jnp.cumsum(pcnt)
    poff = pend - pcnt
    nb = (t * TOP_K + e * (rows - 1)) // rows
    n_real = (pend[-1] // rows).astype(jnp.int32).reshape(1)
    starts = jnp.arange(nb, dtype=jnp.int32) * rows
    blk_e = jnp.minimum(jnp.sum((pend[None, :] <= starts[:, None]).astype(jnp.int32), axis=1), e - 1)
    dest = _dest_rows(poff.astype(jnp.int32), eid, pos, sub)
    xs = _dispatch(h2p, dest, pend.astype(jnp.int32), pcnt.astype(jnp.int32), nb * rows, rows, sub)
    ys = _experts(xs, blk_e, n_real, w_gate, w_up, w_down, layer, rows, sub)
    out = _combine(ys, dest, wts.T, x1.reshape(t, d), h2.reshape(t, d), s_gate.astype(BF16), s_up.astype(BF16),
                   s_down.astype(BF16), g2, s, sub)
    return out.reshape(b, s, d)


def _rope_tables(n_tok):
    axis_dim = HEAD_DIM // 2
    rows = n_tok // GRID_W
    r, col = jnp.meshgrid(jnp.arange(rows), jnp.arange(GRID_W), indexing="ij")
    pos = jnp.stack([r.reshape(-1), col.reshape(-1)], axis=-1).astype(F32)
    inv = ROPE_THETA ** (-jnp.arange(0, axis_dim, 2, dtype=F32) / axis_dim)
    ang = pos[:, :, None] * inv
    ang = jnp.broadcast_to(ang[:, :, None, :], (n_tok, 2, 2, axis_dim // 2)).reshape(n_tok, HEAD_DIM)
    sign = jnp.where((jnp.arange(HEAD_DIM) % axis_dim) < axis_dim // 2, -1.0, 1.0).astype(F32)
    return jnp.cos(ang), jnp.sin(ang) * sign


def kernel(x, c, ctx, c_ctx, w_mod, b_mod, norm1_g, norm2_g, mix_w_in, q_norm_g, k_norm_g, conv_w, conv_b,
           conv_norm_g, conv_norm_b, mix_w_out, pool_w, pool_b, pool_scale, router_w, router_bias,
           moe_w_gate, moe_w_up, moe_w_down, shared_w_gate, shared_w_up, shared_w_down):
    b, s, d = x.shape
    depth = w_mod.shape[0]
    assert depth == 2, "layer schedule below is written for an attention layer followed by a pooling layer"
    kv_w = KV_HEADS * HEAD_DIM
    q_w = mix_w_in.shape[2] - 2 * kv_w - d

    pad = (-(b + 1)) % SUBLANES
    cc = jnp.concatenate([c, c_ctx[None, :], jnp.zeros((pad, d), F32)], axis=0)
    mods = _modulation(cc, w_mod, b_mod)

    def mod(layer, k, rows=slice(0, b)):
        return mods[layer, rows, k * d:(k + 1) * d][:, None, :]

    vec = lambda a: a.reshape(1, -1)
    rwt = jnp.swapaxes(router_w, 1, 2)
    cos, sin_signed = _rope_tables(s)

    w_in = mix_w_in[0].astype(BF16)
    q, k, v, z = _inproj(x, mod(0, 0), mod(0, 1), vec(norm1_g[0]), w_in, vec(q_norm_g[0]), vec(k_norm_g[0]),
                         cos, sin_signed)
    ctx_row = slice(b, b + 1)
    kc, vc = _ctx_kv(ctx, mods[0, ctx_row, 0:d], mods[0, ctx_row, d:2 * d], vec(norm1_g[0]),
                     w_in[:, q_w:q_w + 2 * kv_w], vec(k_norm_g[0]))
    attn = _attention(q, k, v, kc, vc)
    cv = _conformer_conv(z, conv_w[0], vec(conv_b[0]), vec(conv_norm_g[0]), vec(conv_norm_b[0]))
    w_out = mix_w_out[0].astype(BF16)
    x1, h2, h2p, lgt = _outproj(attn, cv, x, w_out[:q_w], w_out[q_w:], mod(0, 2), vec(norm2_g[0]), mod(0, 3),
                                mod(0, 4), rwt[0])
    x = _moe(x1, h2, h2p, lgt, mod(0, 5), router_bias[0], moe_w_gate, moe_w_up, moe_w_down, 0,
             shared_w_gate[0], shared_w_up[0], shared_w_down[0])

    x1, h2, h2p, lgt = _pool_layer(x, vec(norm1_g[1]), mod(1, 0), mod(1, 1), pool_w[0].astype(BF16), vec(pool_b[0]),
                                   vec(pool_scale[0]), mod(1, 2), vec(norm2_g[1]), mod(1, 3), mod(1, 4), rwt[1])
    x = _moe(x1, h2, h2p, lgt, mod(1, 5), router_bias[1], moe_w_gate, moe_w_up, moe_w_down, 1,
             shared_w_gate[1], shared_w_up[1], shared_w_down[1])
    return x
```

```python
import functools

import jax
import jax.numpy as jnp
from jax import lax
from jax.experimental import pallas as pl
from jax.experimental.pallas import tpu as pltpu

HEAD_DIM = 128
KV_HEADS = 2
GRID_W = 64
ROPE_THETA = 10000.0
EPS = 1e-6
CONV_K = 31
CONV_PAD = CONV_K // 2
POOL_WINDOWS = (2, 4, 8, 16)
N_EXPERT_GROUPS = 8
TOPK_GROUPS = 4
TOP_K = 8
ROUTED_SCALE = 2.5

LANES = 128
SUBLANES = 8
HALO = 16
MOE_BLOCK_ROWS = 512
VMEM_LIMIT_BYTES = 56 * 1024 * 1024

F32 = jnp.float32
BF16 = jnp.bfloat16


def _tile(n, pref):
    t = min(n, pref)
    while n % t:
        t //= 2
    return t


def _params(*sem):
    return pltpu.CompilerParams(dimension_semantics=sem, vmem_limit_bytes=VMEM_LIMIT_BYTES)


def _const_spec(shape):
    nd = len(shape)
    return pl.BlockSpec(shape, lambda *_: (0,) * nd, pipeline_mode=pl.Buffered(1))


def _sigmoid(x):
    return 1.0 / (1.0 + jnp.exp(-x))


def _adaln(x, g, shift, scale):
    ms = jnp.mean(x * x, axis=-1, keepdims=True)
    return (x * lax.rsqrt(ms + EPS) * g) * (1.0 + scale) + shift


def _mod_kernel(c_ref, w_ref, b_ref, o_ref):
    c = c_ref[...]
    a = c * _sigmoid(c)
    o_ref[...] = jnp.dot(a.astype(BF16), w_ref[...].astype(BF16), preferred_element_type=F32) + b_ref[...]


def _modulation(cc, w_mod, b_mod):
    n_layers, d, n = w_mod.shape
    r = cc.shape[0]
    tn = _tile(n, 1024)
    return pl.pallas_call(
        _mod_kernel,
        grid=(n_layers, n // tn),
        in_specs=[
            pl.BlockSpec((r, d), lambda l, j: (0, 0)),
            pl.BlockSpec((None, d, tn), lambda l, j: (l, 0, j)),
            pl.BlockSpec((None, 1, tn), lambda l, j: (l, 0, j)),
        ],
        out_specs=pl.BlockSpec((None, r, tn), lambda l, j: (l, 0, j)),
        out_shape=jax.ShapeDtypeStruct((n_layers, r, n), F32),
        compiler_params=_params("arbitrary", "arbitrary"),
        name="modulation",
    )(cc, w_mod, b_mod.reshape(n_layers, 1, n))


def _head_norm(x, g):
    ms = jnp.mean(x * x, axis=-1, keepdims=True)
    return x * lax.rsqrt(ms + EPS) * g


def _rope(x, cos, sin_signed, first_half):
    up = pltpu.roll(x, HEAD_DIM - HEAD_DIM // 4, 1)
    dn = pltpu.roll(x, HEAD_DIM // 4, 1)
    return x * cos + jnp.where(first_half, up, dn) * sin_signed


def _inproj_kernel(x_ref, sh_ref, sc_ref, g_ref, w_ref, qg_ref, kg_ref, cos_ref, sin_ref,
                   q_ref, k_ref, v_ref, z_ref, *, n_heads, conv_c):
    h = _adaln(x_ref[...], g_ref[...], sh_ref[...], sc_ref[...]).astype(BF16)
    acc = jnp.dot(h, w_ref[...], preferred_element_type=F32)
    cos = cos_ref[...]
    sin = sin_ref[...]
    lane = lax.broadcasted_iota(jnp.int32, cos.shape, 1)
    first_half = (lane % (HEAD_DIM // 2)) < (HEAD_DIM // 4)
    q_w = n_heads * HEAD_DIM
    kv_w = KV_HEADS * HEAD_DIM
    q_scale = HEAD_DIM ** -0.5
    for hd in range(n_heads):
        sl = slice(hd * HEAD_DIM, (hd + 1) * HEAD_DIM)
        qn = _head_norm(acc[:, sl], qg_ref[...])
        q_ref[:, sl] = (_rope(qn, cos, sin, first_half) * q_scale).astype(q_ref.dtype)
    for hd in range(KV_HEADS):
        sl = slice(hd * HEAD_DIM, (hd + 1) * HEAD_DIM)
        kn = _head_norm(acc[:, q_w + hd * HEAD_DIM:q_w + (hd + 1) * HEAD_DIM], kg_ref[...])
        k_ref[:, sl] = _rope(kn, cos, sin, first_half).astype(k_ref.dtype)
    v_ref[...] = acc[:, q_w + kv_w:q_w + 2 * kv_w].astype(v_ref.dtype)
    u0 = q_w + 2 * kv_w
    a = acc[:, u0:u0 + conv_c]
    gate = acc[:, u0 + conv_c:u0 + 2 * conv_c]
    z_ref[...] = a * _sigmoid(gate)


def _inproj(x, sh, sc, g, w_in, q_g, k_g, cos, sin_signed):
    b, s, d = x.shape
    in_w = w_in.shape[1]
    kv_w = KV_HEADS * HEAD_DIM
    conv_c = d // 2
    q_w = in_w - 2 * kv_w - 2 * conv_c
    n_heads = q_w // HEAD_DIM
    tm = _tile(s, 512)
    row = lambda bi, i: (bi, i, 0)
    vec = pl.BlockSpec((None, 1, d), lambda bi, i: (bi, 0, 0))
    tab = pl.BlockSpec((tm, HEAD_DIM), lambda bi, i: (i, 0))
    return pl.pallas_call(
        functools.partial(_inproj_kernel, n_heads=n_heads, conv_c=conv_c),
        grid=(b, s // tm),
        in_specs=[
            pl.BlockSpec((None, tm, d), row), vec, vec, _const_spec((1, d)), _const_spec((d, in_w)),
            _const_spec((1, HEAD_DIM)), _const_spec((1, HEAD_DIM)), tab, tab,
        ],
        out_specs=[
            pl.BlockSpec((None, tm, q_w), row), pl.BlockSpec((None, tm, kv_w), row),
            pl.BlockSpec((None, tm, kv_w), row), pl.BlockSpec((None, tm, conv_c), row),
        ],
        out_shape=[
            jax.ShapeDtypeStruct((b, s, q_w), BF16), jax.ShapeDtypeStruct((b, s, kv_w), BF16),
            jax.ShapeDtypeStruct((b, s, kv_w), BF16), jax.ShapeDtypeStruct((b, s, conv_c), F32),
        ],
        compiler_params=_params("arbitrary", "arbitrary"),
        name="inproj",
    )(x, sh, sc, g, w_in, q_g, k_g, cos, sin_signed)


def _ctx_kv_kernel(x_ref, sh_ref, sc_ref, g_ref, w_ref, kg_ref, k_ref, v_ref):
    h = _adaln(x_ref[...], g_ref[...], sh_ref[...], sc_ref[...]).astype(BF16)
    acc = jnp.dot(h, w_ref[...], preferred_element_type=F32)
    kv_w = KV_HEADS * HEAD_DIM
    for hd in range(KV_HEADS):
        sl = slice(hd * HEAD_DIM, (hd + 1) * HEAD_DIM)
        k_ref[:, sl] = _head_norm(acc[:, sl], kg_ref[...]).astype(k_ref.dtype)
    v_ref[...] = acc[:, kv_w:2 * kv_w].astype(v_ref.dtype)


def _ctx_kv(ctx, sh, sc, g, w_kv, k_g):
    b, lc, d = ctx.shape
    kv_w = KV_HEADS * HEAD_DIM
    row = lambda bi: (bi, 0, 0)
    return pl.pallas_call(
        _ctx_kv_kernel,
        grid=(b,),
        in_specs=[
            pl.BlockSpec((None, lc, d), row), _const_spec((1, d)), _const_spec((1, d)), _const_spec((1, d)),
            _const_spec((d, 2 * kv_w)), _const_spec((1, HEAD_DIM)),
        ],
        out_specs=[pl.BlockSpec((None, lc, kv_w), row), pl.BlockSpec((None, lc, kv_w), row)],
        out_shape=[jax.ShapeDtypeStruct((b, lc, kv_w), BF16), jax.ShapeDtypeStruct((b, lc, kv_w), BF16)],
        compiler_params=_params("arbitrary"),
        name="ctx_kv",
    )(ctx, sh, sc, g, w_kv, k_g)


def _attn_kernel(q_ref, k_ref, v_ref, kc_ref, vc_ref, o_ref, *, group, tk):
    tq = q_ref.shape[0]
    rows = group * tq
    q = jnp.concatenate([q_ref[:, g * HEAD_DIM:(g + 1) * HEAD_DIM] for g in range(group)], axis=0)
    nt = (((1,), (1,)), ((), ()))
    chunks = [(k_ref, v_ref, c * tk, tk) for c in range(k_ref.shape[0] // tk)]
    chunks.append((kc_ref, vc_ref, 0, kc_ref.shape[0]))
    m = jnp.full((rows, 1), -jnp.inf, F32)
    l = jnp.zeros((rows, 1), F32)
    acc = jnp.zeros((rows, HEAD_DIM), F32)
    for kr, vr, start, size in chunks:
        s = lax.dot_general(q, kr[start:start + size, :], nt, preferred_element_type=F32)
        m_new = jnp.maximum(m, jnp.max(s, axis=-1, keepdims=True))
        alpha = jnp.exp(m - m_new)
        p = jnp.exp(s - m_new)
        l = alpha * l + jnp.sum(p, axis=-1, keepdims=True)
        acc = alpha * acc + jnp.dot(p.astype(BF16), vr[start:start + size, :], preferred_element_type=F32)
        m = m_new
    o = acc / l
    for g in range(group):
        o_ref[:, g * HEAD_DIM:(g + 1) * HEAD_DIM] = o[g * tq:(g + 1) * tq, :].astype(o_ref.dtype)


def _attention(q, k, v, kc, vc):
    b, s, q_w = q.shape
    lc = kc.shape[1]
    group = q_w // (KV_HEADS * HEAD_DIM)
    tq = _tile(s, 256)
    kv_spec = lambda n: pl.BlockSpec((None, n, HEAD_DIM), lambda bi, h, i: (bi, 0, h))
    q_spec = pl.BlockSpec((None, tq, group * HEAD_DIM), lambda bi, h, i: (bi, i, h))
    return pl.pallas_call(
        functools.partial(_attn_kernel, group=group, tk=_tile(s, 512)),
        grid=(b, KV_HEADS, s // tq),
        in_specs=[q_spec, kv_spec(s), kv_spec(s), kv_spec(lc), kv_spec(lc)],
        out_specs=q_spec,
        out_shape=jax.ShapeDtypeStruct((b, s, q_w), BF16),
        compiler_params=_params("arbitrary", "arbitrary", "arbitrary"),
        name="attention",
    )(q, k, v, kc, vc)


def _conv_kernel(zc_ref, zp_ref, zn_ref, w_ref, b_ref, g_ref, beta_ref, o_ref, win_ref, conv_ref,
                 *, ts, n_tiles, rb, cb):
    i = pl.program_id(1)
    c = zc_ref.shape[1]
    win_ref[HALO:HALO + ts, :] = zc_ref[...]
    win_ref[0:HALO, :] = jnp.where(i > 0, zp_ref[...], 0.0)
    win_ref[HALO + ts:2 * HALO + ts, :] = jnp.where(i < n_tiles - 1, zn_ref[...], 0.0)
    base = HALO - CONV_PAD
    n_a = -(-CONV_K // SUBLANES)
    qn = rb + SUBLANES
    for r in range(ts // rb):
        for cc in range(c // cb):
            cs = slice(cc * cb, (cc + 1) * cb)
            rows = win_ref[r * rb:r * rb + qn + SUBLANES * (n_a - 1), cs]
            acc = jnp.zeros((rb, cb), F32)
            for rr in range(SUBLANES):
                q = None
                for a in range(n_a):
                    kk = SUBLANES * a + rr
                    if kk < CONV_K:
                        term = rows[SUBLANES * a:SUBLANES * a + qn, :] * w_ref[kk:kk + 1, cs]
                        q = term if q is None else q + term
                acc = acc + pltpu.roll(q, (qn - base - rr) % qn, 0)[0:rb, :]
            conv_ref[r * rb:(r + 1) * rb, cs] = acc + b_ref[:, cs]
    z = conv_ref[...]
    mu = jnp.mean(z, axis=-1, keepdims=True)
    zc = z - mu
    var = jnp.mean(zc * zc, axis=-1, keepdims=True)
    y = zc * lax.rsqrt(var + EPS) * g_ref[...] + beta_ref[...]
    o_ref[...] = (y * _sigmoid(y)).astype(o_ref.dtype)


def _conformer_conv(z, conv_w, conv_b, cn_g, cn_b):
    b, s, c = z.shape
    ts = _tile(s, 128)
    n_tiles = s // ts
    hb = ts // HALO
    n_hb = s // HALO
    cur = lambda bi, i: (bi, i, 0)
    prev = lambda bi, i: (bi, jnp.maximum(i * hb - 1, 0), 0)
    nxt = lambda bi, i: (bi, jnp.minimum((i + 1) * hb, n_hb - 1), 0)
    return pl.pallas_call(
        functools.partial(_conv_kernel, ts=ts, n_tiles=n_tiles, rb=_tile(ts, 32), cb=_tile(c, LANES)),
        grid=(b, n_tiles),
        in_specs=[
            pl.BlockSpec((None, ts, c), cur), pl.BlockSpec((None, HALO, c), prev), pl.BlockSpec((None, HALO, c), nxt),
            _const_spec((CONV_K, c)), _const_spec((1, c)), _const_spec((1, c)), _const_spec((1, c)),
        ],
        out_specs=pl.BlockSpec((None, ts, c), cur),
        out_shape=jax.ShapeDtypeStruct((b, s, c), BF16),
        scratch_shapes=[pltpu.VMEM((ts + 2 * HALO, c), F32), pltpu.VMEM((ts, c), F32)],
        compiler_params=_params("arbitrary", "arbitrary"),
        name="conformer_conv",
    )(z, z, z, conv_w, conv_b, cn_g, cn_b)


def _token_sublanes(d):
    assert d % (2 * LANES) == 0
    return d // (2 * LANES)


def _pack_pair(lo, hi):
    lo_bits = lax.bitcast_convert_type(lo.astype(BF16).astype(F32), jnp.uint32) >> 16
    hi_bits = lax.bitcast_convert_type(hi.astype(BF16).astype(F32), jnp.uint32) & jnp.uint32(0xFFFF0000)
    return lo_bits | hi_bits


def _unpack_pair(w):
    lo = lax.bitcast_convert_type(w << 16, F32)
    hi = lax.bitcast_convert_type(w & jnp.uint32(0xFFFF0000), F32)
    return lo, hi


def _store_token_tiles(v, ref, start=0):
    m, d = v.shape
    sub = _token_sublanes(d)
    for s in range(sub):
        lo = v[:, s * LANES:(s + 1) * LANES]
        hi = v[:, d // 2 + s * LANES:d // 2 + (s + 1) * LANES]
        ref[pl.ds(start + s, m, stride=sub), :] = _pack_pair(lo, hi)


def _load_token_tiles(ref, start, m, sub):
    los, his = [], []
    for s in range(sub):
        lo, hi = _unpack_pair(ref[pl.ds(start + s, m, stride=sub), :])
        los.append(lo)
        his.append(hi)
    return los, his


def _finish_tile(xnew, g2n_ref, sh2_ref, sc2_ref, rwt_ref, x1_ref, h2_ref, h2p_ref, lgt_ref):
    x1_ref[...] = xnew
    h2 = _adaln(xnew, g2n_ref[...], sh2_ref[...], sc2_ref[...])
    h2b = h2.astype(BF16)
    h2_ref[...] = h2b
    _store_token_tiles(h2, h2p_ref)
    lgt_ref[...] = lax.dot_general(rwt_ref[...], h2b, (((1,), (1,)), ((), ())), preferred_element_type=F32)


def _outproj_kernel(a_ref, c_ref, x_ref, wa_ref, wc_ref, g1_ref, g2n_ref, sh2_ref, sc2_ref, rwt_ref,
                    x1_ref, h2_ref, h2p_ref, lgt_ref):
    y = jnp.dot(a_ref[...], wa_ref[...], preferred_element_type=F32)
    y = y + jnp.dot(c_ref[...], wc_ref[...], preferred_element_type=F32)
    _finish_tile(x_ref[...] + g1_ref[...] * y, g2n_ref, sh2_ref, sc2_ref, rwt_ref, x1_ref, h2_ref, h2p_ref, lgt_ref)


def _finish_specs(b, s, d, e, tm):
    nt = s // tm
    sub = _token_sublanes(d)
    row = lambda bi, i: (bi, i, 0)
    flat = lambda bi, i: (bi * nt + i, 0)
    out_specs = [pl.BlockSpec((None, tm, d), row), pl.BlockSpec((None, tm, d), row),
                 pl.BlockSpec((tm * sub, LANES), flat), pl.BlockSpec((e, tm), lambda bi, i: (0, bi * nt + i))]
    out_shape = [jax.ShapeDtypeStruct((b, s, d), F32), jax.ShapeDtypeStruct((b, s, d), BF16),
                 jax.ShapeDtypeStruct((b * s * sub, LANES), jnp.uint32), jax.ShapeDtypeStruct((e, b * s), F32)]
    return out_specs, out_shape


def _outproj(attn, cv, x, w_a, w_c, g1, g2n, sh2, sc2, rwt):
    b, s, d = x.shape
    e = rwt.shape[0]
    tm = _tile(s, 512)
    row = lambda bi, i: (bi, i, 0)
    vec = pl.BlockSpec((None, 1, d), lambda bi, i: (bi, 0, 0))
    out_specs, out_shape = _finish_specs(b, s, d, e, tm)
    return pl.pallas_call(
        _outproj_kernel,
        grid=(b, s // tm),
        in_specs=[
            pl.BlockSpec((None, tm, attn.shape[2]), row), pl.BlockSpec((None, tm, cv.shape[2]), row),
            pl.BlockSpec((None, tm, d), row), _const_spec(w_a.shape), _const_spec(w_c.shape),
            vec, _const_spec((1, d)), vec, vec, _const_spec((e, d)),
        ],
        out_specs=out_specs,
        out_shape=out_shape,
        compiler_params=_params("arbitrary", "arbitrary"),
        name="outproj",
    )(attn, cv, x, w_a, w_c, g1, g2n, sh2, sc2, rwt)


def _pool_kernel(xc_ref, xp_ref, xn_ref, g1n_ref, sh1_ref, sc1_ref, pw_ref, pb_ref, ps_ref, g1_ref,
                 g2n_ref, sh2_ref, sc2_ref, rwt_ref, x1_ref, h2_ref, h2p_ref, lgt_ref, win_ref, *, tm, n_tiles, seq):
    i = pl.program_id(1)
    d = xc_ref.shape[1]
    gc = d // len(POOL_WINDOWS)
    norm = lambda x: _adaln(x, g1n_ref[...], sh1_ref[...], sc1_ref[...])
    win_ref[HALO:HALO + tm, :] = norm(xc_ref[...])
    win_ref[0:HALO, :] = jnp.where(i > 0, norm(xp_ref[...]), 0.0)
    win_ref[HALO + tm:2 * HALO + tm, :] = jnp.where(i < n_tiles - 1, norm(xn_ref[...]), 0.0)
    t = i * tm + lax.broadcasted_iota(jnp.int32, (tm, 1), 0)
    n = tm + 2 * SUBLANES
    ahead = lambda a, k: pltpu.roll(a, n - k, 0)
    ys = []
    for gi, w in enumerate(POOL_WINDOWS):
        assert w in (2, 4, 8, 16)
        cs = slice(gi * gc, (gi + 1) * gc)
        run = win_ref[HALO - SUBLANES:HALO + tm + SUBLANES, cs]
        span = 1
        while 2 * span < w:
            run = run + ahead(run, span)
            span *= 2
        first = run[0:tm, :] if span == SUBLANES else ahead(run, SUBLANES - span)[0:tm, :]
        tot = first + run[SUBLANES:SUBLANES + tm, :]
        lo = jnp.clip(t - w // 2, 0, seq)
        hi = jnp.clip(t + w - w // 2, 0, seq)
        p = tot / (hi - lo).astype(F32) - win_ref[HALO:HALO + tm, cs]
        ys.append(jnp.dot(p.astype(BF16), pw_ref[gi], preferred_element_type=F32))
    y = (jnp.concatenate(ys, axis=-1) + pb_ref[...]) * ps_ref[...]
    _finish_tile(xc_ref[...] + g1_ref[...] * y, g2n_ref, sh2_ref, sc2_ref, rwt_ref, x1_ref, h2_ref, h2p_ref, lgt_ref)


def _pool_layer(x, g1n, sh1, sc1, pool_w, pool_b, pool_scale, g1, g2n, sh2, sc2, rwt):
    b, s, d = x.shape
    e = rwt.shape[0]
    tm = _tile(s, 256)
    n_tiles = s // tm
    hb = tm // HALO
    n_hb = s // HALO
    cur = lambda bi, i: (bi, i, 0)
    prev = lambda bi, i: (bi, jnp.maximum(i * hb - 1, 0), 0)
    nxt = lambda bi, i: (bi, jnp.minimum((i + 1) * hb, n_hb - 1), 0)
    vec = pl.BlockSpec((None, 1, d), lambda bi, i: (bi, 0, 0))
    cvec = _const_spec((1, d))
    out_specs, out_shape = _finish_specs(b, s, d, e, tm)
    return pl.pallas_call(
        functools.partial(_pool_kernel, tm=tm, n_tiles=n_tiles, seq=s),
        grid=(b, n_tiles),
        in_specs=[
            pl.BlockSpec((None, tm, d), cur), pl.BlockSpec((None, HALO, d), prev), pl.BlockSpec((None, HALO, d), nxt),
            cvec, vec, vec, _const_spec(pool_w.shape), cvec, cvec, vec, cvec, vec, vec, _const_spec((e, d)),
        ],
        out_specs=out_specs,
        out_shape=out_shape,
        scratch_shapes=[pltpu.VMEM((tm + 2 * HALO, d), F32)],
        compiler_params=_params("arbitrary", "arbitrary"),
        name="pool_mixer",
    )(x, x, x, g1n, sh1, sc1, pool_w, pool_b, pool_scale, g1, g2n, sh2, sc2, rwt)


def _route_kernel(lgt_ref, bias_ref, tri_ref, eid_ref, pos_ref, wts_ref, cnt_ref, carry_ref):
    i = pl.program_id(0)
    e, tn = lgt_ref.shape
    eg = e // N_EXPERT_GROUPS

    @pl.when(i == 0)
    def _():
        carry_ref[...] = jnp.zeros_like(carry_ref)

    scores = _sigmoid(lgt_ref[...])
    sel = scores + bias_ref[...]
    neg = jnp.float32(-jnp.inf)
    sub = lax.broadcasted_iota(jnp.int32, (eg, tn), 0)
    group_scores = []
    for g in range(N_EXPERT_GROUPS):
        blk = sel[g * eg:(g + 1) * eg, :]
        m1 = jnp.max(blk, axis=0, keepdims=True)
        first = jnp.min(jnp.where(blk == m1, sub, eg), axis=0, keepdims=True)
        m2 = jnp.max(jnp.where(sub == first, neg, blk), axis=0, keepdims=True)
        group_scores.append(m1 + m2)
    masked = []
    for g in range(N_EXPERT_GROUPS):
        rank = jnp.zeros((1, tn), jnp.int32)
        for g2 in range(N_EXPERT_GROUPS):
            if g2 == g:
                continue
            ahead = (group_scores[g2] >= group_scores[g]) if g2 < g else (group_scores[g2] > group_scores[g])
            rank = rank + ahead.astype(jnp.int32)
        masked.append(jnp.where(rank < TOPK_GROUPS, sel[g * eg:(g + 1) * eg, :], neg))
    selm = jnp.concatenate(masked, axis=0)

    eidx = lax.broadcasted_iota(jnp.int32, (e, tn), 0)
    erank = jnp.zeros((e, tn), jnp.int32)
    for e2 in range(e):
        rowv = selm[e2:e2 + 1, :]
        ahead = jnp.where(rowv > selm, 1, jnp.where((rowv == selm) & (eidx > e2), 1, 0))
        erank = erank + ahead
    chosen = erank < TOP_K

    m = chosen.astype(BF16)
    pos = jnp.dot(m, tri_ref[...], preferred_element_type=F32) + carry_ref[...]
    carry_ref[...] = carry_ref[...] + jnp.sum(chosen.astype(F32), axis=1, keepdims=True)
    cnt_ref[...] = carry_ref[...]

    wsum = jnp.sum(jnp.where(chosen, scores, 0.0), axis=0, keepdims=True)
    eidf = eidx.astype(F32)
    for j in range(TOP_K):
        hit = erank == j
        eid_ref[j:j + 1, :] = jnp.sum(jnp.where(hit, eidf, 0.0), axis=0, keepdims=True).astype(jnp.int32)
        pos_ref[j:j + 1, :] = jnp.sum(jnp.where(hit, pos, 0.0), axis=0, keepdims=True).astype(jnp.int32)
        wj = jnp.sum(jnp.where(hit, scores, 0.0), axis=0, keepdims=True)
        wts_ref[j:j + 1, :] = wj / wsum * ROUTED_SCALE


def _route(lgt, router_bias):
    e, t = lgt.shape
    tn = _tile(t, 512)
    tri = (lax.broadcasted_iota(jnp.int32, (tn, tn), 0) < lax.broadcasted_iota(jnp.int32, (tn, tn), 1)).astype(BF16)
    col = lambda i: (0, i)
    return pl.pallas_call(
        _route_kernel,
        grid=(t // tn,),
        in_specs=[pl.BlockSpec((e, tn), col), _const_spec((e, 1)), _const_spec((tn, tn))],
        out_specs=[pl.BlockSpec((TOP_K, tn), col), pl.BlockSpec((TOP_K, tn), col), pl.BlockSpec((TOP_K, tn), col),
                   pl.BlockSpec((e, 1), lambda i: (0, 0))],
        out_shape=[jax.ShapeDtypeStruct((TOP_K, t), jnp.int32), jax.ShapeDtypeStruct((TOP_K, t), jnp.int32),
                   jax.ShapeDtypeStruct((TOP_K, t), F32), jax.ShapeDtypeStruct((e, 1), F32)],
        scratch_shapes=[pltpu.VMEM((e, 1), F32)],
        compiler_params=_params("arbitrary"),
        name="route",
    )(lgt, router_bias.reshape(e, 1), tri)


def _dest_kernel(poff_ref, eid_ref, pos_ref, dest_ref, *, sub):
    eid = eid_ref[...]
    acc = pos_ref[...]
    for e in range(poff_ref.shape[0]):
        acc = acc + jnp.where(eid == e, poff_ref[e], 0)
    dest_ref[...] = acc * sub


def _dest_rows(poff, eid, pos, sub):
    k, t = eid.shape
    tn = _tile(t, 2048)
    col = lambda i, *_: (0, i)
    return pl.pallas_call(
        functools.partial(_dest_kernel, sub=sub),
        grid_spec=pltpu.PrefetchScalarGridSpec(
            num_scalar_prefetch=1, grid=(t // tn,),
            in_specs=[pl.BlockSpec((k, tn), col), pl.BlockSpec((k, tn), col)],
            out_specs=pl.BlockSpec((k, tn), col)),
        out_shape=jax.ShapeDtypeStruct((k, t), jnp.int32),
        compiler_params=_params("arbitrary"),
        name="dest_rows",
    )(poff, eid, pos)


def _dispatch_kernel(pend_ref, pcnt_ref, dest_ref, h_ref, xs_ref, zero_ref, sem, zsem, *, tt, rows, sub):
    i = pl.program_id(0)
    n_exp = pend_ref.shape[0]

    def zero_copy(e):
        start = pl.multiple_of((pend_ref[e] - rows) * sub, rows * sub)
        return pltpu.make_async_copy(zero_ref, xs_ref.at[pl.ds(start, rows * sub)], zsem)

    @pl.when(i == 0)
    def _():
        zero_ref[...] = jnp.zeros_like(zero_ref)

        def start_zero(e, c):
            @pl.when(pcnt_ref[e] > 0)
            def _():
                zero_copy(e).start()
            return c

        def wait_zero(e, c):
            @pl.when(pcnt_ref[e] > 0)
            def _():
                zero_copy(e).wait()
            return c

        lax.fori_loop(0, n_exp, start_zero, 0)
        lax.fori_loop(0, n_exp, wait_zero, 0)

    def row_copy(t, j):
        src = h_ref.at[pl.ds(pl.multiple_of(t * sub, sub), sub)]
        dst = xs_ref.at[pl.ds(pl.multiple_of(dest_ref[j, t], sub), sub)]
        return pltpu.make_async_copy(src, dst, sem)

    def issue(t, c):
        for j in range(TOP_K):
            row_copy(t, j).start(priority=j % 2)
        return c

    def drain(t, c):
        for j in range(TOP_K):
            row_copy(t, j).wait()
        return c

    lax.fori_loop(0, tt, issue, 0)
    lax.fori_loop(0, tt, drain, 0)


def _dispatch(h2p, dest, pend, pcnt, n_rows, rows, sub):
    t = h2p.shape[0] // sub
    tt = _tile(t, 256)
    return pl.pallas_call(
        functools.partial(_dispatch_kernel, tt=tt, rows=rows, sub=sub),
        grid_spec=pltpu.PrefetchScalarGridSpec(
            num_scalar_prefetch=2, grid=(t // tt,),
            in_specs=[pl.BlockSpec((TOP_K, tt), lambda i, *_: (0, i), memory_space=pltpu.SMEM),
                      pl.BlockSpec((tt * sub, LANES), lambda i, *_: (i, 0))],
            out_specs=pl.BlockSpec(memory_space=pl.ANY),
            scratch_shapes=[pltpu.VMEM((rows * sub, LANES), jnp.uint32), pltpu.SemaphoreType.DMA,
                            pltpu.SemaphoreType.DMA]),
        out_shape=jax.ShapeDtypeStruct((n_rows * sub, LANES), jnp.uint32),
        compiler_params=_params("arbitrary"),
        name="dispatch",
    )(pend, pcnt, dest, h2p)


def _expert_kernel(blk_e_ref, nreal_ref, x_ref, wg_ref, wu_ref, wd_ref, y_ref, wgb_ref, wub_ref, wdb_ref,
                   *, rows, sub):
    b = pl.program_id(0)
    changed = jnp.logical_or(b == 0, blk_e_ref[b] != blk_e_ref[jnp.maximum(b - 1, 0)])

    @pl.when(jnp.logical_and(changed, b < nreal_ref[0]))
    def _():
        wgb_ref[...] = wg_ref[...].astype(BF16)
        wub_ref[...] = wu_ref[...].astype(BF16)
        wdb_ref[...] = wd_ref[...].astype(BF16)

    @pl.when(b < nreal_ref[0])
    def _():
        los, his = _load_token_tiles(x_ref, 0, rows, sub)
        x = jnp.concatenate([p.astype(BF16) for p in los + his], axis=1)
        g = jnp.dot(x, wgb_ref[...], preferred_element_type=F32)
        u = jnp.dot(x, wub_ref[...], preferred_element_type=F32)
        a = (g * _sigmoid(g) * u).astype(BF16)
        _store_token_tiles(jnp.dot(a, wdb_ref[...], preferred_element_type=F32), y_ref)


def _experts(xs, blk_e, n_real, w_gate, w_up, w_down, layer, rows, sub):
    d, ff = w_gate.shape[2:]
    nb = xs.shape[0] // (rows * sub)
    xrow = lambda b, be, nr: (jnp.minimum(b, jnp.maximum(nr[0] - 1, 0)), 0)
    wsel = lambda b, be, nr: (layer, be[b], 0, 0)
    return pl.pallas_call(
        functools.partial(_expert_kernel, rows=rows, sub=sub),
        grid_spec=pltpu.PrefetchScalarGridSpec(
            num_scalar_prefetch=2, grid=(nb,),
            in_specs=[pl.BlockSpec((rows * sub, LANES), xrow), pl.BlockSpec((None, None, d, ff), wsel),
                      pl.BlockSpec((None, None, d, ff), wsel), pl.BlockSpec((None, None, ff, d), wsel)],
            out_specs=pl.BlockSpec((rows * sub, LANES), xrow),
            scratch_shapes=[pltpu.VMEM((d, ff), BF16), pltpu.VMEM((d, ff), BF16), pltpu.VMEM((ff, d), BF16)]),
        out_shape=jax.ShapeDtypeStruct(xs.shape, jnp.uint32),
        compiler_params=_params("arbitrary"),
        name="experts",
    )(blk_e, n_real, xs, w_gate, w_up, w_down)


def _combine_kernel(dest_ref, ys_ref, wt_ref, x1_ref, h_ref, sg_ref, su_ref, sd_ref, g2_ref, o_ref, gbuf_ref, sem,
                    *, tt, sub):
    def row_copy(t, j):
        src = ys_ref.at[pl.ds(pl.multiple_of(dest_ref[j, t], sub), sub)]
        dst = gbuf_ref.at[pl.ds(pl.multiple_of((j * tt + t) * sub, sub), sub)]
        return pltpu.make_async_copy(src, dst, sem)

    def issue(t, c):
        for j in range(TOP_K):
            row_copy(t, j).start(priority=j % 2)
        return c

    def drain(t, c):
        for j in range(TOP_K):
            row_copy(t, j).wait()
        return c

    lax.fori_loop(0, tt, issue, 0)
    h = h_ref[...]
    g = jnp.dot(h, sg_ref[...], preferred_element_type=F32)
    u = jnp.dot(h, su_ref[...], preferred_element_type=F32)
    shared = jnp.dot((g * _sigmoid(g) * u).astype(BF16), sd_ref[...], preferred_element_type=F32)
    lax.fori_loop(0, tt, drain, 0)
    wt = wt_ref[...]
    acc = [jnp.zeros((tt, LANES), F32) for _ in range(2 * sub)]
    for j in range(TOP_K):
        wj = jnp.broadcast_to(wt[:, j:j + 1], (tt, LANES))
        los, his = _load_token_tiles(gbuf_ref, j * tt * sub, tt, sub)
        acc = [a + p * wj for a, p in zip(acc, los + his)]
    routed = jnp.concatenate(acc, axis=1)
    o_ref[...] = x1_ref[...] + g2_ref[...] * (routed + shared)


def _combine(ys, dest, wts_t, x1, h2, s_gate, s_up, s_down, g2, seq, sub):
    t, d = x1.shape
    sf = s_gate.shape[1]
    tt = _tile(seq, 128)
    row = lambda i, *_: (i, 0)
    return pl.pallas_call(
        functools.partial(_combine_kernel, tt=tt, sub=sub),
        grid=(t // tt,),
        in_specs=[
            pl.BlockSpec((TOP_K, tt), lambda i: (0, i), memory_space=pltpu.SMEM),
            pl.BlockSpec(memory_space=pl.ANY),
            pl.BlockSpec((tt, TOP_K), row), pl.BlockSpec((tt, d), row), pl.BlockSpec((tt, d), row),
            _const_spec((d, sf)), _const_spec((d, sf)), _const_spec((sf, d)),
            pl.BlockSpec((None, 1, d), lambda i: ((i * tt) // seq, 0, 0)),
        ],
        out_specs=pl.BlockSpec((tt, d), row),
        out_shape=jax.ShapeDtypeStruct((t, d), F32),
        scratch_shapes=[pltpu.VMEM((TOP_K * tt * sub, LANES), jnp.uint32), pltpu.SemaphoreType.DMA],
        compiler_params=_params("arbitrary"),
        name="combine",
    )(dest, ys, wts_t, x1, h2, s_gate, s_up, s_down, g2)


def _moe(x1, h2, h2p, lgt, g2, router_bias, w_gate, w_up, w_down, layer, s_gate, s_up, s_down):
    b, s, d = x1.shape
    t = b * s
    e = lgt.shape[0]
    sub = _token_sublanes(d)
    rows = _tile(t, MOE_BLOCK_ROWS)
    eid, pos, wts, cnt = _route(lgt, router_bias)
    counts = cnt[:, 0].astype(jnp.int32)
    pcnt = (counts + rows - 1) // rows * rows
    pend = jnp.cumsum(pcnt)
    poff = pend - pcnt
    nb = (t * TOP_K + e * (rows - 1)) // rows
    n_real = (pend[-1] // rows).astype(jnp.int32).reshape(1)
    starts = jnp.arange(nb, dtype=jnp.int32) * rows
    blk_e = jnp.minimum(jnp.sum((pend[None, :] <= starts[:, None]).astype(jnp.int32), axis=1), e - 1)
    dest = _dest_rows(poff.astype(jnp.int32), eid, pos, sub)
    xs = _dispatch(h2p, dest, pend.astype(jnp.int32), pcnt.astype(jnp.int32), nb * rows, rows, sub)
    ys = _experts(xs, blk_e, n_real, w_gate, w_up, w_down, layer, rows, sub)
    out = _combine(ys, dest, wts.T, x1.reshape(t, d), h2.reshape(t, d), s_gate.astype(BF16), s_up.astype(BF16),
                   s_down.astype(BF16), g2, s, sub)
    return out.reshape(b, s, d)


def _rope_tables(n_tok):
    axis_dim = HEAD_DIM // 2
    rows = n_tok // GRID_W
    r, col = jnp.meshgrid(jnp.arange(rows), jnp.arange(GRID_W), indexing="ij")
    pos = jnp.stack([r.reshape(-1), col.reshape(-1)], axis=-1).astype(F32)
    inv = ROPE_THETA ** (-jnp.arange(0, axis_dim, 2, dtype=F32) / axis_dim)
    ang = pos[:, :, None] * inv
    ang = jnp.broadcast_to(ang[:, :, None, :], (n_tok, 2, 2, axis_dim // 2)).reshape(n_tok, HEAD_DIM)
    sign = jnp.where((jnp.arange(HEAD_DIM) % axis_dim) < axis_dim // 2, -1.0, 1.0).astype(F32)
    return jnp.cos(ang), jnp.sin(ang) * sign


def kernel(x, c, ctx, c_ctx, w_mod, b_mod, norm1_g, norm2_g, mix_w_in, q_norm_g, k_norm_g, conv_w, conv_b,
           conv_norm_g, conv_norm_b, mix_w_out, pool_w, pool_b, pool_scale, router_w, router_bias,
           moe_w_gate, moe_w_up, moe_w_down, shared_w_gate, shared_w_up, shared_w_down):
    b, s, d = x.shape
    depth = w_mod.shape[0]
    assert depth == 2, "layer schedule below is written for an attention layer followed by a pooling layer"
    kv_w = KV_HEADS * HEAD_DIM
    q_w = mix_w_in.shape[2] - 2 * kv_w - d

    pad = (-(b + 1)) % SUBLANES
    cc = jnp.concatenate([c, c_ctx[None, :], jnp.zeros((pad, d), F32)], axis=0)
    mods = _modulation(cc, w_mod, b_mod)

    def mod(layer, k, rows=slice(0, b)):
        return mods[layer, rows, k * d:(k + 1) * d][:, None, :]

    vec = lambda a: a.reshape(1, -1)
    rwt = jnp.swapaxes(router_w, 1, 2).astype(BF16)
    cos, sin_signed = _rope_tables(s)

    w_in = mix_w_in[0].astype(BF16)
    q, k, v, z = _inproj(x, mod(0, 0), mod(0, 1), vec(norm1_g[0]), w_in, vec(q_norm_g[0]), vec(k_norm_g[0]),
                         cos, sin_signed)
    ctx_row = slice(b, b + 1)
    kc, vc = _ctx_kv(ctx, mods[0, ctx_row, 0:d], mods[0, ctx_row, d:2 * d], vec(norm1_g[0]),
                     w_in[:, q_w:q_w + 2 * kv_w], vec(k_norm_g[0]))
    attn = _attention(q, k, v, kc, vc)
    cv = _conformer_conv(z, conv_w[0], vec(conv_b[0]), vec(conv_norm_g[0]), vec(conv_norm_b[0]))
    w_out = mix_w_out[0].astype(BF16)
    x1, h2, h2p, lgt = _outproj(attn, cv, x, w_out[:q_w], w_out[q_w:], mod(0, 2), vec(norm2_g[0]), mod(0, 3),
                                mod(0, 4), rwt[0])
    x = _moe(x1, h2, h2p, lgt, mod(0, 5), router_bias[0], moe_w_gate, moe_w_up, moe_w_down, 0,
             shared_w_gate[0], shared_w_up[0], shared_w_down[0])

    x1, h2, h2p, lgt = _pool_layer(x, vec(norm1_g[1]), mod(1, 0), mod(1, 1), pool_w[0].astype(BF16), vec(pool_b[0]),
                                   vec(pool_scale[0]), mod(1, 2), vec(norm2_g[1]), mod(1, 3), mod(1, 4), rwt[1])
    x = _moe(x1, h2, h2p, lgt, mod(1, 5), router_bias[1], moe_w_gate, moe_w_up, moe_w_down, 1,
             shared_w_gate[1], shared_w_up[1], shared_w_down[1])
    return x
```

```python
import functools

import jax
import jax.numpy as jnp
from jax import lax
from jax.experimental import pallas as pl
from jax.experimental.pallas import tpu as pltpu

HEAD_DIM = 128
KV_HEADS = 2
GRID_W = 64
ROPE_THETA = 10000.0
EPS = 1e-6
CONV_K = 31
CONV_PAD = CONV_K // 2
POOL_WINDOWS = (2, 4, 8, 16)
N_EXPERT_GROUPS = 8
TOPK_GROUPS = 4
TOP_K = 8
ROUTED_SCALE = 2.5

LANES = 128
SUBLANES = 8
HALO = 16
MOE_BLOCK_ROWS = 512
VMEM_LIMIT_BYTES = 56 * 1024 * 1024

F32 = jnp.float32
BF16 = jnp.bfloat16


def _tile(n, pref):
    t = min(n, pref)
    while n % t:
        t //= 2
    return t


def _params(*sem):
    return pltpu.CompilerParams(dimension_semantics=sem, vmem_limit_bytes=VMEM_LIMIT_BYTES)


def _const_spec(shape):
    nd = len(shape)
    return pl.BlockSpec(shape, lambda *_: (0,) * nd, pipeline_mode=pl.Buffered(1))


def _sigmoid(x):
    return 1.0 / (1.0 + jnp.exp(-x))


def _adaln(x, g, shift, scale):
    ms = jnp.mean(x * x, axis=-1, keepdims=True)
    return (x * lax.rsqrt(ms + EPS) * g) * (1.0 + scale) + shift


def _mod_kernel(c_ref, w_ref, b_ref, o_ref):
    c = c_ref[...]
    a = c * _sigmoid(c)
    o_ref[...] = jnp.dot(a.astype(BF16), w_ref[...].astype(BF16), preferred_element_type=F32) + b_ref[...]


def _modulation(cc, w_mod, b_mod):
    n_layers, d, n = w_mod.shape
    r = cc.shape[0]
    tn = _tile(n, 1024)
    return pl.pallas_call(
        _mod_kernel,
        grid=(n_layers, n // tn),
        in_specs=[
            pl.BlockSpec((r, d), lambda l, j: (0, 0)),
            pl.BlockSpec((None, d, tn), lambda l, j: (l, 0, j)),
            pl.BlockSpec((None, 1, tn), lambda l, j: (l, 0, j)),
        ],
        out_specs=pl.BlockSpec((None, r, tn), lambda l, j: (l, 0, j)),
        out_shape=jax.ShapeDtypeStruct((n_layers, r, n), F32),
        compiler_params=_params("arbitrary", "arbitrary"),
        name="modulation",
    )(cc, w_mod, b_mod.reshape(n_layers, 1, n))


def _head_norm(x, g):
    ms = jnp.mean(x * x, axis=-1, keepdims=True)
    return x * lax.rsqrt(ms + EPS) * g


def _rope(x, cos, sin_signed, first_half):
    up = pltpu.roll(x, HEAD_DIM - HEAD_DIM // 4, 1)
    dn = pltpu.roll(x, HEAD_DIM // 4, 1)
    return x * cos + jnp.where(first_half, up, dn) * sin_signed


def _inproj_kernel(x_ref, sh_ref, sc_ref, g_ref, w_ref, qg_ref, kg_ref, cos_ref, sin_ref,
                   q_ref, k_ref, v_ref, z_ref, *, n_heads, conv_c):
    h = _adaln(x_ref[...], g_ref[...], sh_ref[...], sc_ref[...]).astype(BF16)
    acc = jnp.dot(h, w_ref[...], preferred_element_type=F32)
    cos = cos_ref[...]
    sin = sin_ref[...]
    lane = lax.broadcasted_iota(jnp.int32, cos.shape, 1)
    first_half = (lane % (HEAD_DIM // 2)) < (HEAD_DIM // 4)
    q_w = n_heads * HEAD_DIM
    kv_w = KV_HEADS * HEAD_DIM
    q_scale = HEAD_DIM ** -0.5
    for hd in range(n_heads):
        sl = slice(hd * HEAD_DIM, (hd + 1) * HEAD_DIM)
        qn = _head_norm(acc[:, sl], qg_ref[...])
        q_ref[:, sl] = (_rope(qn, cos, sin, first_half) * q_scale).astype(q_ref.dtype)
    for hd in range(KV_HEADS):
        sl = slice(hd * HEAD_DIM, (hd + 1) * HEAD_DIM)
        kn = _head_norm(acc[:, q_w + hd * HEAD_DIM:q_w + (hd + 1) * HEAD_DIM], kg_ref[...])
        k_ref[:, sl] = _rope(kn, cos, sin, first_half).astype(k_ref.dtype)
    v_ref[...] = acc[:, q_w + kv_w:q_w + 2 * kv_w].astype(v_ref.dtype)
    u0 = q_w + 2 * kv_w
    a = acc[:, u0:u0 + conv_c]
    gate = acc[:, u0 + conv_c:u0 + 2 * conv_c]
    z_ref[...] = a * _sigmoid(gate)


def _inproj(x, sh, sc, g, w_in, q_g, k_g, cos, sin_signed):
    b, s, d = x.shape
    in_w = w_in.shape[1]
    kv_w = KV_HEADS * HEAD_DIM
    conv_c = d // 2
    q_w = in_w - 2 * kv_w - 2 * conv_c
    n_heads = q_w // HEAD_DIM
    tm = _tile(s, 512)
    row = lambda bi, i: (bi, i, 0)
    vec = pl.BlockSpec((None, 1, d), lambda bi, i: (bi, 0, 0))
    tab = pl.BlockSpec((tm, HEAD_DIM), lambda bi, i: (i, 0))
    return pl.pallas_call(
        functools.partial(_inproj_kernel, n_heads=n_heads, conv_c=conv_c),
        grid=(b, s // tm),
        in_specs=[
            pl.BlockSpec((None, tm, d), row), vec, vec, _const_spec((1, d)), _const_spec((d, in_w)),
            _const_spec((1, HEAD_DIM)), _const_spec((1, HEAD_DIM)), tab, tab,
        ],
        out_specs=[
            pl.BlockSpec((None, tm, q_w), row), pl.BlockSpec((None, tm, kv_w), row),
            pl.BlockSpec((None, tm, kv_w), row), pl.BlockSpec((None, tm, conv_c), row),
        ],
        out_shape=[
            jax.ShapeDtypeStruct((b, s, q_w), BF16), jax.ShapeDtypeStruct((b, s, kv_w), BF16),
            jax.ShapeDtypeStruct((b, s, kv_w), BF16), jax.ShapeDtypeStruct((b, s, conv_c), F32),
        ],
        compiler_params=_params("arbitrary", "arbitrary"),
        name="inproj",
    )(x, sh, sc, g, w_in, q_g, k_g, cos, sin_signed)


def _ctx_kv_kernel(x_ref, sh_ref, sc_ref, g_ref, w_ref, kg_ref, k_ref, v_ref):
    h = _adaln(x_ref[...], g_ref[...], sh_ref[...], sc_ref[...]).astype(BF16)
    acc = jnp.dot(h, w_ref[...], preferred_element_type=F32)
    kv_w = KV_HEADS * HEAD_DIM
    for hd in range(KV_HEADS):
        sl = slice(hd * HEAD_DIM, (hd + 1) * HEAD_DIM)
        k_ref[:, sl] = _head_norm(acc[:, sl], kg_ref[...]).astype(k_ref.dtype)
    v_ref[...] = acc[:, kv_w:2 * kv_w].astype(v_ref.dtype)


def _ctx_kv(ctx, sh, sc, g, w_kv, k_g):
    b, lc, d = ctx.shape
    kv_w = KV_HEADS * HEAD_DIM
    row = lambda bi: (bi, 0, 0)
    return pl.pallas_call(
        _ctx_kv_kernel,
        grid=(b,),
        in_specs=[
            pl.BlockSpec((None, lc, d), row), _const_spec((1, d)), _const_spec((1, d)), _const_spec((1, d)),
            _const_spec((d, 2 * kv_w)), _const_spec((1, HEAD_DIM)),
        ],
        out_specs=[pl.BlockSpec((None, lc, kv_w), row), pl.BlockSpec((None, lc, kv_w), row)],
        out_shape=[jax.ShapeDtypeStruct((b, lc, kv_w), BF16), jax.ShapeDtypeStruct((b, lc, kv_w), BF16)],
        compiler_params=_params("arbitrary"),
        name="ctx_kv",
    )(ctx, sh, sc, g, w_kv, k_g)


def _attn_kernel(q_ref, k_ref, v_ref, kc_ref, vc_ref, o_ref, *, group, tk):
    tq = q_ref.shape[0]
    rows = group * tq
    q = jnp.concatenate([q_ref[:, g * HEAD_DIM:(g + 1) * HEAD_DIM] for g in range(group)], axis=0)
    nt = (((1,), (1,)), ((), ()))
    chunks = [(k_ref, v_ref, c * tk, tk) for c in range(k_ref.shape[0] // tk)]
    chunks.append((kc_ref, vc_ref, 0, kc_ref.shape[0]))
    m = jnp.full((rows, 1), -jnp.inf, F32)
    l = jnp.zeros((rows, 1), F32)
    acc = jnp.zeros((rows, HEAD_DIM), F32)
    for kr, vr, start, size in chunks:
        s = lax.dot_general(q, kr[start:start + size, :], nt, preferred_element_type=F32)
        m_new = jnp.maximum(m, jnp.max(s, axis=-1, keepdims=True))
        alpha = jnp.exp(m - m_new)
        p = jnp.exp(s - m_new)
        l = alpha * l + jnp.sum(p, axis=-1, keepdims=True)
        acc = alpha * acc + jnp.dot(p.astype(BF16), vr[start:start + size, :], preferred_element_type=F32)
        m = m_new
    o = acc / l
    for g in range(group):
        o_ref[:, g * HEAD_DIM:(g + 1) * HEAD_DIM] = o[g * tq:(g + 1) * tq, :].astype(o_ref.dtype)


def _attention(q, k, v, kc, vc):
    b, s, q_w = q.shape
    lc = kc.shape[1]
    group = q_w // (KV_HEADS * HEAD_DIM)
    tq = _tile(s, 256)
    kv_spec = lambda n: pl.BlockSpec((None, n, HEAD_DIM), lambda bi, h, i: (bi, 0, h))
    q_spec = pl.BlockSpec((None, tq, group * HEAD_DIM), lambda bi, h, i: (bi, i, h))
    return pl.pallas_call(
        functools.partial(_attn_kernel, group=group, tk=_tile(s, 512)),
        grid=(b, KV_HEADS, s // tq),
        in_specs=[q_spec, kv_spec(s), kv_spec(s), kv_spec(lc), kv_spec(lc)],
        out_specs=q_spec,
        out_shape=jax.ShapeDtypeStruct((b, s, q_w), BF16),
        compiler_params=_params("arbitrary", "arbitrary", "arbitrary"),
        name="attention",
    )(q, k, v, kc, vc)


def _conv_kernel(zc_ref, zp_ref, zn_ref, w_ref, b_ref, g_ref, beta_ref, o_ref, win_ref, conv_ref,
                 *, ts, n_tiles, rb, cb):
    i = pl.program_id(1)
    c = zc_ref.shape[1]
    win_ref[HALO:HALO + ts, :] = zc_ref[...]
    win_ref[0:HALO, :] = jnp.where(i > 0, zp_ref[...], 0.0)
    win_ref[HALO + ts:2 * HALO + ts, :] = jnp.where(i < n_tiles - 1, zn_ref[...], 0.0)
    base = HALO - CONV_PAD
    n_a = -(-CONV_K // SUBLANES)
    qn = rb + SUBLANES
    for r in range(ts // rb):
        for cc in range(c // cb):
            cs = slice(cc * cb, (cc + 1) * cb)
            rows = win_ref[r * rb:r * rb + qn + SUBLANES * (n_a - 1), cs]
            acc = jnp.zeros((rb, cb), F32)
            for rr in range(SUBLANES):
                q = None
                for a in range(n_a):
                    kk = SUBLANES * a + rr
                    if kk < CONV_K:
                        term = rows[SUBLANES * a:SUBLANES * a + qn, :] * w_ref[kk:kk + 1, cs]
                        q = term if q is None else q + term
                acc = acc + pltpu.roll(q, (qn - base - rr) % qn, 0)[0:rb, :]
            conv_ref[r * rb:(r + 1) * rb, cs] = acc + b_ref[:, cs]
    z = conv_ref[...]
    mu = jnp.mean(z, axis=-1, keepdims=True)
    zc = z - mu
    var = jnp.mean(zc * zc, axis=-1, keepdims=True)
    y = zc * lax.rsqrt(var + EPS) * g_ref[...] + beta_ref[...]
    o_ref[...] = (y * _sigmoid(y)).astype(o_ref.dtype)


def _conformer_conv(z, conv_w, conv_b, cn_g, cn_b):
    b, s, c = z.shape
    ts = _tile(s, 128)
    n_tiles = s // ts
    hb = ts // HALO
    n_hb = s // HALO
    cur = lambda bi, i: (bi, i, 0)
    prev = lambda bi, i: (bi, jnp.maximum(i * hb - 1, 0), 0)
    nxt = lambda bi, i: (bi, jnp.minimum((i + 1) * hb, n_hb - 1), 0)
    return pl.pallas_call(
        functools.partial(_conv_kernel, ts=ts, n_tiles=n_tiles, rb=_tile(ts, 32), cb=_tile(c, LANES)),
        grid=(b, n_tiles),
        in_specs=[
            pl.BlockSpec((None, ts, c), cur), pl.BlockSpec((None, HALO, c), prev), pl.BlockSpec((None, HALO, c), nxt),
            _const_spec((CONV_K, c)), _const_spec((1, c)), _const_spec((1, c)), _const_spec((1, c)),
        ],
        out_specs=pl.BlockSpec((None, ts, c), cur),
        out_shape=jax.ShapeDtypeStruct((b, s, c), BF16),
        scratch_shapes=[pltpu.VMEM((ts + 2 * HALO, c), F32), pltpu.VMEM((ts, c), F32)],
        compiler_params=_params("arbitrary", "arbitrary"),
        name="conformer_conv",
    )(z, z, z, conv_w, conv_b, cn_g, cn_b)


def _token_sublanes(d):
    assert d % (2 * LANES) == 0
    return d // (2 * LANES)


def _pack_pair(lo, hi):
    lo_bits = lax.bitcast_convert_type(lo.astype(BF16).astype(F32), jnp.uint32) >> 16
    hi_bits = lax.bitcast_convert_type(hi.astype(BF16).astype(F32), jnp.uint32) & jnp.uint32(0xFFFF0000)
    return lo_bits | hi_bits


def _unpack_pair(w):
    lo = lax.bitcast_convert_type(w << 16, F32)
    hi = lax.bitcast_convert_type(w & jnp.uint32(0xFFFF0000), F32)
    return lo, hi


def _store_token_tiles(v, ref, start=0):
    m, d = v.shape
    sub = _token_sublanes(d)
    for s in range(sub):
        lo = v[:, s * LANES:(s + 1) * LANES]
        hi = v[:, d // 2 + s * LANES:d // 2 + (s + 1) * LANES]
        ref[pl.ds(start + s, m, stride=sub), :] = _pack_pair(lo, hi)


def _load_token_tiles(ref, start, m, sub):
    los, his = [], []
    for s in range(sub):
        lo, hi = _unpack_pair(ref[pl.ds(start + s, m, stride=sub), :])
        los.append(lo)
        his.append(hi)
    return los, his


def _finish_tile(xnew, g2n_ref, sh2_ref, sc2_ref, rwt_ref, x1_ref, h2_ref, h2p_ref, lgt_ref):
    x1_ref[...] = xnew
    h2 = _adaln(xnew, g2n_ref[...], sh2_ref[...], sc2_ref[...])
    h2b = h2.astype(BF16)
    h2_ref[...] = h2b
    _store_token_tiles(h2, h2p_ref)
    lgt_ref[...] = lax.dot_general(rwt_ref[...], h2b, (((1,), (1,)), ((), ())), preferred_element_type=F32)


def _outproj_kernel(a_ref, c_ref, x_ref, wa_ref, wc_ref, g1_ref, g2n_ref, sh2_ref, sc2_ref, rwt_ref,
                    x1_ref, h2_ref, h2p_ref, lgt_ref):
    y = jnp.dot(a_ref[...], wa_ref[...], preferred_element_type=F32)
    y = y + jnp.dot(c_ref[...], wc_ref[...], preferred_element_type=F32)
    _finish_tile(x_ref[...] + g1_ref[...] * y, g2n_ref, sh2_ref, sc2_ref, rwt_ref, x1_ref, h2_ref, h2p_ref, lgt_ref)


def _finish_specs(b, s, d, e, tm):
    nt = s // tm
    sub = _token_sublanes(d)
    row = lambda bi, i: (bi, i, 0)
    flat = lambda bi, i: (bi * nt + i, 0)
    out_specs = [pl.BlockSpec((None, tm, d), row), pl.BlockSpec((None, tm, d), row),
                 pl.BlockSpec((tm * sub, LANES), flat), pl.BlockSpec((e, tm), lambda bi, i: (0, bi * nt + i))]
    out_shape = [jax.ShapeDtypeStruct((b, s, d), F32), jax.ShapeDtypeStruct((b, s, d), BF16),
                 jax.ShapeDtypeStruct((b * s * sub, LANES), jnp.uint32), jax.ShapeDtypeStruct((e, b * s), F32)]
    return out_specs, out_shape


def _outproj(attn, cv, x, w_a, w_c, g1, g2n, sh2, sc2, rwt):
    b, s, d = x.shape
    e = rwt.shape[0]
    tm = _tile(s, 512)
    row = lambda bi, i: (bi, i, 0)
    vec = pl.BlockSpec((None, 1, d), lambda bi, i: (bi, 0, 0))
    out_specs, out_shape = _finish_specs(b, s, d, e, tm)
    return pl.pallas_call(
        _outproj_kernel,
        grid=(b, s // tm),
        in_specs=[
            pl.BlockSpec((None, tm, attn.shape[2]), row), pl.BlockSpec((None, tm, cv.shape[2]), row),
            pl.BlockSpec((None, tm, d), row), _const_spec(w_a.shape), _const_spec(w_c.shape),
            vec, _const_spec((1, d)), vec, vec, _const_spec((e, d)),
        ],
        out_specs=out_specs,
        out_shape=out_shape,
        compiler_params=_params("arbitrary", "arbitrary"),
        name="outproj",
    )(attn, cv, x, w_a, w_c, g1, g2n, sh2, sc2, rwt)


def _pool_kernel(xc_ref, xp_ref, xn_ref, g1n_ref, sh1_ref, sc1_ref, pw_ref, pb_ref, ps_ref, g1_ref,
                 g2n_ref, sh2_ref, sc2_ref, rwt_ref, x1_ref, h2_ref, h2p_ref, lgt_ref, win_ref, *, tm, n_tiles, seq):
    i = pl.program_id(1)
    d = xc_ref.shape[1]
    gc = d // len(POOL_WINDOWS)
    norm = lambda x: _adaln(x, g1n_ref[...], sh1_ref[...], sc1_ref[...])
    win_ref[HALO:HALO + tm, :] = norm(xc_ref[...])
    win_ref[0:HALO, :] = jnp.where(i > 0, norm(xp_ref[...]), 0.0)
    win_ref[HALO + tm:2 * HALO + tm, :] = jnp.where(i < n_tiles - 1, norm(xn_ref[...]), 0.0)
    t = i * tm + lax.broadcasted_iota(jnp.int32, (tm, 1), 0)
    n = tm + 2 * SUBLANES
    ahead = lambda a, k: pltpu.roll(a, n - k, 0)
    ys = []
    for gi, w in enumerate(POOL_WINDOWS):
        assert w in (2, 4, 8, 16)
        cs = slice(gi * gc, (gi + 1) * gc)
        run = win_ref[HALO - SUBLANES:HALO + tm + SUBLANES, cs]
        span = 1
        while 2 * span < w:
            run = run + ahead(run, span)
            span *= 2
        first = run[0:tm, :] if span == SUBLANES else ahead(run, SUBLANES - span)[0:tm, :]
        tot = first + run[SUBLANES:SUBLANES + tm, :]
        lo = jnp.clip(t - w // 2, 0, seq)
        hi = jnp.clip(t + w - w // 2, 0, seq)
        p = tot / (hi - lo).astype(F32) - win_ref[HALO:HALO + tm, cs]
        ys.append(jnp.dot(p.astype(BF16), pw_ref[gi], preferred_element_type=F32))
    y = (jnp.concatenate(ys, axis=-1) + pb_ref[...]) * ps_ref[...]
    _finish_tile(xc_ref[...] + g1_ref[...] * y, g2n_ref, sh2_ref, sc2_ref, rwt_ref, x1_ref, h2_ref, h2p_ref, lgt_ref)


def _pool_layer(x, g1n, sh1, sc1, pool_w, pool_b, pool_scale, g1, g2n, sh2, sc2, rwt):
    b, s, d = x.shape
    e = rwt.shape[0]
    tm = _tile(s, 256)
    n_tiles = s // tm
    hb = tm // HALO
    n_hb = s // HALO
    cur = lambda bi, i: (bi, i, 0)
    prev = lambda bi, i: (bi, jnp.maximum(i * hb - 1, 0), 0)
    nxt = lambda bi, i: (bi, jnp.minimum((i + 1) * hb, n_hb - 1), 0)
    vec = pl.BlockSpec((None, 1, d), lambda bi, i: (bi, 0, 0))
    cvec = _const_spec((1, d))
    out_specs, out_shape = _finish_specs(b, s, d, e, tm)
    return pl.pallas_call(
        functools.partial(_pool_kernel, tm=tm, n_tiles=n_tiles, seq=s),
        grid=(b, n_tiles),
        in_specs=[
            pl.BlockSpec((None, tm, d), cur), pl.BlockSpec((None, HALO, d), prev), pl.BlockSpec((None, HALO, d), nxt),
            cvec, vec, vec, _const_spec(pool_w.shape), cvec, cvec, vec, cvec, vec, vec, _const_spec((e, d)),
        ],
        out_specs=out_specs,
        out_shape=out_shape,
        scratch_shapes=[pltpu.VMEM((tm + 2 * HALO, d), F32)],
        compiler_params=_params("arbitrary", "arbitrary"),
        name="pool_mixer",
    )(x, x, x, g1n, sh1, sc1, pool_w, pool_b, pool_scale, g1, g2n, sh2, sc2, rwt)


def _route_kernel(lgt_ref, bias_ref, tri_ref, eid_ref, pos_ref, wts_ref, cnt_ref, carry_ref):
    i = pl.program_id(0)
    e, tn = lgt_ref.shape
    eg = e // N_EXPERT_GROUPS

    @pl.when(i == 0)
    def _():
        carry_ref[...] = jnp.zeros_like(carry_ref)

    scores = _sigmoid(lgt_ref[...])
    sel = scores + bias_ref[...]
    neg = jnp.float32(-jnp.inf)
    sub = lax.broadcasted_iota(jnp.int32, (eg, tn), 0)
    group_scores = []
    for g in range(N_EXPERT_GROUPS):
        blk = sel[g * eg:(g + 1) * eg, :]
        m1 = jnp.max(blk, axis=0, keepdims=True)
        first = jnp.min(jnp.where(blk == m1, sub, eg), axis=0, keepdims=True)
        m2 = jnp.max(jnp.where(sub == first, neg, blk), axis=0, keepdims=True)
        group_scores.append(m1 + m2)
    masked = []
    for g in range(N_EXPERT_GROUPS):
        rank = jnp.zeros((1, tn), jnp.int32)
        for g2 in range(N_EXPERT_GROUPS):
            if g2 == g:
                continue
            ahead = (group_scores[g2] >= group_scores[g]) if g2 < g else (group_scores[g2] > group_scores[g])
            rank = rank + ahead.astype(jnp.int32)
        masked.append(jnp.where(rank < TOPK_GROUPS, sel[g * eg:(g + 1) * eg, :], neg))
    selm = jnp.concatenate(masked, axis=0)

    eidx = lax.broadcasted_iota(jnp.int32, (e, tn), 0)
    erank = jnp.zeros((e, tn), jnp.int32)
    for e2 in range(e):
        rowv = selm[e2:e2 + 1, :]
        ahead = jnp.where(rowv > selm, 1, jnp.where((rowv == selm) & (eidx > e2), 1, 0))
        erank = erank + ahead
    chosen = erank < TOP_K

    m = chosen.astype(BF16)
    pos = jnp.dot(m, tri_ref[...], preferred_element_type=F32) + carry_ref[...]
    carry_ref[...] = carry_ref[...] + jnp.sum(chosen.astype(F32), axis=1, keepdims=True)
    cnt_ref[...] = carry_ref[...]

    wsum = jnp.sum(jnp.where(chosen, scores, 0.0), axis=0, keepdims=True)
    eidf = eidx.astype(F32)
    for j in range(TOP_K):
        hit = erank == j
        eid_ref[j:j + 1, :] = jnp.sum(jnp.where(hit, eidf, 0.0), axis=0, keepdims=True).astype(jnp.int32)
        pos_ref[j:j + 1, :] = jnp.sum(jnp.where(hit, pos, 0.0), axis=0, keepdims=True).astype(jnp.int32)
        wj = jnp.sum(jnp.where(hit, scores, 0.0), axis=0, keepdims=True)
        wts_ref[j:j + 1, :] = wj / wsum * ROUTED_SCALE


def _route(lgt, router_bias):
    e, t = lgt.shape
    tn = _tile(t, 512)
    tri = (lax.broadcasted_iota(jnp.int32, (tn, tn), 0) < lax.broadcasted_iota(jnp.int32, (tn, tn), 1)).astype(BF16)
    col = lambda i: (0, i)
    return pl.pallas_call(
        _route_kernel,
        grid=(t // tn,),
        in_specs=[pl.BlockSpec((e, tn), col), _const_spec((e, 1)), _const_spec((tn, tn))],
        out_specs=[pl.BlockSpec((TOP_K, tn), col), pl.BlockSpec((TOP_K, tn), col), pl.BlockSpec((TOP_K, tn), col),
                   pl.BlockSpec((e, 1), lambda i: (0, 0))],
        out_shape=[jax.ShapeDtypeStruct((TOP_K, t), jnp.int32), jax.ShapeDtypeStruct((TOP_K, t), jnp.int32),
                   jax.ShapeDtypeStruct((TOP_K, t), F32), jax.ShapeDtypeStruct((e, 1), F32)],
        scratch_shapes=[pltpu.VMEM((e, 1), F32)],
        compiler_params=_params("arbitrary"),
        name="route",
    )(lgt, router_bias.reshape(e, 1), tri)


def _dest_kernel(poff_ref, eid_ref, pos_ref, dest_ref, *, sub):
    eid = eid_ref[...]
    acc = pos_ref[...]
    for e in range(poff_ref.shape[0]):
        acc = acc + jnp.where(eid == e, poff_ref[e], 0)
    dest_ref[...] = acc * sub


def _dest_rows(poff, eid, pos, sub):
    k, t = eid.shape
    tn = _tile(t, 2048)
    col = lambda i, *_: (0, i)
    return pl.pallas_call(
        functools.partial(_dest_kernel, sub=sub),
        grid_spec=pltpu.PrefetchScalarGridSpec(
            num_scalar_prefetch=1, grid=(t // tn,),
            in_specs=[pl.BlockSpec((k, tn), col), pl.BlockSpec((k, tn), col)],
            out_specs=pl.BlockSpec((k, tn), col)),
        out_shape=jax.ShapeDtypeStruct((k, t), jnp.int32),
        compiler_params=_params("arbitrary"),
        name="dest_rows",
    )(poff, eid, pos)


def _dispatch_kernel(pend_ref, pcnt_ref, dest_ref, h_ref, xs_ref, zero_ref, sem, zsem, *, tt, rows, sub):
    i = pl.program_id(0)
    n_exp = pend_ref.shape[0]

    def zero_copy(e):
        start = pl.multiple_of((pend_ref[e] - rows) * sub, rows * sub)
        return pltpu.make_async_copy(zero_ref, xs_ref.at[pl.ds(start, rows * sub)], zsem)

    @pl.when(i == 0)
    def _():
        zero_ref[...] = jnp.zeros_like(zero_ref)

        def start_zero(e, c):
            @pl.when(pcnt_ref[e] > 0)
            def _():
                zero_copy(e).start()
            return c

        def wait_zero(e, c):
            @pl.when(pcnt_ref[e] > 0)
            def _():
                zero_copy(e).wait()
            return c

        lax.fori_loop(0, n_exp, start_zero, 0)
        lax.fori_loop(0, n_exp, wait_zero, 0)

    def row_copy(t, j):
        src = h_ref.at[pl.ds(pl.multiple_of(t * sub, sub), sub)]
        dst = xs_ref.at[pl.ds(pl.multiple_of(dest_ref[j, t], sub), sub)]
        return pltpu.make_async_copy(src, dst, sem)

    def issue(t, c):
        for j in range(TOP_K):
            row_copy(t, j).start(priority=j % 2)
        return c

    def drain(t, c):
        for j in range(TOP_K):
            row_copy(t, j).wait()
        return c

    lax.fori_loop(0, tt, issue, 0)
    lax.fori_loop(0, tt, drain, 0)


def _dispatch(h2p, dest, pend, pcnt, n_rows, rows, sub):
    t = h2p.shape[0] // sub
    tt = _tile(t, 256)
    return pl.pallas_call(
        functools.partial(_dispatch_kernel, tt=tt, rows=rows, sub=sub),
        grid_spec=pltpu.PrefetchScalarGridSpec(
            num_scalar_prefetch=2, grid=(t // tt,),
            in_specs=[pl.BlockSpec((TOP_K, tt), lambda i, *_: (0, i), memory_space=pltpu.SMEM),
                      pl.BlockSpec((tt * sub, LANES), lambda i, *_: (i, 0))],
            out_specs=pl.BlockSpec(memory_space=pl.ANY),
            scratch_shapes=[pltpu.VMEM((rows * sub, LANES), jnp.uint32), pltpu.SemaphoreType.DMA,
                            pltpu.SemaphoreType.DMA]),
        out_shape=jax.ShapeDtypeStruct((n_rows * sub, LANES), jnp.uint32),
        compiler_params=_params("arbitrary"),
        name="dispatch",
    )(pend, pcnt, dest, h2p)


def _expert_kernel(blk_e_ref, nreal_ref, x_ref, wg_ref, wu_ref, wd_ref, y_ref, wgb_ref, wub_ref, wdb_ref,
                   *, rows, sub):
    b = pl.program_id(0)
    changed = jnp.logical_or(b == 0, blk_e_ref[b] != blk_e_ref[jnp.maximum(b - 1, 0)])

    @pl.when(jnp.logical_and(changed, b < nreal_ref[0]))
    def _():
        wgb_ref[...] = wg_ref[...].astype(BF16)
        wub_ref[...] = wu_ref[...].astype(BF16)
        wdb_ref[...] = wd_ref[...].astype(BF16)

    @pl.when(b < nreal_ref[0])
    def _():
        los, his = _load_token_tiles(x_ref, 0, rows, sub)
        x = jnp.concatenate([p.astype(BF16) for p in los + his], axis=1)
        g = jnp.dot(x, wgb_ref[...], preferred_element_type=F32)
        u = jnp.dot(x, wub_ref[...], preferred_element_type=F32)
        a = (g * _sigmoid(g) * u).astype(BF16)
        _store_token_tiles(jnp.dot(a, wdb_ref[...], preferred_element_type=F32), y_ref)


def _experts(xs, blk_e, n_real, w_gate, w_up, w_down, layer, rows, sub):
    d, ff = w_gate.shape[2:]
    nb = xs.shape[0] // (rows * sub)
    xrow = lambda b, be, nr: (jnp.minimum(b, jnp.maximum(nr[0] - 1, 0)), 0)
    wsel = lambda b, be, nr: (layer, be[b], 0, 0)
    return pl.pallas_call(
        functools.partial(_expert_kernel, rows=rows, sub=sub),
        grid_spec=pltpu.PrefetchScalarGridSpec(
            num_scalar_prefetch=2, grid=(nb,),
            in_specs=[pl.BlockSpec((rows * sub, LANES), xrow), pl.BlockSpec((None, None, d, ff), wsel),
                      pl.BlockSpec((None, None, d, ff), wsel), pl.BlockSpec((None, None, ff, d), wsel)],
            out_specs=pl.BlockSpec((rows * sub, LANES), xrow),
            scratch_shapes=[pltpu.VMEM((d, ff), BF16), pltpu.VMEM((d, ff), BF16), pltpu.VMEM((ff, d), BF16)]),
        out_shape=jax.ShapeDtypeStruct(xs.shape, jnp.uint32),
        compiler_params=_params("arbitrary"),
        name="experts",
    )(blk_e, n_real, xs, w_gate, w_up, w_down)


def _combine_kernel(dest_ref, dnext_ref, ys_ref, wt_ref, x1_ref, h_ref, sg_ref, su_ref, sd_ref, g2_ref, o_ref,
                    gbuf_ref, sems, *, tt, sub):
    i = pl.program_id(0)
    slot_rows = TOP_K * tt * sub
    t_chunk = tt // TOP_K

    def gather(dref, col, slot, t, j):
        src = ys_ref.at[pl.ds(pl.multiple_of(dref[j, col + t], sub), sub)]
        dst = gbuf_ref.at[pl.ds(slot * slot_rows + (j * tt + t) * sub, sub)]
        return pltpu.make_async_copy(src, dst, sems.at[slot])

    def wait_tile(slot):
        def drain(t, c):
            for j in range(TOP_K):
                gather(dest_ref, 0, slot, 0, j).wait()
            return c
        lax.fori_loop(0, tt, drain, 0)

    @pl.when(i == 0)
    def _():
        def issue(t, c):
            for j in range(TOP_K):
                gather(dest_ref, 0, 0, t, j).start(priority=j % 2)
            return c
        lax.fori_loop(0, tt, issue, 0)

    def tile(slot, row0, dref, col):
        rows = slice(row0, row0 + tt)
        h = h_ref[rows, :]
        g = jnp.dot(h, sg_ref[...], preferred_element_type=F32)
        u = jnp.dot(h, su_ref[...], preferred_element_type=F32)
        shared = jnp.dot((g * _sigmoid(g) * u).astype(BF16), sd_ref[...], preferred_element_type=F32)
        wt = wt_ref[rows, :]
        acc = [jnp.zeros((tt, LANES), F32) for _ in range(2 * sub)]
        for j in range(TOP_K):
            for t in range(j * t_chunk, (j + 1) * t_chunk):
                for jj in range(TOP_K):
                    gather(dref, col, 1 - slot, t, jj).start(priority=jj % 2)
            wj = jnp.broadcast_to(wt[:, j:j + 1], (tt, LANES))
            los, his = _load_token_tiles(gbuf_ref, slot * slot_rows + j * tt * sub, tt, sub)
            acc = [a + p * wj for a, p in zip(acc, los + his)]
        routed = jnp.concatenate(acc, axis=1)
        o_ref[rows, :] = x1_ref[rows, :] + g2_ref[...] * (routed + shared)

    wait_tile(0)
    tile(0, 0, dest_ref, tt)
    wait_tile(1)
    tile(1, tt, dnext_ref, 0)

    @pl.when(i == pl.num_programs(0) - 1)
    def _():
        wait_tile(0)


def _combine(ys, dest, wts_t, x1, h2, s_gate, s_up, s_down, g2, seq, sub):
    t, d = x1.shape
    sf = s_gate.shape[1]
    tt = _tile(seq // 2, 128)
    assert tt % TOP_K == 0
    n_steps = t // (2 * tt)
    row = lambda i: (i, 0)
    return pl.pallas_call(
        functools.partial(_combine_kernel, tt=tt, sub=sub),
        grid=(n_steps,),
        in_specs=[
            pl.BlockSpec((TOP_K, 2 * tt), lambda i: (0, i), memory_space=pltpu.SMEM),
            pl.BlockSpec((TOP_K, tt), lambda i: (0, jnp.minimum(2 * i + 2, 2 * n_steps - 2)),
                         memory_space=pltpu.SMEM),
            pl.BlockSpec(memory_space=pl.ANY),
            pl.BlockSpec((2 * tt, TOP_K), row), pl.BlockSpec((2 * tt, d), row), pl.BlockSpec((2 * tt, d), row),
            _const_spec((d, sf)), _const_spec((d, sf)), _const_spec((sf, d)),
            pl.BlockSpec((None, 1, d), lambda i: ((i * 2 * tt) // seq, 0, 0)),
        ],
        out_specs=pl.BlockSpec((2 * tt, d), row),
        out_shape=jax.ShapeDtypeStruct((t, d), F32),
        scratch_shapes=[pltpu.VMEM((2 * TOP_K * tt * sub, LANES), jnp.uint32), pltpu.SemaphoreType.DMA((2,))],
        compiler_params=_params("arbitrary"),
        name="combine",
    )(dest, dest, ys, wts_t, x1, h2, s_gate, s_up, s_down, g2)


def _moe(x1, h2, h2p, lgt, g2, router_bias, w_gate, w_up, w_down, layer, s_gate, s_up, s_down):
    b, s, d = x1.shape
    t = b * s
    e = lgt.shape[0]
    sub = _token_sublanes(d)
    rows = _tile(t, MOE_BLOCK_ROWS)
    eid, pos, wts, cnt = _route(lgt, router_bias)
    counts = cnt[:, 0].astype(jnp.int32)
    pcnt = (counts + rows - 1) // rows * rows
    pend = jnp.cumsum(pcnt)
    poff = pend - pcnt
    nb = (t * TOP_K + e * (rows - 1)) // rows
    n_real = (pend[-1] // rows).astype(jnp.int32).reshape(1)
    starts = jnp.arange(nb, dtype=jnp.int32) * rows
    blk_e = jnp.minimum(jnp.sum((pend[None, :] <= starts[:, None]).astype(jnp.int32), axis=1), e - 1)
    dest = _dest_rows(poff.astype(jnp.int32), eid, pos, sub)
    xs = _dispatch(h2p, dest, pend.astype(jnp.int32), pcnt.astype(jnp.int32), nb * rows, rows, sub)
    ys = _experts(xs, blk_e, n_real, w_gate, w_up, w_down, layer, rows, sub)
    out = _combine(ys, dest, wts.T, x1.reshape(t, d), h2.reshape(t, d), s_gate.astype(BF16), s_up.astype(BF16),
                   s_down.astype(BF16), g2, s, sub)
    return out.reshape(b, s, d)


def _rope_tables(n_tok):
    axis_dim = HEAD_DIM // 2
    rows = n_tok // GRID_W
    r, col = jnp.meshgrid(jnp.arange(rows), jnp.arange(GRID_W), indexing="ij")
    pos = jnp.stack([r.reshape(-1), col.reshape(-1)], axis=-1).astype(F32)
    inv = ROPE_THETA ** (-jnp.arange(0, axis_dim, 2, dtype=F32) / axis_dim)
    ang = pos[:, :, None] * inv
    ang = jnp.broadcast_to(ang[:, :, None, :], (n_tok, 2, 2, axis_dim // 2)).reshape(n_tok, HEAD_DIM)
    sign = jnp.where((jnp.arange(HEAD_DIM) % axis_dim) < axis_dim // 2, -1.0, 1.0).astype(F32)
    return jnp.cos(ang), jnp.sin(ang) * sign


def kernel(x, c, ctx, c_ctx, w_mod, b_mod, norm1_g, norm2_g, mix_w_in, q_norm_g, k_norm_g, conv_w, conv_b,
           conv_norm_g, conv_norm_b, mix_w_out, pool_w, pool_b, pool_scale, router_w, router_bias,
           moe_w_gate, moe_w_up, moe_w_down, shared_w_gate, shared_w_up, shared_w_down):
    b, s, d = x.shape
    depth = w_mod.shape[0]
    assert depth == 2, "layer schedule below is written for an attention layer followed by a pooling layer"
    kv_w = KV_HEADS * HEAD_DIM
    q_w = mix_w_in.shape[2] - 2 * kv_w - d

    pad = (-(b + 1)) % SUBLANES
    cc = jnp.concatenate([c, c_ctx[None, :], jnp.zeros((pad, d), F32)], axis=0)
    mods = _modulation(cc, w_mod, b_mod)

    def mod(layer, k, rows=slice(0, b)):
        return mods[layer, rows, k * d:(k + 1) * d][:, None, :]

    vec = lambda a: a.reshape(1, -1)
    rwt = jnp.swapaxes(router_w, 1, 2).astype(BF16)
    cos, sin_signed = _rope_tables(s)

    w_in = mix_w_in[0].astype(BF16)
    q, k, v, z = _inproj(x, mod(0, 0), mod(0, 1), vec(norm1_g[0]), w_in, vec(q_norm_g[0]), vec(k_norm_g[0]),
                         cos, sin_signed)
    ctx_row = slice(b, b + 1)
    kc, vc = _ctx_kv(ctx, mods[0, ctx_row, 0:d], mods[0, ctx_row, d:2 * d], vec(norm1_g[0]),
                     w_in[:, q_w:q_w + 2 * kv_w], vec(k_norm_g[0]))
    attn = _attention(q, k, v, kc, vc)
    cv = _conformer_conv(z, conv_w[0], vec(conv_b[0]), vec(conv_norm_g[0]), vec(conv_norm_b[0]))
    w_out = mix_w_out[0].astype(BF16)
    x1, h2, h2p, lgt = _outproj(attn, cv, x, w_out[:q_w], w_out[q_w:], mod(0, 2), vec(norm2_g[0]), mod(0, 3),
                                mod(0, 4), rwt[0])
    x = _moe(x1, h2, h2p, lgt, mod(0, 5), router_bias[0], moe_w_gate, moe_w_up, moe_w_down, 0,
             shared_w_gate[0], shared_w_up[0], shared_w_down[0])

    x1, h2, h2p, lgt = _pool_layer(x, vec(norm1_g[1]), mod(1, 0), mod(1, 1), pool_w[0].astype(BF16), vec(pool_b[0]),
                                   vec(pool_scale[0]), mod(1, 2), vec(norm2_g[1]), mod(1, 3), mod(1, 4), rwt[1])
    x = _moe(x1, h2, h2p, lgt, mod(1, 5), router_bias[1], moe_w_gate, moe_w_up, moe_w_down, 1,
             shared_w_gate[1], shared_w_up[1], shared_w_down[1])
    return x
```

```python
import functools

import jax
import jax.numpy as jnp
from jax import lax
from jax.experimental import pallas as pl
from jax.experimental.pallas import tpu as pltpu

HEAD_DIM = 128
KV_HEADS = 2
GRID_W = 64
ROPE_THETA = 10000.0
EPS = 1e-6
CONV_K = 31
CONV_PAD = CONV_K // 2
POOL_WINDOWS = (2, 4, 8, 16)
N_EXPERT_GROUPS = 8
TOPK_GROUPS = 4
TOP_K = 8
ROUTED_SCALE = 2.5
LOG2_E = 1.4426950408889634

LANES = 128
SUBLANES = 8
HALO = 16
MOE_BLOCK_ROWS = 512
VMEM_LIMIT_BYTES = 56 * 1024 * 1024

F32 = jnp.float32
BF16 = jnp.bfloat16


def _tile(n, pref):
    t = min(n, pref)
    while n % t:
        t //= 2
    return t


def _params(*sem):
    return pltpu.CompilerParams(dimension_semantics=sem, vmem_limit_bytes=VMEM_LIMIT_BYTES)


def _const_spec(shape):
    nd = len(shape)
    return pl.BlockSpec(shape, lambda *_: (0,) * nd, pipeline_mode=pl.Buffered(1))


def _sigmoid(x):
    return 1.0 / (1.0 + jnp.exp(-x))


def _adaln(x, g, shift, scale):
    ms = jnp.mean(x * x, axis=-1, keepdims=True)
    return (x * lax.rsqrt(ms + EPS) * g) * (1.0 + scale) + shift


def _mod_kernel(c_ref, w_ref, b_ref, o_ref):
    c = c_ref[...]
    a = c * _sigmoid(c)
    o_ref[...] = jnp.dot(a.astype(BF16), w_ref[...].astype(BF16), preferred_element_type=F32) + b_ref[...]


def _modulation(cc, w_mod, b_mod):
    n_layers, d, n = w_mod.shape
    r = cc.shape[0]
    tn = _tile(n, 1024)
    return pl.pallas_call(
        _mod_kernel,
        grid=(n_layers, n // tn),
        in_specs=[
            pl.BlockSpec((r, d), lambda l, j: (0, 0)),
            pl.BlockSpec((None, d, tn), lambda l, j: (l, 0, j)),
            pl.BlockSpec((None, 1, tn), lambda l, j: (l, 0, j)),
        ],
        out_specs=pl.BlockSpec((None, r, tn), lambda l, j: (l, 0, j)),
        out_shape=jax.ShapeDtypeStruct((n_layers, r, n), F32),
        compiler_params=_params("arbitrary", "arbitrary"),
        name="modulation",
    )(cc, w_mod, b_mod.reshape(n_layers, 1, n))


def _head_norm(x, g):
    ms = jnp.mean(x * x, axis=-1, keepdims=True)
    return x * lax.rsqrt(ms + EPS) * g


def _rope(x, cos, sin_signed, first_half):
    up = pltpu.roll(x, HEAD_DIM - HEAD_DIM // 4, 1)
    dn = pltpu.roll(x, HEAD_DIM // 4, 1)
    return x * cos + jnp.where(first_half, up, dn) * sin_signed


def _inproj_kernel(x_ref, sh_ref, sc_ref, g_ref, w_ref, qg_ref, kg_ref, cos_ref, sin_ref,
                   q_ref, k_ref, v_ref, z_ref, *, n_heads, conv_c):
    h = _adaln(x_ref[...], g_ref[...], sh_ref[...], sc_ref[...]).astype(BF16)
    acc = jnp.dot(h, w_ref[...], preferred_element_type=F32)
    cos = cos_ref[...]
    sin = sin_ref[...]
    lane = lax.broadcasted_iota(jnp.int32, cos.shape, 1)
    first_half = (lane % (HEAD_DIM // 2)) < (HEAD_DIM // 4)
    q_w = n_heads * HEAD_DIM
    kv_w = KV_HEADS * HEAD_DIM
    q_scale = HEAD_DIM ** -0.5 * LOG2_E
    for hd in range(n_heads):
        sl = slice(hd * HEAD_DIM, (hd + 1) * HEAD_DIM)
        qn = _head_norm(acc[:, sl], qg_ref[...])
        q_ref[:, sl] = (_rope(qn, cos, sin, first_half) * q_scale).astype(q_ref.dtype)
    for hd in range(KV_HEADS):
        sl = slice(hd * HEAD_DIM, (hd + 1) * HEAD_DIM)
        kn = _head_norm(acc[:, q_w + hd * HEAD_DIM:q_w + (hd + 1) * HEAD_DIM], kg_ref[...])
        k_ref[:, sl] = _rope(kn, cos, sin, first_half).astype(k_ref.dtype)
    v_ref[...] = acc[:, q_w + kv_w:q_w + 2 * kv_w].astype(v_ref.dtype)
    u0 = q_w + 2 * kv_w
    a = acc[:, u0:u0 + conv_c]
    gate = acc[:, u0 + conv_c:u0 + 2 * conv_c]
    z_ref[...] = a * _sigmoid(gate)


def _inproj(x, sh, sc, g, w_in, q_g, k_g, cos, sin_signed):
    b, s, d = x.shape
    in_w = w_in.shape[1]
    kv_w = KV_HEADS * HEAD_DIM
    conv_c = d // 2
    q_w = in_w - 2 * kv_w - 2 * conv_c
    n_heads = q_w // HEAD_DIM
    tm = _tile(s, 512)
    row = lambda bi, i: (bi, i, 0)
    vec = pl.BlockSpec((None, 1, d), lambda bi, i: (bi, 0, 0))
    tab = pl.BlockSpec((tm, HEAD_DIM), lambda bi, i: (i, 0))
    return pl.pallas_call(
        functools.partial(_inproj_kernel, n_heads=n_heads, conv_c=conv_c),
        grid=(b, s // tm),
        in_specs=[
            pl.BlockSpec((None, tm, d), row), vec, vec, _const_spec((1, d)), _const_spec((d, in_w)),
            _const_spec((1, HEAD_DIM)), _const_spec((1, HEAD_DIM)), tab, tab,
        ],
        out_specs=[
            pl.BlockSpec((None, tm, q_w), row), pl.BlockSpec((None, tm, kv_w), row),
            pl.BlockSpec((None, tm, kv_w), row), pl.BlockSpec((None, tm, conv_c), row),
        ],
        out_shape=[
            jax.ShapeDtypeStruct((b, s, q_w), BF16), jax.ShapeDtypeStruct((b, s, kv_w), BF16),
            jax.ShapeDtypeStruct((b, s, kv_w), BF16), jax.ShapeDtypeStruct((b, s, conv_c), F32),
        ],
        compiler_params=_params("arbitrary", "arbitrary"),
        name="inproj",
    )(x, sh, sc, g, w_in, q_g, k_g, cos, sin_signed)


def _ctx_kv_kernel(x_ref, sh_ref, sc_ref, g_ref, w_ref, kg_ref, k_ref, v_ref):
    h = _adaln(x_ref[...], g_ref[...], sh_ref[...], sc_ref[...]).astype(BF16)
    acc = jnp.dot(h, w_ref[...], preferred_element_type=F32)
    kv_w = KV_HEADS * HEAD_DIM
    for hd in range(KV_HEADS):
        sl = slice(hd * HEAD_DIM, (hd + 1) * HEAD_DIM)
        k_ref[:, sl] = _head_norm(acc[:, sl], kg_ref[...]).astype(k_ref.dtype)
    v_ref[...] = acc[:, kv_w:2 * kv_w].astype(v_ref.dtype)


def _ctx_kv(ctx, sh, sc, g, w_kv, k_g):
    b, lc, d = ctx.shape
    kv_w = KV_HEADS * HEAD_DIM
    row = lambda bi: (bi, 0, 0)
    return pl.pallas_call(
        _ctx_kv_kernel,
        grid=(b,),
        in_specs=[
            pl.BlockSpec((None, lc, d), row), _const_spec((1, d)), _const_spec((1, d)), _const_spec((1, d)),
            _const_spec((d, 2 * kv_w)), _const_spec((1, HEAD_DIM)),
        ],
        out_specs=[pl.BlockSpec((None, lc, kv_w), row), pl.BlockSpec((None, lc, kv_w), row)],
        out_shape=[jax.ShapeDtypeStruct((b, lc, kv_w), BF16), jax.ShapeDtypeStruct((b, lc, kv_w), BF16)],
        compiler_params=_params("arbitrary"),
        name="ctx_kv",
    )(ctx, sh, sc, g, w_kv, k_g)


def _attn_kernel(q_ref, k_ref, v_ref, kc_ref, vc_ref, o_ref, *, group, tk):
    tq = q_ref.shape[0]
    rows = group * tq
    q = jnp.concatenate([q_ref[:, g * HEAD_DIM:(g + 1) * HEAD_DIM] for g in range(group)], axis=0)
    nt = (((1,), (1,)), ((), ()))
    chunks = [(k_ref, v_ref, c * tk, tk) for c in range(k_ref.shape[0] // tk)]
    chunks.append((kc_ref, vc_ref, 0, kc_ref.shape[0]))
    m = jnp.full((rows, 1), -jnp.inf, F32)
    l = jnp.zeros((rows, 1), F32)
    acc = jnp.zeros((rows, HEAD_DIM), F32)
    for kr, vr, start, size in chunks:
        s = lax.dot_general(q, kr[start:start + size, :], nt, preferred_element_type=F32)
        m_new = jnp.maximum(m, jnp.max(s, axis=-1, keepdims=True))
        alpha = jnp.exp2(m - m_new)
        p = jnp.exp2(s - m_new)
        l = alpha * l + jnp.sum(p, axis=-1, keepdims=True)
        acc = alpha * acc + jnp.dot(p.astype(BF16), vr[start:start + size, :], preferred_element_type=F32)
        m = m_new
    o = acc / l
    for g in range(group):
        o_ref[:, g * HEAD_DIM:(g + 1) * HEAD_DIM] = o[g * tq:(g + 1) * tq, :].astype(o_ref.dtype)


def _attention(q, k, v, kc, vc):
    b, s, q_w = q.shape
    lc = kc.shape[1]
    group = q_w // (KV_HEADS * HEAD_DIM)
    tq = _tile(s, 512)
    kv_spec = lambda n: pl.BlockSpec((None, n, HEAD_DIM), lambda bi, h, i: (bi, 0, h))
    q_spec = pl.BlockSpec((None, tq, group * HEAD_DIM), lambda bi, h, i: (bi, i, h))
    return pl.pallas_call(
        functools.partial(_attn_kernel, group=group, tk=_tile(s, 512)),
        grid=(b, KV_HEADS, s // tq),
        in_specs=[q_spec, kv_spec(s), kv_spec(s), kv_spec(lc), kv_spec(lc)],
        out_specs=q_spec,
        out_shape=jax.ShapeDtypeStruct((b, s, q_w), BF16),
        compiler_params=_params("arbitrary", "arbitrary", "arbitrary"),
        name="attention",
    )(q, k, v, kc, vc)


def _conv_kernel(zc_ref, zp_ref, zn_ref, w_ref, b_ref, g_ref, beta_ref, o_ref, win_ref, conv_ref,
                 *, ts, n_tiles, rb, cb):
    i = pl.program_id(1)
    c = zc_ref.shape[1]
    win_ref[HALO:HALO + ts, :] = zc_ref[...]
    win_ref[0:HALO, :] = jnp.where(i > 0, zp_ref[...], 0.0)
    win_ref[HALO + ts:2 * HALO + ts, :] = jnp.where(i < n_tiles - 1, zn_ref[...], 0.0)
    base = HALO - CONV_PAD
    n_a = -(-CONV_K // SUBLANES)
    qn = rb + SUBLANES
    for r in range(ts // rb):
        for cc in range(c // cb):
            cs = slice(cc * cb, (cc + 1) * cb)
            rows = win_ref[r * rb:r * rb + qn + SUBLANES * (n_a - 1), cs]
            acc = jnp.zeros((rb, cb), F32)
            for rr in range(SUBLANES):
                q = None
                for a in range(n_a):
                    kk = SUBLANES * a + rr
                    if kk < CONV_K:
                        term = rows[SUBLANES * a:SUBLANES * a + qn, :] * w_ref[kk:kk + 1, cs]
                        q = term if q is None else q + term
                acc = acc + pltpu.roll(q, (qn - base - rr) % qn, 0)[0:rb, :]
            conv_ref[r * rb:(r + 1) * rb, cs] = acc + b_ref[:, cs]
    z = conv_ref[...]
    mu = jnp.mean(z, axis=-1, keepdims=True)
    zc = z - mu
    var = jnp.mean(zc * zc, axis=-1, keepdims=True)
    y = zc * lax.rsqrt(var + EPS) * g_ref[...] + beta_ref[...]
    o_ref[...] = (y * _sigmoid(y)).astype(o_ref.dtype)


def _conformer_conv(z, conv_w, conv_b, cn_g, cn_b):
    b, s, c = z.shape
    ts = _tile(s, 128)
    n_tiles = s // ts
    hb = ts // HALO
    n_hb = s // HALO
    cur = lambda bi, i: (bi, i, 0)
    prev = lambda bi, i: (bi, jnp.maximum(i * hb - 1, 0), 0)
    nxt = lambda bi, i: (bi, jnp.minimum((i + 1) * hb, n_hb - 1), 0)
    return pl.pallas_call(
        functools.partial(_conv_kernel, ts=ts, n_tiles=n_tiles, rb=_tile(ts, 32), cb=_tile(c, LANES)),
        grid=(b, n_tiles),
        in_specs=[
            pl.BlockSpec((None, ts, c), cur), pl.BlockSpec((None, HALO, c), prev), pl.BlockSpec((None, HALO, c), nxt),
            _const_spec((CONV_K, c)), _const_spec((1, c)), _const_spec((1, c)), _const_spec((1, c)),
        ],
        out_specs=pl.BlockSpec((None, ts, c), cur),
        out_shape=jax.ShapeDtypeStruct((b, s, c), BF16),
        scratch_shapes=[pltpu.VMEM((ts + 2 * HALO, c), F32), pltpu.VMEM((ts, c), F32)],
        compiler_params=_params("arbitrary", "arbitrary"),
        name="conformer_conv",
    )(z, z, z, conv_w, conv_b, cn_g, cn_b)


def _token_sublanes(d):
    assert d % (2 * LANES) == 0
    return d // (2 * LANES)


def _pack_pair(lo, hi):
    lo_bits = lax.bitcast_convert_type(lo.astype(BF16).astype(F32), jnp.uint32) >> 16
    hi_bits = lax.bitcast_convert_type(hi.astype(BF16).astype(F32), jnp.uint32) & jnp.uint32(0xFFFF0000)
    return lo_bits | hi_bits


def _unpack_pair(w):
    lo = lax.bitcast_convert_type(w << 16, F32)
    hi = lax.bitcast_convert_type(w & jnp.uint32(0xFFFF0000), F32)
    return lo, hi


def _store_token_tiles(v, ref, start=0):
    m, d = v.shape
    sub = _token_sublanes(d)
    for s in range(sub):
        lo = v[:, s * LANES:(s + 1) * LANES]
        hi = v[:, d // 2 + s * LANES:d // 2 + (s + 1) * LANES]
        ref[pl.ds(start + s, m, stride=sub), :] = _pack_pair(lo, hi)


def _load_token_tiles(ref, start, m, sub):
    los, his = [], []
    for s in range(sub):
        lo, hi = _unpack_pair(ref[pl.ds(start + s, m, stride=sub), :])
        los.append(lo)
        his.append(hi)
    return los, his


def _finish_tile(xnew, g2n_ref, sh2_ref, sc2_ref, rwt_ref, x1_ref, h2_ref, h2p_ref, lgt_ref):
    x1_ref[...] = xnew
    h2 = _adaln(xnew, g2n_ref[...], sh2_ref[...], sc2_ref[...])
    h2b = h2.astype(BF16)
    h2_ref[...] = h2b
    _store_token_tiles(h2, h2p_ref)
    lgt_ref[...] = lax.dot_general(rwt_ref[...], h2b, (((1,), (1,)), ((), ())), preferred_element_type=F32)


def _outproj_kernel(a_ref, c_ref, x_ref, wa_ref, wc_ref, g1_ref, g2n_ref, sh2_ref, sc2_ref, rwt_ref,
                    x1_ref, h2_ref, h2p_ref, lgt_ref):
    y = jnp.dot(a_ref[...], wa_ref[...], preferred_element_type=F32)
    y = y + jnp.dot(c_ref[...], wc_ref[...], preferred_element_type=F32)
    _finish_tile(x_ref[...] + g1_ref[...] * y, g2n_ref, sh2_ref, sc2_ref, rwt_ref, x1_ref, h2_ref, h2p_ref, lgt_ref)


def _finish_specs(b, s, d, e, tm):
    nt = s // tm
    sub = _token_sublanes(d)
    row = lambda bi, i: (bi, i, 0)
    flat = lambda bi, i: (bi * nt + i, 0)
    out_specs = [pl.BlockSpec((None, tm, d), row), pl.BlockSpec((None, tm, d), row),
                 pl.BlockSpec((tm * sub, LANES), flat), pl.BlockSpec((e, tm), lambda bi, i: (0, bi * nt + i))]
    out_shape = [jax.ShapeDtypeStruct((b, s, d), F32), jax.ShapeDtypeStruct((b, s, d), BF16),
                 jax.ShapeDtypeStruct((b * s * sub, LANES), jnp.uint32), jax.ShapeDtypeStruct((e, b * s), F32)]
    return out_specs, out_shape


def _outproj(attn, cv, x, w_a, w_c, g1, g2n, sh2, sc2, rwt):
    b, s, d = x.shape
    e = rwt.shape[0]
    tm = _tile(s, 512)
    row = lambda bi, i: (bi, i, 0)
    vec = pl.BlockSpec((None, 1, d), lambda bi, i: (bi, 0, 0))
    out_specs, out_shape = _finish_specs(b, s, d, e, tm)
    return pl.pallas_call(
        _outproj_kernel,
        grid=(b, s // tm),
        in_specs=[
            pl.BlockSpec((None, tm, attn.shape[2]), row), pl.BlockSpec((None, tm, cv.shape[2]), row),
            pl.BlockSpec((None, tm, d), row), _const_spec(w_a.shape), _const_spec(w_c.shape),
            vec, _const_spec((1, d)), vec, vec, _const_spec((e, d)),
        ],
        out_specs=out_specs,
        out_shape=out_shape,
        compiler_params=_params("arbitrary", "arbitrary"),
        name="outproj",
    )(attn, cv, x, w_a, w_c, g1, g2n, sh2, sc2, rwt)


def _pool_kernel(xc_ref, xp_ref, xn_ref, g1n_ref, sh1_ref, sc1_ref, pw_ref, pb_ref, ps_ref, g1_ref,
                 g2n_ref, sh2_ref, sc2_ref, rwt_ref, x1_ref, h2_ref, h2p_ref, lgt_ref, win_ref, *, tm, n_tiles, seq):
    i = pl.program_id(1)
    d = xc_ref.shape[1]
    gc = d // len(POOL_WINDOWS)
    norm = lambda x: _adaln(x, g1n_ref[...], sh1_ref[...], sc1_ref[...])
    win_ref[HALO:HALO + tm, :] = norm(xc_ref[...])
    win_ref[0:HALO, :] = jnp.where(i > 0, norm(xp_ref[...]), 0.0)
    win_ref[HALO + tm:2 * HALO + tm, :] = jnp.where(i < n_tiles - 1, norm(xn_ref[...]), 0.0)
    t = i * tm + lax.broadcasted_iota(jnp.int32, (tm, 1), 0)
    n = tm + 2 * SUBLANES
    ahead = lambda a, k: pltpu.roll(a, n - k, 0)
    ys = []
    for gi, w in enumerate(POOL_WINDOWS):
        assert w in (2, 4, 8, 16)
        cs = slice(gi * gc, (gi + 1) * gc)
        run = win_ref[HALO - SUBLANES:HALO + tm + SUBLANES, cs]
        span = 1
        while 2 * span < w:
            run = run + ahead(run, span)
            span *= 2
        first = run[0:tm, :] if span == SUBLANES else ahead(run, SUBLANES - span)[0:tm, :]
        tot = first + run[SUBLANES:SUBLANES + tm, :]
        lo = jnp.clip(t - w // 2, 0, seq)
        hi = jnp.clip(t + w - w // 2, 0, seq)
        p = tot / (hi - lo).astype(F32) - win_ref[HALO:HALO + tm, cs]
        ys.append(jnp.dot(p.astype(BF16), pw_ref[gi], preferred_element_type=F32))
    y = (jnp.concatenate(ys, axis=-1) + pb_ref[...]) * ps_ref[...]
    _finish_tile(xc_ref[...] + g1_ref[...] * y, g2n_ref, sh2_ref, sc2_ref, rwt_ref, x1_ref, h2_ref, h2p_ref, lgt_ref)


def _pool_layer(x, g1n, sh1, sc1, pool_w, pool_b, pool_scale, g1, g2n, sh2, sc2, rwt):
    b, s, d = x.shape
    e = rwt.shape[0]
    tm = _tile(s, 256)
    n_tiles = s // tm
    hb = tm // HALO
    n_hb = s // HALO
    cur = lambda bi, i: (bi, i, 0)
    prev = lambda bi, i: (bi, jnp.maximum(i * hb - 1, 0), 0)
    nxt = lambda bi, i: (bi, jnp.minimum((i + 1) * hb, n_hb - 1), 0)
    vec = pl.BlockSpec((None, 1, d), lambda bi, i: (bi, 0, 0))
    cvec = _const_spec((1, d))
    out_specs, out_shape = _finish_specs(b, s, d, e, tm)
    return pl.pallas_call(
        functools.partial(_pool_kernel, tm=tm, n_tiles=n_tiles, seq=s),
        grid=(b, n_tiles),
        in_specs=[
            pl.BlockSpec((None, tm, d), cur), pl.BlockSpec((None, HALO, d), prev), pl.BlockSpec((None, HALO, d), nxt),
            cvec, vec, vec, _const_spec(pool_w.shape), cvec, cvec, vec, cvec, vec, vec, _const_spec((e, d)),
        ],
        out_specs=out_specs,
        out_shape=out_shape,
        scratch_shapes=[pltpu.VMEM((tm + 2 * HALO, d), F32)],
        compiler_params=_params("arbitrary", "arbitrary"),
        name="pool_mixer",
    )(x, x, x, g1n, sh1, sc1, pool_w, pool_b, pool_scale, g1, g2n, sh2, sc2, rwt)


def _route_kernel(lgt_ref, bias_ref, tri_ref, eid_ref, pos_ref, wts_ref, cnt_ref, carry_ref):
    i = pl.program_id(0)
    e, tn = lgt_ref.shape
    eg = e // N_EXPERT_GROUPS

    @pl.when(i == 0)
    def _():
        carry_ref[...] = jnp.zeros_like(carry_ref)

    scores = _sigmoid(lgt_ref[...])
    sel = scores + bias_ref[...]
    neg = jnp.float32(-jnp.inf)
    sub = lax.broadcasted_iota(jnp.int32, (eg, tn), 0)
    group_scores = []
    for g in range(N_EXPERT_GROUPS):
        blk = sel[g * eg:(g + 1) * eg, :]
        m1 = jnp.max(blk, axis=0, keepdims=True)
        first = jnp.min(jnp.where(blk == m1, sub, eg), axis=0, keepdims=True)
        m2 = jnp.max(jnp.where(sub == first, neg, blk), axis=0, keepdims=True)
        group_scores.append(m1 + m2)
    masked = []
    for g in range(N_EXPERT_GROUPS):
        rank = jnp.zeros((1, tn), jnp.int32)
        for g2 in range(N_EXPERT_GROUPS):
            if g2 == g:
                continue
            ahead = (group_scores[g2] >= group_scores[g]) if g2 < g else (group_scores[g2] > group_scores[g])
            rank = rank + ahead.astype(jnp.int32)
        masked.append(jnp.where(rank < TOPK_GROUPS, sel[g * eg:(g + 1) * eg, :], neg))
    selm = jnp.concatenate(masked, axis=0)

    eidx = lax.broadcasted_iota(jnp.int32, (e, tn), 0)
    work = selm
    picks = []
    for _ in range(TOP_K):
        mx = jnp.max(work, axis=0, keepdims=True)
        pick = jnp.min(jnp.where(work == mx, eidx, e), axis=0, keepdims=True)
        picks.append(pick)
        work = jnp.where(eidx == pick, neg, work)
    chosen = functools.reduce(jnp.logical_or, [eidx == p for p in picks])

    m = chosen.astype(BF16)
    pos = jnp.dot(m, tri_ref[...], preferred_element_type=F32) + carry_ref[...]
    carry_ref[...] = carry_ref[...] + jnp.sum(chosen.astype(F32), axis=1, keepdims=True)
    cnt_ref[...] = carry_ref[...]

    wsum = jnp.sum(jnp.where(chosen, scores, 0.0), axis=0, keepdims=True)
    for j, pick in enumerate(picks):
        hit = eidx == pick
        eid_ref[j:j + 1, :] = pick
        pos_ref[j:j + 1, :] = jnp.sum(jnp.where(hit, pos, 0.0), axis=0, keepdims=True).astype(jnp.int32)
        wj = jnp.sum(jnp.where(hit, scores, 0.0), axis=0, keepdims=True)
        wts_ref[j:j + 1, :] = wj / wsum * ROUTED_SCALE


def _route(lgt, router_bias):
    e, t = lgt.shape
    tn = _tile(t, 512)
    tri = (lax.broadcasted_iota(jnp.int32, (tn, tn), 0) < lax.broadcasted_iota(jnp.int32, (tn, tn), 1)).astype(BF16)
    col = lambda i: (0, i)
    return pl.pallas_call(
        _route_kernel,
        grid=(t // tn,),
        in_specs=[pl.BlockSpec((e, tn), col), _const_spec((e, 1)), _const_spec((tn, tn))],
        out_specs=[pl.BlockSpec((TOP_K, tn), col), pl.BlockSpec((TOP_K, tn), col), pl.BlockSpec((TOP_K, tn), col),
                   pl.BlockSpec((e, 1), lambda i: (0, 0))],
        out_shape=[jax.ShapeDtypeStruct((TOP_K, t), jnp.int32), jax.ShapeDtypeStruct((TOP_K, t), jnp.int32),
                   jax.ShapeDtypeStruct((TOP_K, t), F32), jax.ShapeDtypeStruct((e, 1), F32)],
        scratch_shapes=[pltpu.VMEM((e, 1), F32)],
        compiler_params=_params("arbitrary"),
        name="route",
    )(lgt, router_bias.reshape(e, 1), tri)


def _dest_kernel(poff_ref, eid_ref, pos_ref, dest_ref, *, sub):
    eid = eid_ref[...]
    acc = pos_ref[...]
    for e in range(poff_ref.shape[0]):
        acc = acc + jnp.where(eid == e, poff_ref[e], 0)
    dest_ref[...] = acc * sub


def _dest_rows(poff, eid, pos, sub):
    k, t = eid.shape
    tn = _tile(t, 2048)
    col = lambda i, *_: (0, i)
    return pl.pallas_call(
        functools.partial(_dest_kernel, sub=sub),
        grid_spec=pltpu.PrefetchScalarGridSpec(
            num_scalar_prefetch=1, grid=(t // tn,),
            in_specs=[pl.BlockSpec((k, tn), col), pl.BlockSpec((k, tn), col)],
            out_specs=pl.BlockSpec((k, tn), col)),
        out_shape=jax.ShapeDtypeStruct((k, t), jnp.int32),
        compiler_params=_params("arbitrary"),
        name="dest_rows",
    )(poff, eid, pos)


def _dispatch_kernel(pend_ref, pcnt_ref, dest_ref, h_ref, xs_ref, zero_ref, sem, zsem, *, tt, rows, sub):
    i = pl.program_id(0)
    n_exp = pend_ref.shape[0]

    def zero_copy(e):
        start = pl.multiple_of((pend_ref[e] - rows) * sub, rows * sub)
        return pltpu.make_async_copy(zero_ref, xs_ref.at[pl.ds(start, rows * sub)], zsem)

    @pl.when(i == 0)
    def _():
        zero_ref[...] = jnp.zeros_like(zero_ref)

        def start_zero(e, c):
            @pl.when(pcnt_ref[e] > 0)
            def _():
                zero_copy(e).start()
            return c

        def wait_zero(e, c):
            @pl.when(pcnt_ref[e] > 0)
            def _():
                zero_copy(e).wait()
            return c

        lax.fori_loop(0, n_exp, start_zero, 0)
        lax.fori_loop(0, n_exp, wait_zero, 0)

    def row_copy(t, j):
        src = h_ref.at[pl.ds(pl.multiple_of(t * sub, sub), sub)]
        dst = xs_ref.at[pl.ds(pl.multiple_of(dest_ref[j, t], sub), sub)]
        return pltpu.make_async_copy(src, dst, sem)

    def issue(t, c):
        for j in range(TOP_K):
            row_copy(t, j).start(priority=j % 2)
        return c

    def drain(t, c):
        for j in range(TOP_K):
            row_copy(t, j).wait()
        return c

    lax.fori_loop(0, tt, issue, 0)
    lax.fori_loop(0, tt, drain, 0)


def _dispatch(h2p, dest, pend, pcnt, n_rows, rows, sub):
    t = h2p.shape[0] // sub
    tt = _tile(t, 256)
    return pl.pallas_call(
        functools.partial(_dispatch_kernel, tt=tt, rows=rows, sub=sub),
        grid_spec=pltpu.PrefetchScalarGridSpec(
            num_scalar_prefetch=2, grid=(t // tt,),
            in_specs=[pl.BlockSpec((TOP_K, tt), lambda i, *_: (0, i), memory_space=pltpu.SMEM),
                      pl.BlockSpec((tt * sub, LANES), lambda i, *_: (i, 0))],
            out_specs=pl.BlockSpec(memory_space=pl.ANY),
            scratch_shapes=[pltpu.VMEM((rows * sub, LANES), jnp.uint32), pltpu.SemaphoreType.DMA,
                            pltpu.SemaphoreType.DMA]),
        out_shape=jax.ShapeDtypeStruct((n_rows * sub, LANES), jnp.uint32),
        compiler_params=_params("arbitrary"),
        name="dispatch",
    )(pend, pcnt, dest, h2p)


def _expert_kernel(blk_e_ref, nreal_ref, x_ref, wg_ref, wu_ref, wd_ref, y_ref, wgb_ref, wub_ref, wdb_ref,
                   *, rows, sub):
    b = pl.program_id(0)
    changed = jnp.logical_or(b == 0, blk_e_ref[b] != blk_e_ref[jnp.maximum(b - 1, 0)])

    @pl.when(jnp.logical_and(changed, b < nreal_ref[0]))
    def _():
        wgb_ref[...] = wg_ref[...].astype(BF16)
        wub_ref[...] = wu_ref[...].astype(BF16)
        wdb_ref[...] = wd_ref[...].astype(BF16)

    @pl.when(b < nreal_ref[0])
    def _():
        los, his = _load_token_tiles(x_ref, 0, rows, sub)
        x = jnp.concatenate([p.astype(BF16) for p in los + his], axis=1)
        g = jnp.dot(x, wgb_ref[...], preferred_element_type=F32)
        u = jnp.dot(x, wub_ref[...], preferred_element_type=F32)
        a = (g * _sigmoid(g) * u).astype(BF16)
        _store_token_tiles(jnp.dot(a, wdb_ref[...], preferred_element_type=F32), y_ref)


def _experts(xs, blk_e, n_real, w_gate, w_up, w_down, layer, rows, sub):
    d, ff = w_gate.shape[2:]
    nb = xs.shape[0] // (rows * sub)
    xrow = lambda b, be, nr: (jnp.minimum(b, jnp.maximum(nr[0] - 1, 0)), 0)
    wsel = lambda b, be, nr: (layer, be[b], 0, 0)
    return pl.pallas_call(
        functools.partial(_expert_kernel, rows=rows, sub=sub),
        grid_spec=pltpu.PrefetchScalarGridSpec(
            num_scalar_prefetch=2, grid=(nb,),
            in_specs=[pl.BlockSpec((rows * sub, LANES), xrow), pl.BlockSpec((None, None, d, ff), wsel),
                      pl.BlockSpec((None, None, d, ff), wsel), pl.BlockSpec((None, None, ff, d), wsel)],
            out_specs=pl.BlockSpec((rows * sub, LANES), xrow),
            scratch_shapes=[pltpu.VMEM((d, ff), BF16), pltpu.VMEM((d, ff), BF16), pltpu.VMEM((ff, d), BF16)]),
        out_shape=jax.ShapeDtypeStruct(xs.shape, jnp.uint32),
        compiler_params=_params("arbitrary"),
        name="experts",
    )(blk_e, n_real, xs, w_gate, w_up, w_down)


def _combine_kernel(dest_ref, dnext_ref, ys_ref, wt_ref, x1_ref, h_ref, sg_ref, su_ref, sd_ref, g2_ref, o_ref,
                    gbuf_ref, sems, *, tt, sub):
    i = pl.program_id(0)
    slot_rows = TOP_K * tt * sub
    t_chunk = tt // TOP_K

    def gather(dref, col, slot, t, j):
        src = ys_ref.at[pl.ds(pl.multiple_of(dref[j, col + t], sub), sub)]
        dst = gbuf_ref.at[pl.ds(slot * slot_rows + (j * tt + t) * sub, sub)]
        return pltpu.make_async_copy(src, dst, sems.at[slot])

    def wait_tile(slot):
        def drain(t, c):
            for j in range(TOP_K):
                gather(dest_ref, 0, slot, 0, j).wait()
            return c
        lax.fori_loop(0, tt, drain, 0)

    @pl.when(i == 0)
    def _():
        def issue(t, c):
            for j in range(TOP_K):
                gather(dest_ref, 0, 0, t, j).start(priority=j % 2)
            return c
        lax.fori_loop(0, tt, issue, 0)

    def tile(slot, row0, dref, col):
        rows = slice(row0, row0 + tt)
        h = h_ref[rows, :]
        g = jnp.dot(h, sg_ref[...], preferred_element_type=F32)
        u = jnp.dot(h, su_ref[...], preferred_element_type=F32)
        shared = jnp.dot((g * _sigmoid(g) * u).astype(BF16), sd_ref[...], preferred_element_type=F32)
        wt = wt_ref[rows, :]
        acc = [jnp.zeros((tt, LANES), F32) for _ in range(2 * sub)]
        for j in range(TOP_K):
            for t in range(j * t_chunk, (j + 1) * t_chunk):
                for jj in range(TOP_K):
                    gather(dref, col, 1 - slot, t, jj).start(priority=jj % 2)
            wj = jnp.broadcast_to(wt[:, j:j + 1], (tt, LANES))
            los, his = _load_token_tiles(gbuf_ref, slot * slot_rows + j * tt * sub, tt, sub)
            acc = [a + p * wj for a, p in zip(acc, los + his)]
        routed = jnp.concatenate(acc, axis=1)
        o_ref[rows, :] = x1_ref[rows, :] + g2_ref[...] * (routed + shared)

    wait_tile(0)
    tile(0, 0, dest_ref, tt)
    wait_tile(1)
    tile(1, tt, dnext_ref, 0)

    @pl.when(i == pl.num_programs(0) - 1)
    def _():
        wait_tile(0)


def _combine(ys, dest, wts_t, x1, h2, s_gate, s_up, s_down, g2, seq, sub):
    t, d = x1.shape
    sf = s_gate.shape[1]
    tt = _tile(seq // 2, 128)
    assert tt % TOP_K == 0
    n_steps = t // (2 * tt)
    row = lambda i: (i, 0)
    return pl.pallas_call(
        functools.partial(_combine_kernel, tt=tt, sub=sub),
        grid=(n_steps,),
        in_specs=[
            pl.BlockSpec((TOP_K, 2 * tt), lambda i: (0, i), memory_space=pltpu.SMEM),
            pl.BlockSpec((TOP_K, tt), lambda i: (0, jnp.minimum(2 * i + 2, 2 * n_steps - 2)),
                         memory_space=pltpu.SMEM),
            pl.BlockSpec(memory_space=pl.ANY),
            pl.BlockSpec((2 * tt, TOP_K), row), pl.BlockSpec((2 * tt, d), row), pl.BlockSpec((2 * tt, d), row),
            _const_spec((d, sf)), _const_spec((d, sf)), _const_spec((sf, d)),
            pl.BlockSpec((None, 1, d), lambda i: ((i * 2 * tt) // seq, 0, 0)),
        ],
        out_specs=pl.BlockSpec((2 * tt, d), row),
        out_shape=jax.ShapeDtypeStruct((t, d), F32),
        scratch_shapes=[pltpu.VMEM((2 * TOP_K * tt * sub, LANES), jnp.uint32), pltpu.SemaphoreType.DMA((2,))],
        compiler_params=_params("arbitrary"),
        name="combine",
    )(dest, dest, ys, wts_t, x1, h2, s_gate, s_up, s_down, g2)


def _moe(x1, h2, h2p, lgt, g2, router_bias, w_gate, w_up, w_down, layer, s_gate, s_up, s_down):
    b, s, d = x1.shape
    t = b * s
    e = lgt.shape[0]
    sub = _token_sublanes(d)
    rows = _tile(t, MOE_BLOCK_ROWS)
    eid, pos, wts, cnt = _route(lgt, router_bias)
    counts = cnt[:, 0].astype(jnp.int32)
    pcnt = (counts + rows - 1) // rows * rows
    pend = jnp.cumsum(pcnt)
    poff = pend - pcnt
    nb = (t * TOP_K + e * (rows - 1)) // rows
    n_real = (pend[-1] // rows).astype(jnp.int32).reshape(1)
    starts = jnp.arange(nb, dtype=jnp.int32) * rows
    blk_e = jnp.minimum(jnp.sum((pend[None, :] <= starts[:, None]).astype(jnp.int32), axis=1), e - 1)
    dest = _dest_rows(poff.astype(jnp.int32), eid, pos, sub)
    xs = _dispatch(h2p, dest, pend.astype(jnp.int32), pcnt.astype(jnp.int32), nb * rows, rows, sub)
    ys = _experts(xs, blk_e, n_real, w_gate, w_up, w_down, layer, rows, sub)
    out = _combine(ys, dest, wts.T, x1.reshape(t, d), h2.reshape(t, d), s_gate.astype(BF16), s_up.astype(BF16),
                   s_down.astype(BF16), g2, s, sub)
    return out.reshape(b, s, d)


def _rope_tables(n_tok):
    axis_dim = HEAD_DIM // 2
    rows = n_tok // GRID_W
    r, col = jnp.meshgrid(jnp.arange(rows), jnp.arange(GRID_W), indexing="ij")
    pos = jnp.stack([r.reshape(-1), col.reshape(-1)], axis=-1).astype(F32)
    inv = ROPE_THETA ** (-jnp.arange(0, axis_dim, 2, dtype=F32) / axis_dim)
    ang = pos[:, :, None] * inv
    ang = jnp.broadcast_to(ang[:, :, None, :], (n_tok, 2, 2, axis_dim // 2)).reshape(n_tok, HEAD_DIM)
    sign = jnp.where((jnp.arange(HEAD_DIM) % axis_dim) < axis_dim // 2, -1.0, 1.0).astype(F32)
    return jnp.cos(ang), jnp.sin(ang) * sign


def kernel(x, c, ctx, c_ctx, w_mod, b_mod, norm1_g, norm2_g, mix_w_in, q_norm_g, k_norm_g, conv_w, conv_b,
           conv_norm_g, conv_norm_b, mix_w_out, pool_w, pool_b, pool_scale, router_w, router_bias,
           moe_w_gate, moe_w_up, moe_w_down, shared_w_gate, shared_w_up, shared_w_down):
    b, s, d = x.shape
    depth = w_mod.shape[0]
    assert depth == 2, "layer schedule below is written for an attention layer followed by a pooling layer"
    kv_w = KV_HEADS * HEAD_DIM
    q_w = mix_w_in.shape[2] - 2 * kv_w - d

    pad = (-(b + 1)) % SUBLANES
    cc = jnp.concatenate([c, c_ctx[None, :], jnp.zeros((pad, d), F32)], axis=0)
    mods = _modulation(cc, w_mod, b_mod)

    def mod(layer, k, rows=slice(0, b)):
        return mods[layer, rows, k * d:(k + 1) * d][:, None, :]

    vec = lambda a: a.reshape(1, -1)
    rwt = jnp.swapaxes(router_w, 1, 2).astype(BF16)
    cos, sin_signed = _rope_tables(s)

    w_in = mix_w_in[0].astype(BF16)
    q, k, v, z = _inproj(x, mod(0, 0), mod(0, 1), vec(norm1_g[0]), w_in, vec(q_norm_g[0]), vec(k_norm_g[0]),
                         cos, sin_signed)
    ctx_row = slice(b, b + 1)
    kc, vc = _ctx_kv(ctx, mods[0, ctx_row, 0:d], mods[0, ctx_row, d:2 * d], vec(norm1_g[0]),
                     w_in[:, q_w:q_w + 2 * kv_w], vec(k_norm_g[0]))
    attn = _attention(q, k, v, kc, vc)
    cv = _conformer_conv(z, conv_w[0], vec(conv_b[0]), vec(conv_norm_g[0]), vec(conv_norm_b[0]))
    w_out = mix_w_out[0].astype(BF16)
    x1, h2, h2p, lgt = _outproj(attn, cv, x, w_out[:q_w], w_out[q_w:], mod(0, 2), vec(norm2_g[0]), mod(0, 3),
                                mod(0, 4), rwt[0])
    x = _moe(x1, h2, h2p, lgt, mod(0, 5), router_bias[0], moe_w_gate, moe_w_up, moe_w_down, 0,
             shared_w_gate[0], shared_w_up[0], shared_w_down[0])

    x1, h2, h2p, lgt = _pool_layer(x, vec(norm1_g[1]), mod(1, 0), mod(1, 1), pool_w[0].astype(BF16), vec(pool_b[0]),
                                   vec(pool_scale[0]), mod(1, 2), vec(norm2_g[1]), mod(1, 3), mod(1, 4), rwt[1])
    x = _moe(x1, h2, h2p, lgt, mod(1, 5), router_bias[1], moe_w_gate, moe_w_up, moe_w_down, 1,
             shared_w_gate[1], shared_w_up[1], shared_w_down[1])
    return x
```

```python
import functools

import jax
import jax.numpy as jnp
from jax import lax
from jax.experimental import pallas as pl
from jax.experimental.pallas import tpu as pltpu

HEAD_DIM = 128
KV_HEADS = 2
GRID_W = 64
ROPE_THETA = 10000.0
EPS = 1e-6
CONV_K = 31
CONV_PAD = CONV_K // 2
POOL_WINDOWS = (2, 4, 8, 16)
N_EXPERT_GROUPS = 8
TOPK_GROUPS = 4
TOP_K = 8
ROUTED_SCALE = 2.5
LOG2_E = 1.4426950408889634

LANES = 128
SUBLANES = 8
HALO = 16
MOE_BLOCK_ROWS = 512
VMEM_LIMIT_BYTES = 56 * 1024 * 1024

F32 = jnp.float32
BF16 = jnp.bfloat16


def _tile(n, pref):
    t = min(n, pref)
    while n % t:
        t //= 2
    return t


def _params(*sem):
    return pltpu.CompilerParams(dimension_semantics=sem, vmem_limit_bytes=VMEM_LIMIT_BYTES)


def _const_spec(shape):
    nd = len(shape)
    return pl.BlockSpec(shape, lambda *_: (0,) * nd, pipeline_mode=pl.Buffered(1))


def _sigmoid(x):
    return 1.0 / (1.0 + jnp.exp(-x))


def _adaln(x, g, shift, scale):
    ms = jnp.mean(x * x, axis=-1, keepdims=True)
    return (x * lax.rsqrt(ms + EPS) * g) * (1.0 + scale) + shift


def _mod_kernel(c_ref, w_ref, b_ref, o_ref):
    c = c_ref[...]
    a = c * _sigmoid(c)
    o_ref[...] = jnp.dot(a.astype(BF16), w_ref[...].astype(BF16), preferred_element_type=F32) + b_ref[...]


def _modulation(cc, w_mod, b_mod):
    n_layers, d, n = w_mod.shape
    r = cc.shape[0]
    tn = _tile(n, 1024)
    return pl.pallas_call(
        _mod_kernel,
        grid=(n_layers, n // tn),
        in_specs=[
            pl.BlockSpec((r, d), lambda l, j: (0, 0)),
            pl.BlockSpec((None, d, tn), lambda l, j: (l, 0, j)),
            pl.BlockSpec((None, 1, tn), lambda l, j: (l, 0, j)),
        ],
        out_specs=pl.BlockSpec((None, r, tn), lambda l, j: (l, 0, j)),
        out_shape=jax.ShapeDtypeStruct((n_layers, r, n), F32),
        compiler_params=_params("arbitrary", "arbitrary"),
        name="modulation",
    )(cc, w_mod, b_mod.reshape(n_layers, 1, n))


def _head_norm(x, g):
    ms = jnp.mean(x * x, axis=-1, keepdims=True)
    return x * lax.rsqrt(ms + EPS) * g


def _rope(x, cos, sin_signed, first_half):
    up = pltpu.roll(x, HEAD_DIM - HEAD_DIM // 4, 1)
    dn = pltpu.roll(x, HEAD_DIM // 4, 1)
    return x * cos + jnp.where(first_half, up, dn) * sin_signed


def _inproj_kernel(x_ref, sh_ref, sc_ref, g_ref, w_ref, qg_ref, kg_ref, cos_ref, sin_ref,
                   q_ref, k_ref, v_ref, z_ref, *, n_heads, conv_c):
    h = _adaln(x_ref[...], g_ref[...], sh_ref[...], sc_ref[...]).astype(BF16)
    acc = jnp.dot(h, w_ref[...], preferred_element_type=F32)
    cos = cos_ref[...]
    sin = sin_ref[...]
    lane = lax.broadcasted_iota(jnp.int32, cos.shape, 1)
    first_half = (lane % (HEAD_DIM // 2)) < (HEAD_DIM // 4)
    q_w = n_heads * HEAD_DIM
    kv_w = KV_HEADS * HEAD_DIM
    q_scale = HEAD_DIM ** -0.5 * LOG2_E
    for hd in range(n_heads):
        sl = slice(hd * HEAD_DIM, (hd + 1) * HEAD_DIM)
        qn = _head_norm(acc[:, sl], qg_ref[...])
        q_ref[:, sl] = (_rope(qn, cos, sin, first_half) * q_scale).astype(q_ref.dtype)
    for hd in range(KV_HEADS):
        sl = slice(hd * HEAD_DIM, (hd + 1) * HEAD_DIM)
        kn = _head_norm(acc[:, q_w + hd * HEAD_DIM:q_w + (hd + 1) * HEAD_DIM], kg_ref[...])
        k_ref[:, sl] = _rope(kn, cos, sin, first_half).astype(k_ref.dtype)
    v_ref[...] = acc[:, q_w + kv_w:q_w + 2 * kv_w].astype(v_ref.dtype)
    u0 = q_w + 2 * kv_w
    a = acc[:, u0:u0 + conv_c]
    gate = acc[:, u0 + conv_c:u0 + 2 * conv_c]
    z_ref[...] = a * _sigmoid(gate)


def _inproj(x, sh, sc, g, w_in, q_g, k_g, cos, sin_signed):
    b, s, d = x.shape
    in_w = w_in.shape[1]
    kv_w = KV_HEADS * HEAD_DIM
    conv_c = d // 2
    q_w = in_w - 2 * kv_w - 2 * conv_c
    n_heads = q_w // HEAD_DIM
    tm = _tile(s, 512)
    row = lambda bi, i: (bi, i, 0)
    vec = pl.BlockSpec((None, 1, d), lambda bi, i: (bi, 0, 0))
    tab = pl.BlockSpec((tm, HEAD_DIM), lambda bi, i: (i, 0))
    return pl.pallas_call(
        functools.partial(_inproj_kernel, n_heads=n_heads, conv_c=conv_c),
        grid=(b, s // tm),
        in_specs=[
            pl.BlockSpec((None, tm, d), row), vec, vec, _const_spec((1, d)), _const_spec((d, in_w)),
            _const_spec((1, HEAD_DIM)), _const_spec((1, HEAD_DIM)), tab, tab,
        ],
        out_specs=[
            pl.BlockSpec((None, tm, q_w), row), pl.BlockSpec((None, tm, kv_w), row),
            pl.BlockSpec((None, tm, kv_w), row), pl.BlockSpec((None, tm, conv_c), row),
        ],
        out_shape=[
            jax.ShapeDtypeStruct((b, s, q_w), BF16), jax.ShapeDtypeStruct((b, s, kv_w), BF16),
            jax.ShapeDtypeStruct((b, s, kv_w), BF16), jax.ShapeDtypeStruct((b, s, conv_c), F32),
        ],
        compiler_params=_params("arbitrary", "arbitrary"),
        name="inproj",
    )(x, sh, sc, g, w_in, q_g, k_g, cos, sin_signed)


def _ctx_kv_kernel(x_ref, sh_ref, sc_ref, g_ref, w_ref, kg_ref, k_ref, v_ref):
    h = _adaln(x_ref[...], g_ref[...], sh_ref[...], sc_ref[...]).astype(BF16)
    acc = jnp.dot(h, w_ref[...], preferred_element_type=F32)
    kv_w = KV_HEADS * HEAD_DIM
    for hd in range(KV_HEADS):
        sl = slice(hd * HEAD_DIM, (hd + 1) * HEAD_DIM)
        k_ref[:, sl] = _head_norm(acc[:, sl], kg_ref[...]).astype(k_ref.dtype)
    v_ref[...] = acc[:, kv_w:2 * kv_w].astype(v_ref.dtype)


def _ctx_kv(ctx, sh, sc, g, w_kv, k_g):
    b, lc, d = ctx.shape
    kv_w = KV_HEADS * HEAD_DIM
    row = lambda bi: (bi, 0, 0)
    return pl.pallas_call(
        _ctx_kv_kernel,
        grid=(b,),
        in_specs=[
            pl.BlockSpec((None, lc, d), row), _const_spec((1, d)), _const_spec((1, d)), _const_spec((1, d)),
            _const_spec((d, 2 * kv_w)), _const_spec((1, HEAD_DIM)),
        ],
        out_specs=[pl.BlockSpec((None, lc, kv_w), row), pl.BlockSpec((None, lc, kv_w), row)],
        out_shape=[jax.ShapeDtypeStruct((b, lc, kv_w), BF16), jax.ShapeDtypeStruct((b, lc, kv_w), BF16)],
        compiler_params=_params("arbitrary"),
        name="ctx_kv",
    )(ctx, sh, sc, g, w_kv, k_g)


def _attn_kernel(q_ref, k_ref, v_ref, kc_ref, vc_ref, o_ref, *, group, tk):
    tq = q_ref.shape[0]
    rows = group * tq
    q = jnp.concatenate([q_ref[:, g * HEAD_DIM:(g + 1) * HEAD_DIM] for g in range(group)], axis=0)
    nt = (((1,), (1,)), ((), ()))
    chunks = [(k_ref, v_ref, c * tk, tk) for c in range(k_ref.shape[0] // tk)]
    chunks.append((kc_ref, vc_ref, 0, kc_ref.shape[0]))
    m = jnp.full((rows, 1), -jnp.inf, F32)
    l = jnp.zeros((rows, 1), F32)
    acc = jnp.zeros((rows, HEAD_DIM), F32)
    for kr, vr, start, size in chunks:
        s = lax.dot_general(q, kr[start:start + size, :], nt, preferred_element_type=F32)
        m_new = jnp.maximum(m, jnp.max(s, axis=-1, keepdims=True))
        alpha = jnp.exp2(m - m_new)
        p = jnp.exp2(s - m_new)
        l = alpha * l + jnp.sum(p, axis=-1, keepdims=True)
        acc = alpha * acc + jnp.dot(p.astype(BF16), vr[start:start + size, :], preferred_element_type=F32)
        m = m_new
    o = acc / l
    for g in range(group):
        o_ref[:, g * HEAD_DIM:(g + 1) * HEAD_DIM] = o[g * tq:(g + 1) * tq, :].astype(o_ref.dtype)


def _attention(q, k, v, kc, vc):
    b, s, q_w = q.shape
    lc = kc.shape[1]
    group = q_w // (KV_HEADS * HEAD_DIM)
    tq = _tile(s, 512)
    kv_spec = lambda n: pl.BlockSpec((None, n, HEAD_DIM), lambda bi, h, i: (bi, 0, h))
    q_spec = pl.BlockSpec((None, tq, group * HEAD_DIM), lambda bi, h, i: (bi, i, h))
    return pl.pallas_call(
        functools.partial(_attn_kernel, group=group, tk=_tile(s, 512)),
        grid=(b, KV_HEADS, s // tq),
        in_specs=[q_spec, kv_spec(s), kv_spec(s), kv_spec(lc), kv_spec(lc)],
        out_specs=q_spec,
        out_shape=jax.ShapeDtypeStruct((b, s, q_w), BF16),
        compiler_params=_params("arbitrary", "arbitrary", "arbitrary"),
        name="attention",
    )(q, k, v, kc, vc)


def _conv_kernel(zc_ref, zp_ref, zn_ref, w_ref, b_ref, g_ref, beta_ref, o_ref, win_ref, conv_ref,
                 *, ts, n_tiles, rb, cb):
    i = pl.program_id(1)
    c = zc_ref.shape[1]
    win_ref[HALO:HALO + ts, :] = zc_ref[...]
    win_ref[0:HALO, :] = jnp.where(i > 0, zp_ref[...], 0.0)
    win_ref[HALO + ts:2 * HALO + ts, :] = jnp.where(i < n_tiles - 1, zn_ref[...], 0.0)
    base = HALO - CONV_PAD
    n_a = -(-CONV_K // SUBLANES)
    qn = rb + SUBLANES
    for r in range(ts // rb):
        for cc in range(c // cb):
            cs = slice(cc * cb, (cc + 1) * cb)
            rows = win_ref[r * rb:r * rb + qn + SUBLANES * (n_a - 1), cs]
            acc = jnp.zeros((rb, cb), F32)
            for rr in range(SUBLANES):
                q = None
                for a in range(n_a):
                    kk = SUBLANES * a + rr
                    if kk < CONV_K:
                        term = rows[SUBLANES * a:SUBLANES * a + qn, :] * w_ref[kk:kk + 1, cs]
                        q = term if q is None else q + term
                acc = acc + pltpu.roll(q, (qn - base - rr) % qn, 0)[0:rb, :]
            conv_ref[r * rb:(r + 1) * rb, cs] = acc + b_ref[:, cs]
    z = conv_ref[...]
    mu = jnp.mean(z, axis=-1, keepdims=True)
    zc = z - mu
    var = jnp.mean(zc * zc, axis=-1, keepdims=True)
    y = zc * lax.rsqrt(var + EPS) * g_ref[...] + beta_ref[...]
    o_ref[...] = (y * _sigmoid(y)).astype(o_ref.dtype)


def _conformer_conv(z, conv_w, conv_b, cn_g, cn_b):
    b, s, c = z.shape
    ts = _tile(s, 128)
    n_tiles = s // ts
    hb = ts // HALO
    n_hb = s // HALO
    cur = lambda bi, i: (bi, i, 0)
    prev = lambda bi, i: (bi, jnp.maximum(i * hb - 1, 0), 0)
    nxt = lambda bi, i: (bi, jnp.minimum((i + 1) * hb, n_hb - 1), 0)
    return pl.pallas_call(
        functools.partial(_conv_kernel, ts=ts, n_tiles=n_tiles, rb=_tile(ts, 32), cb=_tile(c, LANES)),
        grid=(b, n_tiles),
        in_specs=[
            pl.BlockSpec((None, ts, c), cur), pl.BlockSpec((None, HALO, c), prev), pl.BlockSpec((None, HALO, c), nxt),
            _const_spec((CONV_K, c)), _const_spec((1, c)), _const_spec((1, c)), _const_spec((1, c)),
        ],
        out_specs=pl.BlockSpec((None, ts, c), cur),
        out_shape=jax.ShapeDtypeStruct((b, s, c), BF16),
        scratch_shapes=[pltpu.VMEM((ts + 2 * HALO, c), F32), pltpu.VMEM((ts, c), F32)],
        compiler_params=_params("arbitrary", "arbitrary"),
        name="conformer_conv",
    )(z, z, z, conv_w, conv_b, cn_g, cn_b)


def _token_sublanes(d):
    assert d % (2 * LANES) == 0
    return d // (2 * LANES)


def _pack_pair(lo, hi):
    lo_bits = lax.bitcast_convert_type(lo.astype(BF16).astype(F32), jnp.uint32) >> 16
    hi_bits = lax.bitcast_convert_type(hi.astype(BF16).astype(F32), jnp.uint32) & jnp.uint32(0xFFFF0000)
    return lo_bits | hi_bits


def _unpack_pair(w):
    lo = lax.bitcast_convert_type(w << 16, F32)
    hi = lax.bitcast_convert_type(w & jnp.uint32(0xFFFF0000), F32)
    return lo, hi


def _store_token_tiles(v, ref, start=0):
    m, d = v.shape
    sub = _token_sublanes(d)
    for s in range(sub):
        lo = v[:, s * LANES:(s + 1) * LANES]
        hi = v[:, d // 2 + s * LANES:d // 2 + (s + 1) * LANES]
        ref[pl.ds(start + s, m, stride=sub), :] = _pack_pair(lo, hi)


def _load_token_tiles(ref, start, m, sub):
    los, his = [], []
    for s in range(sub):
        lo, hi = _unpack_pair(ref[pl.ds(start + s, m, stride=sub), :])
        los.append(lo)
        his.append(hi)
    return los, his


def _finish_tile(xnew, g2n_ref, sh2_ref, sc2_ref, rwt_ref, x1_ref, h2_ref, h2p_ref, lgt_ref):
    x1_ref[...] = xnew
    h2 = _adaln(xnew, g2n_ref[...], sh2_ref[...], sc2_ref[...])
    h2b = h2.astype(BF16)
    h2_ref[...] = h2b
    _store_token_tiles(h2, h2p_ref)
    lgt_ref[...] = lax.dot_general(rwt_ref[...], h2b, (((1,), (1,)), ((), ())), preferred_element_type=F32)


def _outproj_kernel(a_ref, c_ref, x_ref, wa_ref, wc_ref, g1_ref, g2n_ref, sh2_ref, sc2_ref, rwt_ref,
                    x1_ref, h2_ref, h2p_ref, lgt_ref):
    y = jnp.dot(a_ref[...], wa_ref[...], preferred_element_type=F32)
    y = y + jnp.dot(c_ref[...], wc_ref[...], preferred_element_type=F32)
    _finish_tile(x_ref[...] + g1_ref[...] * y, g2n_ref, sh2_ref, sc2_ref, rwt_ref, x1_ref, h2_ref, h2p_ref, lgt_ref)


def _finish_specs(b, s, d, e, tm):
    nt = s // tm
    sub = _token_sublanes(d)
    row = lambda bi, i: (bi, i, 0)
    flat = lambda bi, i: (bi * nt + i, 0)
    out_specs = [pl.BlockSpec((None, tm, d), row), pl.BlockSpec((None, tm, d), row),
                 pl.BlockSpec((tm * sub, LANES), flat), pl.BlockSpec((e, tm), lambda bi, i: (0, bi * nt + i))]
    out_shape = [jax.ShapeDtypeStruct((b, s, d), F32), jax.ShapeDtypeStruct((b, s, d), BF16),
                 jax.ShapeDtypeStruct((b * s * sub, LANES), jnp.uint32), jax.ShapeDtypeStruct((e, b * s), F32)]
    return out_specs, out_shape


def _outproj(attn, cv, x, w_a, w_c, g1, g2n, sh2, sc2, rwt):
    b, s, d = x.shape
    e = rwt.shape[0]
    tm = _tile(s, 512)
    row = lambda bi, i: (bi, i, 0)
    vec = pl.BlockSpec((None, 1, d), lambda bi, i: (bi, 0, 0))
    out_specs, out_shape = _finish_specs(b, s, d, e, tm)
    return pl.pallas_call(
        _outproj_kernel,
        grid=(b, s // tm),
        in_specs=[
            pl.BlockSpec((None, tm, attn.shape[2]), row), pl.BlockSpec((None, tm, cv.shape[2]), row),
            pl.BlockSpec((None, tm, d), row), _const_spec(w_a.shape), _const_spec(w_c.shape),
            vec, _const_spec((1, d)), vec, vec, _const_spec((e, d)),
        ],
        out_specs=out_specs,
        out_shape=out_shape,
        compiler_params=_params("arbitrary", "arbitrary"),
        name="outproj",
    )(attn, cv, x, w_a, w_c, g1, g2n, sh2, sc2, rwt)


def _pool_kernel(xc_ref, xp_ref, xn_ref, g1n_ref, sh1_ref, sc1_ref, pw_ref, pb_ref, ps_ref, g1_ref,
                 g2n_ref, sh2_ref, sc2_ref, rwt_ref, x1_ref, h2_ref, h2p_ref, lgt_ref, win_ref, *, tm, n_tiles, seq):
    i = pl.program_id(1)
    d = xc_ref.shape[1]
    gc = d // len(POOL_WINDOWS)
    norm = lambda x: _adaln(x, g1n_ref[...], sh1_ref[...], sc1_ref[...])
    win_ref[HALO:HALO + tm, :] = norm(xc_ref[...])
    win_ref[0:HALO, :] = jnp.where(i > 0, norm(xp_ref[...]), 0.0)
    win_ref[HALO + tm:2 * HALO + tm, :] = jnp.where(i < n_tiles - 1, norm(xn_ref[...]), 0.0)
    t = i * tm + lax.broadcasted_iota(jnp.int32, (tm, 1), 0)
    n = tm + 2 * SUBLANES
    ahead = lambda a, k: pltpu.roll(a, n - k, 0)
    ys = []
    for gi, w in enumerate(POOL_WINDOWS):
        assert w in (2, 4, 8, 16)
        cs = slice(gi * gc, (gi + 1) * gc)
        run = win_ref[HALO - SUBLANES:HALO + tm + SUBLANES, cs]
        span = 1
        while 2 * span < w:
            run = run + ahead(run, span)
            span *= 2
        first = run[0:tm, :] if span == SUBLANES else ahead(run, SUBLANES - span)[0:tm, :]
        tot = first + run[SUBLANES:SUBLANES + tm, :]
        lo = jnp.clip(t - w // 2, 0, seq)
        hi = jnp.clip(t + w - w // 2, 0, seq)
        p = tot / (hi - lo).astype(F32) - win_ref[HALO:HALO + tm, cs]
        ys.append(jnp.dot(p.astype(BF16), pw_ref[gi], preferred_element_type=F32))
    y = (jnp.concatenate(ys, axis=-1) + pb_ref[...]) * ps_ref[...]
    _finish_tile(xc_ref[...] + g1_ref[...] * y, g2n_ref, sh2_ref, sc2_ref, rwt_ref, x1_ref, h2_ref, h2p_ref, lgt_ref)


def _pool_layer(x, g1n, sh1, sc1, pool_w, pool_b, pool_scale, g1, g2n, sh2, sc2, rwt):
    b, s, d = x.shape
    e = rwt.shape[0]
    tm = _tile(s, 256)
    n_tiles = s // tm
    hb = tm // HALO
    n_hb = s // HALO
    cur = lambda bi, i: (bi, i, 0)
    prev = lambda bi, i: (bi, jnp.maximum(i * hb - 1, 0), 0)
    nxt = lambda bi, i: (bi, jnp.minimum((i + 1) * hb, n_hb - 1), 0)
    vec = pl.BlockSpec((None, 1, d), lambda bi, i: (bi, 0, 0))
    cvec = _const_spec((1, d))
    out_specs, out_shape = _finish_specs(b, s, d, e, tm)
    return pl.pallas_call(
        functools.partial(_pool_kernel, tm=tm, n_tiles=n_tiles, seq=s),
        grid=(b, n_tiles),
        in_specs=[
            pl.BlockSpec((None, tm, d), cur), pl.BlockSpec((None, HALO, d), prev), pl.BlockSpec((None, HALO, d), nxt),
            cvec, vec, vec, _const_spec(pool_w.shape), cvec, cvec, vec, cvec, vec, vec, _const_spec((e, d)),
        ],
        out_specs=out_specs,
        out_shape=out_shape,
        scratch_shapes=[pltpu.VMEM((tm + 2 * HALO, d), F32)],
        compiler_params=_params("arbitrary", "arbitrary"),
        name="pool_mixer",
    )(x, x, x, g1n, sh1, sc1, pool_w, pool_b, pool_scale, g1, g2n, sh2, sc2, rwt)


def _route_kernel(lgt_ref, bias_ref, tri_ref, eid_ref, pos_ref, wts_ref, cnt_ref, carry_ref):
    i = pl.program_id(0)
    e, tn = lgt_ref.shape
    eg = e // N_EXPERT_GROUPS

    @pl.when(i == 0)
    def _():
        carry_ref[...] = jnp.zeros_like(carry_ref)

    scores = _sigmoid(lgt_ref[...])
    sel = scores + bias_ref[...]
    neg = jnp.float32(-jnp.inf)
    sub = lax.broadcasted_iota(jnp.int32, (eg, tn), 0)
    group_scores = []
    for g in range(N_EXPERT_GROUPS):
        blk = sel[g * eg:(g + 1) * eg, :]
        m1 = jnp.max(blk, axis=0, keepdims=True)
        first = jnp.min(jnp.where(blk == m1, sub, eg), axis=0, keepdims=True)
        m2 = jnp.max(jnp.where(sub == first, neg, blk), axis=0, keepdims=True)
        group_scores.append(m1 + m2)
    masked = []
    for g in range(N_EXPERT_GROUPS):
        rank = jnp.zeros((1, tn), jnp.int32)
        for g2 in range(N_EXPERT_GROUPS):
            if g2 == g:
                continue
            ahead = (group_scores[g2] >= group_scores[g]) if g2 < g else (group_scores[g2] > group_scores[g])
            rank = rank + ahead.astype(jnp.int32)
        masked.append(jnp.where(rank < TOPK_GROUPS, sel[g * eg:(g + 1) * eg, :], neg))
    selm = jnp.concatenate(masked, axis=0)

    eidx = lax.broadcasted_iota(jnp.int32, (e, tn), 0)
    work = selm
    picks = []
    for _ in range(TOP_K):
        mx = jnp.max(work, axis=0, keepdims=True)
        pick = jnp.min(jnp.where(work == mx, eidx, e), axis=0, keepdims=True)
        picks.append(pick)
        work = jnp.where(eidx == pick, neg, work)
    chosen = functools.reduce(jnp.logical_or, [eidx == p for p in picks])

    m = chosen.astype(BF16)
    pos = jnp.dot(m, tri_ref[...], preferred_element_type=F32) + carry_ref[...]
    carry_ref[...] = carry_ref[...] + jnp.sum(chosen.astype(F32), axis=1, keepdims=True)
    cnt_ref[...] = carry_ref[...]

    wsum = jnp.sum(jnp.where(chosen, scores, 0.0), axis=0, keepdims=True)
    for j, pick in enumerate(picks):
        hit = eidx == pick
        eid_ref[j:j + 1, :] = pick
        pos_ref[j:j + 1, :] = jnp.sum(jnp.where(hit, pos, 0.0), axis=0, keepdims=True).astype(jnp.int32)
        wj = jnp.sum(jnp.where(hit, scores, 0.0), axis=0, keepdims=True)
        wts_ref[j:j + 1, :] = wj / wsum * ROUTED_SCALE


def _route(lgt, router_bias):
    e, t = lgt.shape
    tn = _tile(t, 512)
    tri = (lax.broadcasted_iota(jnp.int32, (tn, tn), 0) < lax.broadcasted_iota(jnp.int32, (tn, tn), 1)).astype(BF16)
    col = lambda i: (0, i)
    return pl.pallas_call(
        _route_kernel,
        grid=(t // tn,),
        in_specs=[pl.BlockSpec((e, tn), col), _const_spec((e, 1)), _const_spec((tn, tn))],
        out_specs=[pl.BlockSpec((TOP_K, tn), col), pl.BlockSpec((TOP_K, tn), col), pl.BlockSpec((TOP_K, tn), col),
                   pl.BlockSpec((e, 1), lambda i: (0, 0))],
        out_shape=[jax.ShapeDtypeStruct((TOP_K, t), jnp.int32), jax.ShapeDtypeStruct((TOP_K, t), jnp.int32),
                   jax.ShapeDtypeStruct((TOP_K, t), F32), jax.ShapeDtypeStruct((e, 1), F32)],
        scratch_shapes=[pltpu.VMEM((e, 1), F32)],
        compiler_params=_params("arbitrary"),
        name="route",
    )(lgt, router_bias.reshape(e, 1), tri)


def _dest_kernel(poff_ref, eid_ref, pos_ref, dest_ref, *, sub):
    eid = eid_ref[...]
    acc = pos_ref[...]
    for e in range(poff_ref.shape[0]):
        acc = acc + jnp.where(eid == e, poff_ref[e], 0)
    dest_ref[...] = acc * sub


def _dest_rows(poff, eid, pos, sub):
    k, t = eid.shape
    tn = _tile(t, 2048)
    col = lambda i, *_: (0, i)
    return pl.pallas_call(
        functools.partial(_dest_kernel, sub=sub),
        grid_spec=pltpu.PrefetchScalarGridSpec(
            num_scalar_prefetch=1, grid=(t // tn,),
            in_specs=[pl.BlockSpec((k, tn), col), pl.BlockSpec((k, tn), col)],
            out_specs=pl.BlockSpec((k, tn), col)),
        out_shape=jax.ShapeDtypeStruct((k, t), jnp.int32),
        compiler_params=_params("arbitrary"),
        name="dest_rows",
    )(poff, eid, pos)


def _dispatch_kernel(pend_ref, pcnt_ref, dest_ref, h_ref, xs_ref, zero_ref, sem, zsem, *, tt, rows, sub):
    i = pl.program_id(0)
    n_exp = pend_ref.shape[0]

    def zero_copy(e):
        start = pl.multiple_of((pend_ref[e] - rows) * sub, rows * sub)
        return pltpu.make_async_copy(zero_ref, xs_ref.at[pl.ds(start, rows * sub)], zsem)

    @pl.when(i == 0)
    def _():
        zero_ref[...] = jnp.zeros_like(zero_ref)

        def start_zero(e, c):
            @pl.when(pcnt_ref[e] > 0)
            def _():
                zero_copy(e).start()
            return c

        def wait_zero(e, c):
            @pl.when(pcnt_ref[e] > 0)
            def _():
                zero_copy(e).wait()
            return c

        lax.fori_loop(0, n_exp, start_zero, 0)
        lax.fori_loop(0, n_exp, wait_zero, 0)

    def row_copy(t, j):
        src = h_ref.at[pl.ds(pl.multiple_of(t * sub, sub), sub)]
        dst = xs_ref.at[pl.ds(pl.multiple_of(dest_ref[j, t], sub), sub)]
        return pltpu.make_async_copy(src, dst, sem)

    def issue(t, c):
        for j in range(TOP_K):
            row_copy(t, j).start(priority=j % 2)
        return c

    def drain(t, c):
        for j in range(TOP_K):
            row_copy(t, j).wait()
        return c

    lax.fori_loop(0, tt, issue, 0)
    lax.fori_loop(0, tt, drain, 0)


def _dispatch(h2p, dest, pend, pcnt, n_rows, rows, sub):
    t = h2p.shape[0] // sub
    tt = _tile(t, 256)
    return pl.pallas_call(
        functools.partial(_dispatch_kernel, tt=tt, rows=rows, sub=sub),
        grid_spec=pltpu.PrefetchScalarGridSpec(
            num_scalar_prefetch=2, grid=(t // tt,),
            in_specs=[pl.BlockSpec((TOP_K, tt), lambda i, *_: (0, i), memory_space=pltpu.SMEM),
                      pl.BlockSpec((tt * sub, LANES), lambda i, *_: (i, 0))],
            out_specs=pl.BlockSpec(memory_space=pl.ANY),
            scratch_shapes=[pltpu.VMEM((rows * sub, LANES), jnp.uint32), pltpu.SemaphoreType.DMA,
                            pltpu.SemaphoreType.DMA]),
        out_shape=jax.ShapeDtypeStruct((n_rows * sub, LANES), jnp.uint32),
        compiler_params=_params("arbitrary"),
        name="dispatch",
    )(pend, pcnt, dest, h2p)


def _expert_kernel(blk_e_ref, nreal_ref, first_ref, slot_ref, next_e_ref, has_next_ref, x_ref, wg_hbm, wu_hbm, wd_hbm,
                   y_ref, wg32_ref, wu32_ref, wd32_ref, wgb_ref, wub_ref, wdb_ref, wsem, *, rows, sub, layer):
    b = pl.program_id(0)

    def weight_copies(e, slot):
        return [pltpu.make_async_copy(src.at[layer, e], dst.at[slot], wsem.at[slot])
                for src, dst in ((wg_hbm, wg32_ref), (wu_hbm, wu32_ref), (wd_hbm, wd32_ref))]

    @pl.when(first_ref[b] == 1)
    def _():
        e = blk_e_ref[b]
        slot = slot_ref[b]

        @pl.when(b == 0)
        def _():
            for cp in weight_copies(e, slot):
                cp.start()

        @pl.when(has_next_ref[b] == 1)
        def _():
            for cp in weight_copies(next_e_ref[b], 1 - slot):
                cp.start()

        for cp in weight_copies(e, slot):
            cp.wait()
        wgb_ref[...] = wg32_ref[slot].astype(BF16)
        wub_ref[...] = wu32_ref[slot].astype(BF16)
        wdb_ref[...] = wd32_ref[slot].astype(BF16)

    @pl.when(b < nreal_ref[0])
    def _():
        los, his = _load_token_tiles(x_ref, 0, rows, sub)
        x = jnp.concatenate([p.astype(BF16) for p in los + his], axis=1)
        g = jnp.dot(x, wgb_ref[...], preferred_element_type=F32)
        u = jnp.dot(x, wub_ref[...], preferred_element_type=F32)
        a = (g * _sigmoid(g) * u).astype(BF16)
        _store_token_tiles(jnp.dot(a, wdb_ref[...], preferred_element_type=F32), y_ref)


def _experts(xs, blk_e, n_real, pend, w_gate, w_up, w_down, layer, rows, sub):
    d, ff = w_gate.shape[2:]
    nb = xs.shape[0] // (rows * sub)
    blocks = jnp.arange(nb, dtype=jnp.int32)
    valid = blocks < n_real[0]
    prev_e = jnp.concatenate([jnp.full((1,), -1, jnp.int32), blk_e[:-1]])
    first = jnp.logical_and(valid, blk_e != prev_e)
    slot = (jnp.cumsum(first.astype(jnp.int32)) - 1) % 2
    next_blk = pend[blk_e] // rows
    has_next = jnp.logical_and(first, next_blk < n_real[0])
    next_e = blk_e[jnp.minimum(next_blk, nb - 1)]
    i32 = lambda a: a.astype(jnp.int32)

    xrow = lambda b, be, nr, *_: (jnp.minimum(b, jnp.maximum(nr[0] - 1, 0)), 0)
    hbm = pl.BlockSpec(memory_space=pl.ANY)
    return pl.pallas_call(
        functools.partial(_expert_kernel, rows=rows, sub=sub, layer=layer),
        grid_spec=pltpu.PrefetchScalarGridSpec(
            num_scalar_prefetch=6, grid=(nb,),
            in_specs=[pl.BlockSpec((rows * sub, LANES), xrow), hbm, hbm, hbm],
            out_specs=pl.BlockSpec((rows * sub, LANES), xrow),
            scratch_shapes=[pltpu.VMEM((2, d, ff), F32), pltpu.VMEM((2, d, ff), F32), pltpu.VMEM((2, ff, d), F32),
                            pltpu.VMEM((d, ff), BF16), pltpu.VMEM((d, ff), BF16), pltpu.VMEM((ff, d), BF16),
                            pltpu.SemaphoreType.DMA((2,))]),
        out_shape=jax.ShapeDtypeStruct(xs.shape, jnp.uint32),
        compiler_params=_params("arbitrary"),
        name="experts",
    )(blk_e, n_real, i32(first), i32(slot), i32(next_e), i32(has_next), xs, w_gate, w_up, w_down)


def _combine_kernel(dest_ref, dnext_ref, ys_ref, wt_ref, x1_ref, h_ref, sg_ref, su_ref, sd_ref, g2_ref, o_ref,
                    gbuf_ref, sems, *, tt, sub):
    i = pl.program_id(0)
    slot_rows = TOP_K * tt * sub
    t_chunk = tt // TOP_K

    def gather(dref, col, slot, t, j):
        src = ys_ref.at[pl.ds(pl.multiple_of(dref[j, col + t], sub), sub)]
        dst = gbuf_ref.at[pl.ds(slot * slot_rows + (j * tt + t) * sub, sub)]
        return pltpu.make_async_copy(src, dst, sems.at[slot])

    def wait_tile(slot):
        def drain(t, c):
            for j in range(TOP_K):
                gather(dest_ref, 0, slot, 0, j).wait()
            return c
        lax.fori_loop(0, tt, drain, 0)

    @pl.when(i == 0)
    def _():
        def issue(t, c):
            for j in range(TOP_K):
                gather(dest_ref, 0, 0, t, j).start(priority=j % 2)
            return c
        lax.fori_loop(0, tt, issue, 0)

    def tile(slot, row0, dref, col):
        rows = slice(row0, row0 + tt)
        h = h_ref[rows, :]
        g = jnp.dot(h, sg_ref[...], preferred_element_type=F32)
        u = jnp.dot(h, su_ref[...], preferred_element_type=F32)
        shared = jnp.dot((g * _sigmoid(g) * u).astype(BF16), sd_ref[...], preferred_element_type=F32)
        wt = wt_ref[rows, :]
        acc = [jnp.zeros((tt, LANES), F32) for _ in range(2 * sub)]
        for j in range(TOP_K):
            for t in range(j * t_chunk, (j + 1) * t_chunk):
                for jj in range(TOP_K):
                    gather(dref, col, 1 - slot, t, jj).start(priority=jj % 2)
            wj = jnp.broadcast_to(wt[:, j:j + 1], (tt, LANES))
            los, his = _load_token_tiles(gbuf_ref, slot * slot_rows + j * tt * sub, tt, sub)
            acc = [a + p * wj for a, p in zip(acc, los + his)]
        routed = jnp.concatenate(acc, axis=1)
        o_ref[rows, :] = x1_ref[rows, :] + g2_ref[...] * (routed + shared)

    wait_tile(0)
    tile(0, 0, dest_ref, tt)
    wait_tile(1)
    tile(1, tt, dnext_ref, 0)

    @pl.when(i == pl.num_programs(0) - 1)
    def _():
        wait_tile(0)


def _combine(ys, dest, wts_t, x1, h2, s_gate, s_up, s_down, g2, seq, sub):
    t, d = x1.shape
    sf = s_gate.shape[1]
    tt = _tile(seq // 2, 128)
    assert tt % TOP_K == 0
    n_steps = t // (2 * tt)
    row = lambda i: (i, 0)
    return pl.pallas_call(
        functools.partial(_combine_kernel, tt=tt, sub=sub),
        grid=(n_steps,),
        in_specs=[
            pl.BlockSpec((TOP_K, 2 * tt), lambda i: (0, i), memory_space=pltpu.SMEM),
            pl.BlockSpec((TOP_K, tt), lambda i: (0, jnp.minimum(2 * i + 2, 2 * n_steps - 2)),
                         memory_space=pltpu.SMEM),
            pl.BlockSpec(memory_space=pl.ANY),
            pl.BlockSpec((2 * tt, TOP_K), row), pl.BlockSpec((2 * tt, d), row), pl.BlockSpec((2 * tt, d), row),
            _const_spec((d, sf)), _const_spec((d, sf)), _const_spec((sf, d)),
            pl.BlockSpec((None, 1, d), lambda i: ((i * 2 * tt) // seq, 0, 0)),
        ],
        out_specs=pl.BlockSpec((2 * tt, d), row),
        out_shape=jax.ShapeDtypeStruct((t, d), F32),
        scratch_shapes=[pltpu.VMEM((2 * TOP_K * tt * sub, LANES), jnp.uint32), pltpu.SemaphoreType.DMA((2,))],
        compiler_params=_params("arbitrary"),
        name="combine",
    )(dest, dest, ys, wts_t, x1, h2, s_gate, s_up, s_down, g2)


def _moe(x1, h2, h2p, lgt, g2, router_bias, w_gate, w_up, w_down, layer, s_gate, s_up, s_down):
    b, s, d = x1.shape
    t = b * s
    e = lgt.shape[0]
    sub = _token_sublanes(d)
    rows = _tile(t, MOE_BLOCK_ROWS)
    eid, pos, wts, cnt = _route(lgt, router_bias)
    counts = cnt[:, 0].astype(jnp.int32)
    pcnt = (counts + rows - 1) // rows * rows
    pend = jnp.cumsum(pcnt)
    poff = pend - pcnt
    nb = (t * TOP_K + e * (rows - 1)) // rows
    n_real = (pend[-1] // rows).astype(jnp.int32).reshape(1)
    starts = jnp.arange(nb, dtype=jnp.int32) * rows
    blk_e = jnp.minimum(jnp.sum((pend[None, :] <= starts[:, None]).astype(jnp.int32), axis=1), e - 1)
    dest = _dest_rows(poff.astype(jnp.int32), eid, pos, sub)
    xs = _dispatch(h2p, dest, pend.astype(jnp.int32), pcnt.astype(jnp.int32), nb * rows, rows, sub)
    ys = _experts(xs, blk_e, n_real, pend.astype(jnp.int32), w_gate, w_up, w_down, layer, rows, sub)
    out = _combine(ys, dest, wts.T, x1.reshape(t, d), h2.reshape(t, d), s_gate.astype(BF16), s_up.astype(BF16),
                   s_down.astype(BF16), g2, s, sub)
    return out.reshape(b, s, d)


def _rope_tables(n_tok):
    axis_dim = HEAD_DIM // 2
    rows = n_tok // GRID_W
    r, col = jnp.meshgrid(jnp.arange(rows), jnp.arange(GRID_W), indexing="ij")
    pos = jnp.stack([r.reshape(-1), col.reshape(-1)], axis=-1).astype(F32)
    inv = ROPE_THETA ** (-jnp.arange(0, axis_dim, 2, dtype=F32) / axis_dim)
    ang = pos[:, :, None] * inv
    ang = jnp.broadcast_to(ang[:, :, None, :], (n_tok, 2, 2, axis_dim // 2)).reshape(n_tok, HEAD_DIM)
    sign = jnp.where((jnp.arange(HEAD_DIM) % axis_dim) < axis_dim // 2, -1.0, 1.0).astype(F32)
    return jnp.cos(ang), jnp.sin(ang) * sign


def kernel(x, c, ctx, c_ctx, w_mod, b_mod, norm1_g, norm2_g, mix_w_in, q_norm_g, k_norm_g, conv_w, conv_b,
           conv_norm_g, conv_norm_b, mix_w_out, pool_w, pool_b, pool_scale, router_w, router_bias,
           moe_w_gate, moe_w_up, moe_w_down, shared_w_gate, shared_w_up, shared_w_down):
    b, s, d = x.shape
    depth = w_mod.shape[0]
    assert depth == 2, "layer schedule below is written for an attention layer followed by a pooling layer"
    kv_w = KV_HEADS * HEAD_DIM
    q_w = mix_w_in.shape[2] - 2 * kv_w - d

    pad = (-(b + 1)) % SUBLANES
    cc = jnp.concatenate([c, c_ctx[None, :], jnp.zeros((pad, d), F32)], axis=0)
    mods = _modulation(cc, w_mod, b_mod)

    def mod(layer, k, rows=slice(0, b)):
        return mods[layer, rows, k * d:(k + 1) * d][:, None, :]

    vec = lambda a: a.reshape(1, -1)
    rwt = jnp.swapaxes(router_w, 1, 2).astype(BF16)
    cos, sin_signed = _rope_tables(s)

    w_in = mix_w_in[0].astype(BF16)
    q, k, v, z = _inproj(x, mod(0, 0), mod(0, 1), vec(norm1_g[0]), w_in, vec(q_norm_g[0]), vec(k_norm_g[0]),
                         cos, sin_signed)
    ctx_row = slice(b, b + 1)
    kc, vc = _ctx_kv(ctx, mods[0, ctx_row, 0:d], mods[0, ctx_row, d:2 * d], vec(norm1_g[0]),
                     w_in[:, q_w:q_w + 2 * kv_w], vec(k_norm_g[0]))
    attn = _attention(q, k, v, kc, vc)
    cv = _conformer_conv(z, conv_w[0], vec(conv_b[0]), vec(conv_norm_g[0]), vec(conv_norm_b[0]))
    w_out = mix_w_out[0].astype(BF16)
    x1, h2, h2p, lgt = _outproj(attn, cv, x, w_out[:q_w], w_out[q_w:], mod(0, 2), vec(norm2_g[0]), mod(0, 3),
                                mod(0, 4), rwt[0])
    x = _moe(x1, h2, h2p, lgt, mod(0, 5), router_bias[0], moe_w_gate, moe_w_up, moe_w_down, 0,
             shared_w_gate[0], shared_w_up[0], shared_w_down[0])

    x1, h2, h2p, lgt = _pool_layer(x, vec(norm1_g[1]), mod(1, 0), mod(1, 1), pool_w[0].astype(BF16), vec(pool_b[0]),
                                   vec(pool_scale[0]), mod(1, 2), vec(norm2_g[1]), mod(1, 3), mod(1, 4), rwt[1])
    x = _moe(x1, h2, h2p, lgt, mod(1, 5), router_bias[1], moe_w_gate, moe_w_up, moe_w_down, 1,
             shared_w_gate[1], shared_w_up[1], shared_w_down[1])
    return x
```

```python
import functools

import jax
import jax.numpy as jnp
from jax import lax
from jax.experimental import pallas as pl
from jax.experimental.pallas import tpu as pltpu

HEAD_DIM = 128
KV_HEADS = 2
GRID_W = 64
ROPE_THETA = 10000.0
EPS = 1e-6
CONV_K = 31
CONV_PAD = CONV_K // 2
POOL_WINDOWS = (2, 4, 8, 16)
N_EXPERT_GROUPS = 8
TOPK_GROUPS = 4
TOP_K = 8
ROUTED_SCALE = 2.5
LOG2_E = 1.4426950408889634

LANES = 128
SUBLANES = 8
HALO = 16
assert HALO % SUBLANES == 0 and 0 <= HALO - CONV_PAD <= 1
MOE_BLOCK_ROWS = 512
VMEM_LIMIT_BYTES = 56 * 1024 * 1024

F32 = jnp.float32
BF16 = jnp.bfloat16


def _tile(n, pref):
    t = min(n, pref)
    while n % t:
        t //= 2
    return t


def _params(*sem):
    return pltpu.CompilerParams(dimension_semantics=sem, vmem_limit_bytes=VMEM_LIMIT_BYTES)


def _const_spec(shape):
    nd = len(shape)
    return pl.BlockSpec(shape, lambda *_: (0,) * nd, pipeline_mode=pl.Buffered(1))


def _sigmoid(x):
    return 1.0 / (1.0 + jnp.exp(-x))


def _adaln(x, g, shift, scale):
    ms = jnp.mean(x * x, axis=-1, keepdims=True)
    return (x * lax.rsqrt(ms + EPS) * g) * (1.0 + scale) + shift


def _mod_kernel(c_ref, w_ref, b_ref, o_ref):
    c = c_ref[...]
    a = c * _sigmoid(c)
    o_ref[...] = jnp.dot(a.astype(BF16), w_ref[...].astype(BF16), preferred_element_type=F32) + b_ref[...]


def _modulation(cc, w_mod, b_mod):
    n_layers, d, n = w_mod.shape
    r = cc.shape[0]
    tn = _tile(n, 1024)
    return pl.pallas_call(
        _mod_kernel,
        grid=(n_layers, n // tn),
        in_specs=[
            pl.BlockSpec((r, d), lambda l, j: (0, 0)),
            pl.BlockSpec((None, d, tn), lambda l, j: (l, 0, j)),
            pl.BlockSpec((None, 1, tn), lambda l, j: (l, 0, j)),
        ],
        out_specs=pl.BlockSpec((None, r, tn), lambda l, j: (l, 0, j)),
        out_shape=jax.ShapeDtypeStruct((n_layers, r, n), F32),
        compiler_params=_params("arbitrary", "arbitrary"),
        name="modulation",
    )(cc, w_mod, b_mod.reshape(n_layers, 1, n))


def _head_norm(x, g):
    ms = jnp.mean(x * x, axis=-1, keepdims=True)
    return x * lax.rsqrt(ms + EPS) * g


def _rope(x, cos, sin_signed, first_half):
    up = pltpu.roll(x, HEAD_DIM - HEAD_DIM // 4, 1)
    dn = pltpu.roll(x, HEAD_DIM // 4, 1)
    return x * cos + jnp.where(first_half, up, dn) * sin_signed


def _inproj_kernel(x_ref, sh_ref, sc_ref, g_ref, w_ref, qg_ref, kg_ref, cos_ref, sin_ref,
                   q_ref, k_ref, v_ref, z_ref, *, n_heads, conv_c):
    h = _adaln(x_ref[...], g_ref[...], sh_ref[...], sc_ref[...]).astype(BF16)
    acc = jnp.dot(h, w_ref[...], preferred_element_type=F32)
    cos = cos_ref[...]
    sin = sin_ref[...]
    lane = lax.broadcasted_iota(jnp.int32, cos.shape, 1)
    first_half = (lane % (HEAD_DIM // 2)) < (HEAD_DIM // 4)
    q_w = n_heads * HEAD_DIM
    kv_w = KV_HEADS * HEAD_DIM
    q_scale = HEAD_DIM ** -0.5 * LOG2_E
    for hd in range(n_heads):
        sl = slice(hd * HEAD_DIM, (hd + 1) * HEAD_DIM)
        qn = _head_norm(acc[:, sl], qg_ref[...])
        q_ref[:, sl] = (_rope(qn, cos, sin, first_half) * q_scale).astype(q_ref.dtype)
    for hd in range(KV_HEADS):
        sl = slice(hd * HEAD_DIM, (hd + 1) * HEAD_DIM)
        kn = _head_norm(acc[:, q_w + hd * HEAD_DIM:q_w + (hd + 1) * HEAD_DIM], kg_ref[...])
        k_ref[:, sl] = _rope(kn, cos, sin, first_half).astype(k_ref.dtype)
    v_ref[...] = acc[:, q_w + kv_w:q_w + 2 * kv_w].astype(v_ref.dtype)
    u0 = q_w + 2 * kv_w
    a = acc[:, u0:u0 + conv_c]
    gate = acc[:, u0 + conv_c:u0 + 2 * conv_c]
    z_ref[...] = a * _sigmoid(gate)


def _inproj(x, sh, sc, g, w_in, q_g, k_g, cos, sin_signed):
    b, s, d = x.shape
    in_w = w_in.shape[1]
    kv_w = KV_HEADS * HEAD_DIM
    conv_c = d // 2
    q_w = in_w - 2 * kv_w - 2 * conv_c
    n_heads = q_w // HEAD_DIM
    tm = _tile(s, 512)
    row = lambda bi, i: (bi, i, 0)
    vec = pl.BlockSpec((None, 1, d), lambda bi, i: (bi, 0, 0))
    tab = pl.BlockSpec((tm, HEAD_DIM), lambda bi, i: (i, 0))
    return pl.pallas_call(
        functools.partial(_inproj_kernel, n_heads=n_heads, conv_c=conv_c),
        grid=(b, s // tm),
        in_specs=[
            pl.BlockSpec((None, tm, d), row), vec, vec, _const_spec((1, d)), _const_spec((d, in_w)),
            _const_spec((1, HEAD_DIM)), _const_spec((1, HEAD_DIM)), tab, tab,
        ],
        out_specs=[
            pl.BlockSpec((None, tm, q_w), row), pl.BlockSpec((None, tm, kv_w), row),
            pl.BlockSpec((None, tm, kv_w), row), pl.BlockSpec((None, tm, conv_c), row),
        ],
        out_shape=[
            jax.ShapeDtypeStruct((b, s, q_w), BF16), jax.ShapeDtypeStruct((b, s, kv_w), BF16),
            jax.ShapeDtypeStruct((b, s, kv_w), BF16), jax.ShapeDtypeStruct((b, s, conv_c), F32),
        ],
        compiler_params=_params("arbitrary", "arbitrary"),
        name="inproj",
    )(x, sh, sc, g, w_in, q_g, k_g, cos, sin_signed)


def _ctx_kv_kernel(x_ref, sh_ref, sc_ref, g_ref, w_ref, kg_ref, k_ref, v_ref):
    h = _adaln(x_ref[...], g_ref[...], sh_ref[...], sc_ref[...]).astype(BF16)
    acc = jnp.dot(h, w_ref[...], preferred_element_type=F32)
    kv_w = KV_HEADS * HEAD_DIM
    for hd in range(KV_HEADS):
        sl = slice(hd * HEAD_DIM, (hd + 1) * HEAD_DIM)
        k_ref[:, sl] = _head_norm(acc[:, sl], kg_ref[...]).astype(k_ref.dtype)
    v_ref[...] = acc[:, kv_w:2 * kv_w].astype(v_ref.dtype)


def _ctx_kv(ctx, sh, sc, g, w_kv, k_g):
    b, lc, d = ctx.shape
    kv_w = KV_HEADS * HEAD_DIM
    row = lambda bi: (bi, 0, 0)
    return pl.pallas_call(
        _ctx_kv_kernel,
        grid=(b,),
        in_specs=[
            pl.BlockSpec((None, lc, d), row), _const_spec((1, d)), _const_spec((1, d)), _const_spec((1, d)),
            _const_spec((d, 2 * kv_w)), _const_spec((1, HEAD_DIM)),
        ],
        out_specs=[pl.BlockSpec((None, lc, kv_w), row), pl.BlockSpec((None, lc, kv_w), row)],
        out_shape=[jax.ShapeDtypeStruct((b, lc, kv_w), BF16), jax.ShapeDtypeStruct((b, lc, kv_w), BF16)],
        compiler_params=_params("arbitrary"),
        name="ctx_kv",
    )(ctx, sh, sc, g, w_kv, k_g)


def _attn_kernel(q_ref, k_ref, v_ref, kc_ref, vc_ref, o_ref, *, group, tk):
    tq = q_ref.shape[0]
    rows = group * tq
    q = jnp.concatenate([q_ref[:, g * HEAD_DIM:(g + 1) * HEAD_DIM] for g in range(group)], axis=0)
    nt = (((1,), (1,)), ((), ()))
    chunks = [(k_ref, v_ref, c * tk, tk) for c in range(k_ref.shape[0] // tk)]
    chunks.append((kc_ref, vc_ref, 0, kc_ref.shape[0]))
    m = jnp.full((rows, 1), -jnp.inf, F32)
    l = jnp.zeros((rows, 1), F32)
    acc = jnp.zeros((rows, HEAD_DIM), F32)
    for kr, vr, start, size in chunks:
        s = lax.dot_general(q, kr[start:start + size, :], nt, preferred_element_type=F32)
        m_new = jnp.maximum(m, jnp.max(s, axis=-1, keepdims=True))
        alpha = jnp.exp2(m - m_new)
        p = jnp.exp2(s - m_new)
        l = alpha * l + jnp.sum(p, axis=-1, keepdims=True)
        acc = alpha * acc + jnp.dot(p.astype(BF16), vr[start:start + size, :], preferred_element_type=F32)
        m = m_new
    o = acc / l
    for g in range(group):
        o_ref[:, g * HEAD_DIM:(g + 1) * HEAD_DIM] = o[g * tq:(g + 1) * tq, :].astype(o_ref.dtype)


def _attention(q, k, v, kc, vc):
    b, s, q_w = q.shape
    lc = kc.shape[1]
    group = q_w // (KV_HEADS * HEAD_DIM)
    tq = _tile(s, 512)
    kv_spec = lambda n: pl.BlockSpec((None, n, HEAD_DIM), lambda bi, h, i: (bi, 0, h))
    q_spec = pl.BlockSpec((None, tq, group * HEAD_DIM), lambda bi, h, i: (bi, i, h))
    return pl.pallas_call(
        functools.partial(_attn_kernel, group=group, tk=_tile(s, 512)),
        grid=(b, KV_HEADS, s // tq),
        in_specs=[q_spec, kv_spec(s), kv_spec(s), kv_spec(lc), kv_spec(lc)],
        out_specs=q_spec,
        out_shape=jax.ShapeDtypeStruct((b, s, q_w), BF16),
        compiler_params=_params("arbitrary", "arbitrary", "arbitrary"),
        name="attention",
    )(q, k, v, kc, vc)


def _conv_kernel(zc_ref, zp_ref, zn_ref, w_ref, b_ref, g_ref, beta_ref, o_ref, win_ref, conv_ref,
                 *, ts, n_tiles, rb, cb):
    i = pl.program_id(1)
    c = zc_ref.shape[1]
    win_ref[HALO:HALO + ts, :] = zc_ref[...]
    win_ref[0:HALO, :] = jnp.where(i > 0, zp_ref[...], 0.0)
    win_ref[HALO + ts:2 * HALO + ts, :] = jnp.where(i < n_tiles - 1, zn_ref[...], 0.0)
    base = HALO - CONV_PAD
    n_a = -(-CONV_K // SUBLANES)
    qn = rb + SUBLANES
    for r in range(ts // rb):
        for cc in range(c // cb):
            cs = slice(cc * cb, (cc + 1) * cb)
            rows = win_ref[r * rb:r * rb + qn + SUBLANES * (n_a - 1), cs]
            acc = jnp.zeros((rb, cb), F32)
            for rr in range(SUBLANES):
                q = None
                for a in range(n_a):
                    kk = SUBLANES * a + rr
                    if kk < CONV_K:
                        term = rows[SUBLANES * a:SUBLANES * a + qn, :] * w_ref[kk:kk + 1, cs]
                        q = term if q is None else q + term
                acc = acc + pltpu.roll(q, (qn - base - rr) % qn, 0)[0:rb, :]
            conv_ref[r * rb:(r + 1) * rb, cs] = acc + b_ref[:, cs]
    z = conv_ref[...]
    mu = jnp.mean(z, axis=-1, keepdims=True)
    zc = z - mu
    var = jnp.mean(zc * zc, axis=-1, keepdims=True)
    y = zc * lax.rsqrt(var + EPS) * g_ref[...] + beta_ref[...]
    o_ref[...] = (y * _sigmoid(y)).astype(o_ref.dtype)


def _conformer_conv(z, conv_w, conv_b, cn_g, cn_b):
    b, s, c = z.shape
    ts = _tile(s, 256)
    n_tiles = s // ts
    hb = ts // HALO
    n_hb = s // HALO
    cur = lambda bi, i: (bi, i, 0)
    prev = lambda bi, i: (bi, jnp.maximum(i * hb - 1, 0), 0)
    nxt = lambda bi, i: (bi, jnp.minimum((i + 1) * hb, n_hb - 1), 0)
    return pl.pallas_call(
        functools.partial(_conv_kernel, ts=ts, n_tiles=n_tiles, rb=_tile(ts, 64), cb=_tile(c, LANES)),
        grid=(b, n_tiles),
        in_specs=[
            pl.BlockSpec((None, ts, c), cur), pl.BlockSpec((None, HALO, c), prev), pl.BlockSpec((None, HALO, c), nxt),
            _const_spec((CONV_K, c)), _const_spec((1, c)), _const_spec((1, c)), _const_spec((1, c)),
        ],
        out_specs=pl.BlockSpec((None, ts, c), cur),
        out_shape=jax.ShapeDtypeStruct((b, s, c), BF16),
        scratch_shapes=[pltpu.VMEM((ts + 2 * HALO, c), F32), pltpu.VMEM((ts, c), F32)],
        compiler_params=_params("arbitrary", "arbitrary"),
        name="conformer_conv",
    )(z, z, z, conv_w, conv_b, cn_g, cn_b)


def _token_sublanes(d):
    assert d % (2 * LANES) == 0
    return d // (2 * LANES)


def _pack_pair(lo, hi):
    lo_bits = lax.bitcast_convert_type(lo.astype(BF16).astype(F32), jnp.uint32) >> 16
    hi_bits = lax.bitcast_convert_type(hi.astype(BF16).astype(F32), jnp.uint32) & jnp.uint32(0xFFFF0000)
    return lo_bits | hi_bits


def _unpack_pair(w):
    lo = lax.bitcast_convert_type(w << 16, F32)
    hi = lax.bitcast_convert_type(w & jnp.uint32(0xFFFF0000), F32)
    return lo, hi


def _store_token_tiles(v, ref, start=0):
    m, d = v.shape
    sub = _token_sublanes(d)
    for s in range(sub):
        lo = v[:, 2 * s * LANES:(2 * s + 1) * LANES]
        hi = v[:, (2 * s + 1) * LANES:(2 * s + 2) * LANES]
        ref[pl.ds(start + s, m, stride=sub), :] = _pack_pair(lo, hi)


def _load_token_tiles(ref, start, m, sub):
    parts = []
    for s in range(sub):
        parts.extend(_unpack_pair(ref[pl.ds(start + s, m, stride=sub), :]))
    return parts


def _finish_tile(xnew, g2n_ref, sh2_ref, sc2_ref, rwt_ref, x1_ref, h2_ref, h2p_ref, lgt_ref):
    x1_ref[...] = xnew
    h2 = _adaln(xnew, g2n_ref[...], sh2_ref[...], sc2_ref[...])
    h2b = h2.astype(BF16)
    h2_ref[...] = h2b
    _store_token_tiles(h2, h2p_ref)
    lgt_ref[...] = lax.dot_general(rwt_ref[...], h2b, (((1,), (1,)), ((), ())), preferred_element_type=F32)


def _outproj_kernel(a_ref, c_ref, x_ref, wa_ref, wc_ref, g1_ref, g2n_ref, sh2_ref, sc2_ref, rwt_ref,
                    x1_ref, h2_ref, h2p_ref, lgt_ref):
    y = jnp.dot(a_ref[...], wa_ref[...], preferred_element_type=F32)
    y = y + jnp.dot(c_ref[...], wc_ref[...], preferred_element_type=F32)
    _finish_tile(x_ref[...] + g1_ref[...] * y, g2n_ref, sh2_ref, sc2_ref, rwt_ref, x1_ref, h2_ref, h2p_ref, lgt_ref)


def _finish_specs(b, s, d, e, tm):
    nt = s // tm
    sub = _token_sublanes(d)
    row = lambda bi, i: (bi, i, 0)
    flat = lambda bi, i: (bi * nt + i, 0)
    out_specs = [pl.BlockSpec((None, tm, d), row), pl.BlockSpec((None, tm, d), row),
                 pl.BlockSpec((tm * sub, LANES), flat), pl.BlockSpec((e, tm), lambda bi, i: (0, bi * nt + i))]
    out_shape = [jax.ShapeDtypeStruct((b, s, d), F32), jax.ShapeDtypeStruct((b, s, d), BF16),
                 jax.ShapeDtypeStruct((b * s * sub, LANES), jnp.uint32), jax.ShapeDtypeStruct((e, b * s), F32)]
    return out_specs, out_shape


def _outproj(attn, cv, x, w_a, w_c, g1, g2n, sh2, sc2, rwt):
    b, s, d = x.shape
    e = rwt.shape[0]
    tm = _tile(s, 512)
    row = lambda bi, i: (bi, i, 0)
    vec = pl.BlockSpec((None, 1, d), lambda bi, i: (bi, 0, 0))
    out_specs, out_shape = _finish_specs(b, s, d, e, tm)
    return pl.pallas_call(
        _outproj_kernel,
        grid=(b, s // tm),
        in_specs=[
            pl.BlockSpec((None, tm, attn.shape[2]), row), pl.BlockSpec((None, tm, cv.shape[2]), row),
            pl.BlockSpec((None, tm, d), row), _const_spec(w_a.shape), _const_spec(w_c.shape),
            vec, _const_spec((1, d)), vec, vec, _const_spec((e, d)),
        ],
        out_specs=out_specs,
        out_shape=out_shape,
        compiler_params=_params("arbitrary", "arbitrary"),
        name="outproj",
    )(attn, cv, x, w_a, w_c, g1, g2n, sh2, sc2, rwt)


def _pool_kernel(xc_ref, xp_ref, xn_ref, g1n_ref, sh1_ref, sc1_ref, pw_ref, pb_ref, ps_ref, g1_ref,
                 g2n_ref, sh2_ref, sc2_ref, rwt_ref, x1_ref, h2_ref, h2p_ref, lgt_ref, win_ref, *, tm, n_tiles, seq):
    i = pl.program_id(1)
    d = xc_ref.shape[1]
    gc = d // len(POOL_WINDOWS)
    norm = lambda x: _adaln(x, g1n_ref[...], sh1_ref[...], sc1_ref[...])
    win_ref[HALO:HALO + tm, :] = norm(xc_ref[...])
    win_ref[0:HALO, :] = jnp.where(i > 0, norm(xp_ref[...]), 0.0)
    win_ref[HALO + tm:2 * HALO + tm, :] = jnp.where(i < n_tiles - 1, norm(xn_ref[...]), 0.0)
    t = i * tm + lax.broadcasted_iota(jnp.int32, (tm, 1), 0)
    n = tm + 2 * SUBLANES
    ahead = lambda a, k: pltpu.roll(a, n - k, 0)
    ys = []
    for gi, w in enumerate(POOL_WINDOWS):
        assert w in (2, 4, 8, 16)
        cs = slice(gi * gc, (gi + 1) * gc)
        run = win_ref[HALO - SUBLANES:HALO + tm + SUBLANES, cs]
        span = 1
        while 2 * span < w:
            run = run + ahead(run, span)
            span *= 2
        first = run[0:tm, :] if span == SUBLANES else ahead(run, SUBLANES - span)[0:tm, :]
        tot = first + run[SUBLANES:SUBLANES + tm, :]
        lo = jnp.clip(t - w // 2, 0, seq)
        hi = jnp.clip(t + w - w // 2, 0, seq)
        p = tot / (hi - lo).astype(F32) - win_ref[HALO:HALO + tm, cs]
        ys.append(jnp.dot(p.astype(BF16), pw_ref[gi], preferred_element_type=F32))
    y = (jnp.concatenate(ys, axis=-1) + pb_ref[...]) * ps_ref[...]
    _finish_tile(xc_ref[...] + g1_ref[...] * y, g2n_ref, sh2_ref, sc2_ref, rwt_ref, x1_ref, h2_ref, h2p_ref, lgt_ref)


def _pool_layer(x, g1n, sh1, sc1, pool_w, pool_b, pool_scale, g1, g2n, sh2, sc2, rwt):
    b, s, d = x.shape
    e = rwt.shape[0]
    tm = _tile(s, 256)
    n_tiles = s // tm
    hb = tm // HALO
    n_hb = s // HALO
    cur = lambda bi, i: (bi, i, 0)
    prev = lambda bi, i: (bi, jnp.maximum(i * hb - 1, 0), 0)
    nxt = lambda bi, i: (bi, jnp.minimum((i + 1) * hb, n_hb - 1), 0)
    vec = pl.BlockSpec((None, 1, d), lambda bi, i: (bi, 0, 0))
    cvec = _const_spec((1, d))
    out_specs, out_shape = _finish_specs(b, s, d, e, tm)
    return pl.pallas_call(
        functools.partial(_pool_kernel, tm=tm, n_tiles=n_tiles, seq=s),
        grid=(b, n_tiles),
        in_specs=[
            pl.BlockSpec((None, tm, d), cur), pl.BlockSpec((None, HALO, d), prev), pl.BlockSpec((None, HALO, d), nxt),
            cvec, vec, vec, _const_spec(pool_w.shape), cvec, cvec, vec, cvec, vec, vec, _const_spec((e, d)),
        ],
        out_specs=out_specs,
        out_shape=out_shape,
        scratch_shapes=[pltpu.VMEM((tm + 2 * HALO, d), F32)],
        compiler_params=_params("arbitrary", "arbitrary"),
        name="pool_mixer",
    )(x, x, x, g1n, sh1, sc1, pool_w, pool_b, pool_scale, g1, g2n, sh2, sc2, rwt)


def _route_kernel(lgt_ref, bias_ref, tri_ref, eid_ref, pos_ref, wts_ref, cnt_ref, carry_ref):
    i = pl.program_id(0)
    e, tn = lgt_ref.shape
    eg = e // N_EXPERT_GROUPS

    @pl.when(i == 0)
    def _():
        carry_ref[...] = jnp.zeros_like(carry_ref)

    scores = _sigmoid(lgt_ref[...])
    sel = scores + bias_ref[...]
    neg = jnp.float32(-jnp.inf)
    sub = lax.broadcasted_iota(jnp.int32, (eg, tn), 0)
    group_scores = []
    for g in range(N_EXPERT_GROUPS):
        blk = sel[g * eg:(g + 1) * eg, :]
        m1 = jnp.max(blk, axis=0, keepdims=True)
        first = jnp.min(jnp.where(blk == m1, sub, eg), axis=0, keepdims=True)
        m2 = jnp.max(jnp.where(sub == first, neg, blk), axis=0, keepdims=True)
        group_scores.append(m1 + m2)
    masked = []
    for g in range(N_EXPERT_GROUPS):
        rank = jnp.zeros((1, tn), jnp.int32)
        for g2 in range(N_EXPERT_GROUPS):
            if g2 == g:
                continue
            ahead = (group_scores[g2] >= group_scores[g]) if g2 < g else (group_scores[g2] > group_scores[g])
            rank = rank + ahead.astype(jnp.int32)
        masked.append(jnp.where(rank < TOPK_GROUPS, sel[g * eg:(g + 1) * eg, :], neg))
    selm = jnp.concatenate(masked, axis=0)

    eidx = lax.broadcasted_iota(jnp.int32, (e, tn), 0)
    work = selm
    picks = []
    for _ in range(TOP_K):
        mx = jnp.max(work, axis=0, keepdims=True)
        pick = jnp.min(jnp.where(work == mx, eidx, e), axis=0, keepdims=True)
        picks.append(pick)
        work = jnp.where(eidx == pick, neg, work)
    chosen = functools.reduce(jnp.logical_or, [eidx == p for p in picks])

    m = chosen.astype(BF16)
    pos = jnp.dot(m, tri_ref[...], preferred_element_type=F32) + carry_ref[...]
    carry_ref[...] = carry_ref[...] + jnp.sum(chosen.astype(F32), axis=1, keepdims=True)
    cnt_ref[...] = carry_ref[...]

    wsum = jnp.sum(jnp.where(chosen, scores, 0.0), axis=0, keepdims=True)
    for j, pick in enumerate(picks):
        hit = eidx == pick
        eid_ref[j:j + 1, :] = pick
        pos_ref[j:j + 1, :] = jnp.sum(jnp.where(hit, pos, 0.0), axis=0, keepdims=True).astype(jnp.int32)
        wj = jnp.sum(jnp.where(hit, scores, 0.0), axis=0, keepdims=True)
        wts_ref[j:j + 1, :] = wj / wsum * ROUTED_SCALE


def _route(lgt, router_bias):
    e, t = lgt.shape
    tn = _tile(t, 512)
    tri = (lax.broadcasted_iota(jnp.int32, (tn, tn), 0) < lax.broadcasted_iota(jnp.int32, (tn, tn), 1)).astype(BF16)
    col = lambda i: (0, i)
    return pl.pallas_call(
        _route_kernel,
        grid=(t // tn,),
        in_specs=[pl.BlockSpec((e, tn), col), _const_spec((e, 1)), _const_spec((tn, tn))],
        out_specs=[pl.BlockSpec((TOP_K, tn), col), pl.BlockSpec((TOP_K, tn), col), pl.BlockSpec((TOP_K, tn), col),
                   pl.BlockSpec((e, 1), lambda i: (0, 0))],
        out_shape=[jax.ShapeDtypeStruct((TOP_K, t), jnp.int32), jax.ShapeDtypeStruct((TOP_K, t), jnp.int32),
                   jax.ShapeDtypeStruct((TOP_K, t), F32), jax.ShapeDtypeStruct((e, 1), F32)],
        scratch_shapes=[pltpu.VMEM((e, 1), F32)],
        compiler_params=_params("arbitrary"),
        name="route",
    )(lgt, router_bias.reshape(e, 1), tri)


def _dest_kernel(poff_ref, eid_ref, pos_ref, dest_ref, *, sub):
    eid = eid_ref[...]
    acc = pos_ref[...]
    for e in range(poff_ref.shape[0]):
        acc = acc + jnp.where(eid == e, poff_ref[e], 0)
    dest_ref[...] = acc * sub


def _dest_rows(poff, eid, pos, sub):
    k, t = eid.shape
    tn = _tile(t, 2048)
    col = lambda i, *_: (0, i)
    return pl.pallas_call(
        functools.partial(_dest_kernel, sub=sub),
        grid_spec=pltpu.PrefetchScalarGridSpec(
            num_scalar_prefetch=1, grid=(t // tn,),
            in_specs=[pl.BlockSpec((k, tn), col), pl.BlockSpec((k, tn), col)],
            out_specs=pl.BlockSpec((k, tn), col)),
        out_shape=jax.ShapeDtypeStruct((k, t), jnp.int32),
        compiler_params=_params("arbitrary"),
        name="dest_rows",
    )(poff, eid, pos)


def _dispatch_kernel(pend_ref, pcnt_ref, dest_ref, h_ref, xs_ref, zero_ref, sem, zsem, *, tt, rows, sub):
    i = pl.program_id(0)
    n_exp = pend_ref.shape[0]

    def zero_copy(e):
        start = pl.multiple_of((pend_ref[e] - rows) * sub, rows * sub)
        return pltpu.make_async_copy(zero_ref, xs_ref.at[pl.ds(start, rows * sub)], zsem)

    @pl.when(i == 0)
    def _():
        zero_ref[...] = jnp.zeros_like(zero_ref)

        def start_zero(e, c):
            @pl.when(pcnt_ref[e] > 0)
            def _():
                zero_copy(e).start()
            return c

        def wait_zero(e, c):
            @pl.when(pcnt_ref[e] > 0)
            def _():
                zero_copy(e).wait()
            return c

        lax.fori_loop(0, n_exp, start_zero, 0)
        lax.fori_loop(0, n_exp, wait_zero, 0)

    def row_copy(t, j):
        src = h_ref.at[pl.ds(pl.multiple_of(t * sub, sub), sub)]
        dst = xs_ref.at[pl.ds(pl.multiple_of(dest_ref[j, t], sub), sub)]
        return pltpu.make_async_copy(src, dst, sem)

    def issue(t, c):
        for j in range(TOP_K):
            row_copy(t, j).start(priority=j % 2)
        return c

    def drain(t, c):
        for j in range(TOP_K):
            row_copy(t, j).wait()
        return c

    lax.fori_loop(0, tt, issue, 0)
    lax.fori_loop(0, tt, drain, 0)


def _dispatch(h2p, dest, pend, pcnt, n_rows, rows, sub):
    t = h2p.shape[0] // sub
    tt = _tile(t, 256)
    return pl.pallas_call(
        functools.partial(_dispatch_kernel, tt=tt, rows=rows, sub=sub),
        grid_spec=pltpu.PrefetchScalarGridSpec(
            num_scalar_prefetch=2, grid=(t // tt,),
            in_specs=[pl.BlockSpec((TOP_K, tt), lambda i, *_: (0, i), memory_space=pltpu.SMEM),
                      pl.BlockSpec((tt * sub, LANES), lambda i, *_: (i, 0))],
            out_specs=pl.BlockSpec(memory_space=pl.ANY),
            scratch_shapes=[pltpu.VMEM((rows * sub, LANES), jnp.uint32), pltpu.SemaphoreType.DMA,
                            pltpu.SemaphoreType.DMA]),
        out_shape=jax.ShapeDtypeStruct((n_rows * sub, LANES), jnp.uint32),
        compiler_params=_params("arbitrary"),
        name="dispatch",
    )(pend, pcnt, dest, h2p)


def _expert_kernel(blk_e_ref, nreal_ref, first_ref, slot_ref, next_e_ref, has_next_ref, x_ref, wg_hbm, wu_hbm, wd_hbm,
                   y_ref, wg32_ref, wu32_ref, wd32_ref, wgb_ref, wub_ref, wdb_ref, wsem, *, rows, sub, layer):
    b = pl.program_id(0)

    def weight_copies(e, slot):
        return [pltpu.make_async_copy(src.at[layer, e], dst.at[slot], wsem.at[slot])
                for src, dst in ((wg_hbm, wg32_ref), (wu_hbm, wu32_ref), (wd_hbm, wd32_ref))]

    @pl.when(first_ref[b] == 1)
    def _():
        e = blk_e_ref[b]
        slot = slot_ref[b]

        @pl.when(b == 0)
        def _():
            for cp in weight_copies(e, slot):
                cp.start()

        @pl.when(has_next_ref[b] == 1)
        def _():
            for cp in weight_copies(next_e_ref[b], 1 - slot):
                cp.start()

        for cp in weight_copies(e, slot):
            cp.wait()
        wgb_ref[...] = wg32_ref[slot].astype(BF16)
        wub_ref[...] = wu32_ref[slot].astype(BF16)
        wdb_ref[...] = wd32_ref[slot].astype(BF16)

    @pl.when(b < nreal_ref[0])
    def _():
        x = jnp.concatenate([p.astype(BF16) for p in _load_token_tiles(x_ref, 0, rows, sub)], axis=1)
        g = jnp.dot(x, wgb_ref[...], preferred_element_type=F32)
        u = jnp.dot(x, wub_ref[...], preferred_element_type=F32)
        a = (g * _sigmoid(g) * u).astype(BF16)
        _store_token_tiles(jnp.dot(a, wdb_ref[...], preferred_element_type=F32), y_ref)


def _experts(xs, blk_e, n_real, pend, w_gate, w_up, w_down, layer, rows, sub):
    d, ff = w_gate.shape[2:]
    nb = xs.shape[0] // (rows * sub)
    blocks = jnp.arange(nb, dtype=jnp.int32)
    valid = blocks < n_real[0]
    prev_e = jnp.concatenate([jnp.full((1,), -1, jnp.int32), blk_e[:-1]])
    first = jnp.logical_and(valid, blk_e != prev_e)
    slot = (jnp.cumsum(first.astype(jnp.int32)) - 1) % 2
    next_blk = pend[blk_e] // rows
    has_next = jnp.logical_and(first, next_blk < n_real[0])
    next_e = blk_e[jnp.minimum(next_blk, nb - 1)]
    i32 = lambda a: a.astype(jnp.int32)

    xrow = lambda b, be, nr, *_: (jnp.minimum(b, jnp.maximum(nr[0] - 1, 0)), 0)
    hbm = pl.BlockSpec(memory_space=pl.ANY)
    return pl.pallas_call(
        functools.partial(_expert_kernel, rows=rows, sub=sub, layer=layer),
        grid_spec=pltpu.PrefetchScalarGridSpec(
            num_scalar_prefetch=6, grid=(nb,),
            in_specs=[pl.BlockSpec((rows * sub, LANES), xrow), hbm, hbm, hbm],
            out_specs=pl.BlockSpec((rows * sub, LANES), xrow),
            scratch_shapes=[pltpu.VMEM((2, d, ff), F32), pltpu.VMEM((2, d, ff), F32), pltpu.VMEM((2, ff, d), F32),
                            pltpu.VMEM((d, ff), BF16), pltpu.VMEM((d, ff), BF16), pltpu.VMEM((ff, d), BF16),
                            pltpu.SemaphoreType.DMA((2,))]),
        out_shape=jax.ShapeDtypeStruct(xs.shape, jnp.uint32),
        compiler_params=_params("arbitrary"),
        name="experts",
    )(blk_e, n_real, i32(first), i32(slot), i32(next_e), i32(has_next), xs, w_gate, w_up, w_down)


def _combine_kernel(dest_ref, dnext_ref, ys_ref, wt_ref, x1_ref, h_ref, sg_ref, su_ref, sd_ref, g2_ref, o_ref,
                    gbuf_ref, sems, *, tt, sub):
    i = pl.program_id(0)
    slot_rows = TOP_K * tt * sub
    t_chunk = tt // TOP_K

    def gather(dref, col, slot, t, j):
        src = ys_ref.at[pl.ds(pl.multiple_of(dref[j, col + t], sub), sub)]
        dst = gbuf_ref.at[pl.ds(slot * slot_rows + (j * tt + t) * sub, sub)]
        return pltpu.make_async_copy(src, dst, sems.at[slot])

    def wait_tile(slot):
        def drain(t, c):
            for j in range(TOP_K):
                gather(dest_ref, 0, slot, 0, j).wait()
            return c
        lax.fori_loop(0, tt, drain, 0)

    @pl.when(i == 0)
    def _():
        def issue(t, c):
            for j in range(TOP_K):
                gather(dest_ref, 0, 0, t, j).start(priority=j % 2)
            return c
        lax.fori_loop(0, tt, issue, 0)

    def tile(slot, row0, dref, col):
        rows = slice(row0, row0 + tt)
        h = h_ref[rows, :]
        g = jnp.dot(h, sg_ref[...], preferred_element_type=F32)
        u = jnp.dot(h, su_ref[...], preferred_element_type=F32)
        shared = jnp.dot((g * _sigmoid(g) * u).astype(BF16), sd_ref[...], preferred_element_type=F32)
        wt = wt_ref[rows, :]
        acc = [jnp.zeros((tt, LANES), F32) for _ in range(2 * sub)]
        for j in range(TOP_K):
            for t in range(j * t_chunk, (j + 1) * t_chunk):
                for jj in range(TOP_K):
                    gather(dref, col, 1 - slot, t, jj).start(priority=jj % 2)
            wj = jnp.broadcast_to(wt[:, j:j + 1], (tt, LANES))
            parts = _load_token_tiles(gbuf_ref, slot * slot_rows + j * tt * sub, tt, sub)
            acc = [a + p * wj for a, p in zip(acc, parts)]
        routed = jnp.concatenate(acc, axis=1)
        o_ref[rows, :] = x1_ref[rows, :] + g2_ref[...] * (routed + shared)

    wait_tile(0)
    tile(0, 0, dest_ref, tt)
    wait_tile(1)
    tile(1, tt, dnext_ref, 0)

    @pl.when(i == pl.num_programs(0) - 1)
    def _():
        wait_tile(0)


def _combine(ys, dest, wts_t, x1, h2, s_gate, s_up, s_down, g2, seq, sub):
    t, d = x1.shape
    sf = s_gate.shape[1]
    tt = _tile(seq // 2, 128)
    assert tt % TOP_K == 0
    n_steps = t // (2 * tt)
    row = lambda i: (i, 0)
    return pl.pallas_call(
        functools.partial(_combine_kernel, tt=tt, sub=sub),
        grid=(n_steps,),
        in_specs=[
            pl.BlockSpec((TOP_K, 2 * tt), lambda i: (0, i), memory_space=pltpu.SMEM),
            pl.BlockSpec((TOP_K, tt), lambda i: (0, jnp.minimum(2 * i + 2, 2 * n_steps - 2)),
                         memory_space=pltpu.SMEM),
            pl.BlockSpec(memory_space=pl.ANY),
            pl.BlockSpec((2 * tt, TOP_K), row), pl.BlockSpec((2 * tt, d), row), pl.BlockSpec((2 * tt, d), row),
            _const_spec((d, sf)), _const_spec((d, sf)), _const_spec((sf, d)),
            pl.BlockSpec((None, 1, d), lambda i: ((i * 2 * tt) // seq, 0, 0)),
        ],
        out_specs=pl.BlockSpec((2 * tt, d), row),
        out_shape=jax.ShapeDtypeStruct((t, d), F32),
        scratch_shapes=[pltpu.VMEM((2 * TOP_K * tt * sub, LANES), jnp.uint32), pltpu.SemaphoreType.DMA((2,))],
        compiler_params=_params("arbitrary"),
        name="combine",
    )(dest, dest, ys, wts_t, x1, h2, s_gate, s_up, s_down, g2)


def _moe(x1, h2, h2p, lgt, g2, router_bias, w_gate, w_up, w_down, layer, s_gate, s_up, s_down):
    b, s, d = x1.shape
    t = b * s
    e = lgt.shape[0]
    sub = _token_sublanes(d)
    rows = _tile(t, MOE_BLOCK_ROWS)
    eid, pos, wts, cnt = _route(lgt, router_bias)
    counts = cnt[:, 0].astype(jnp.int32)
    pcnt = (counts + rows - 1) // rows * rows
    pend = jnp.cumsum(pcnt)
    poff = pend - pcnt
    nb = (t * TOP_K + e * (rows - 1)) // rows
    n_real = (pend[-1] // rows).astype(jnp.int32).reshape(1)
    starts = jnp.arange(nb, dtype=jnp.int32) * rows
    blk_e = jnp.minimum(jnp.sum((pend[None, :] <= starts[:, None]).astype(jnp.int32), axis=1), e - 1)
    dest = _dest_rows(poff.astype(jnp.int32), eid, pos, sub)
    xs = _dispatch(h2p, dest, pend.astype(jnp.int32), pcnt.astype(jnp.int32), nb * rows, rows, sub)
    ys = _experts(xs, blk_e, n_real, pend.astype(jnp.int32), w_gate, w_up, w_down, layer, rows, sub)
    out = _combine(ys, dest, wts.T, x1.reshape(t, d), h2.reshape(t, d), s_gate.astype(BF16), s_up.astype(BF16),
                   s_down.astype(BF16), g2, s, sub)
    return out.reshape(b, s, d)


def _rope_tables(n_tok):
    axis_dim = HEAD_DIM // 2
    rows = n_tok // GRID_W
    r, col = jnp.meshgrid(jnp.arange(rows), jnp.arange(GRID_W), indexing="ij")
    pos = jnp.stack([r.reshape(-1), col.reshape(-1)], axis=-1).astype(F32)
    inv = ROPE_THETA ** (-jnp.arange(0, axis_dim, 2, dtype=F32) / axis_dim)
    ang = pos[:, :, None] * inv
    ang = jnp.broadcast_to(ang[:, :, None, :], (n_tok, 2, 2, axis_dim // 2)).reshape(n_tok, HEAD_DIM)
    sign = jnp.where((jnp.arange(HEAD_DIM) % axis_dim) < axis_dim // 2, -1.0, 1.0).astype(F32)
    return jnp.cos(ang), jnp.sin(ang) * sign


def kernel(x, c, ctx, c_ctx, w_mod, b_mod, norm1_g, norm2_g, mix_w_in, q_norm_g, k_norm_g, conv_w, conv_b,
           conv_norm_g, conv_norm_b, mix_w_out, pool_w, pool_b, pool_scale, router_w, router_bias,
           moe_w_gate, moe_w_up, moe_w_down, shared_w_gate, shared_w_up, shared_w_down):
    b, s, d = x.shape
    depth = w_mod.shape[0]
    assert depth == 2, "layer schedule below is written for an attention layer followed by a pooling layer"
    kv_w = KV_HEADS * HEAD_DIM
    q_w = mix_w_in.shape[2] - 2 * kv_w - d

    pad = (-(b + 1)) % SUBLANES
    cc = jnp.concatenate([c, c_ctx[None, :], jnp.zeros((pad, d), F32)], axis=0)
    mods = _modulation(cc, w_mod, b_mod)

    def mod(layer, k, rows=slice(0, b)):
        return mods[layer, rows, k * d:(k + 1) * d][:, None, :]

    vec = lambda a: a.reshape(1, -1)
    rwt = jnp.swapaxes(router_w, 1, 2).astype(BF16)
    cos, sin_signed = _rope_tables(s)

    w_in = mix_w_in[0].astype(BF16)
    q, k, v, z = _inproj(x, mod(0, 0), mod(0, 1), vec(norm1_g[0]), w_in, vec(q_norm_g[0]), vec(k_norm_g[0]),
                         cos, sin_signed)
    ctx_row = slice(b, b + 1)
    kc, vc = _ctx_kv(ctx, mods[0, ctx_row, 0:d], mods[0, ctx_row, d:2 * d], vec(norm1_g[0]),
                     w_in[:, q_w:q_w + 2 * kv_w], vec(k_norm_g[0]))
    attn = _attention(q, k, v, kc, vc)
    cv = _conformer_conv(z, conv_w[0], vec(conv_b[0]), vec(conv_norm_g[0]), vec(conv_norm_b[0]))
    w_out = mix_w_out[0].astype(BF16)
    x1, h2, h2p, lgt = _outproj(attn, cv, x, w_out[:q_w], w_out[q_w:], mod(0, 2), vec(norm2_g[0]), mod(0, 3),
                                mod(0, 4), rwt[0])
    x = _moe(x1, h2, h2p, lgt, mod(0, 5), router_bias[0], moe_w_gate, moe_w_up, moe_w_down, 0,
             shared_w_gate[0], shared_w_up[0], shared_w_down[0])

    x1, h2, h2p, lgt = _pool_layer(x, vec(norm1_g[1]), mod(1, 0), mod(1, 1), pool_w[0].astype(BF16), vec(pool_b[0]),
                                   vec(pool_scale[0]), mod(1, 2), vec(norm2_g[1]), mod(1, 3), mod(1, 4), rwt[1])
    x = _moe(x1, h2, h2p, lgt, mod(1, 5), router_bias[1], moe_w_gate, moe_w_up, moe_w_down, 1,
             shared_w_gate[1], shared_w_up[1], shared_w_down[1])
    return x
```

```python
import functools

import jax
import jax.numpy as jnp
from jax import lax
from jax.experimental import pallas as pl
from jax.experimental.pallas import tpu as pltpu

HEAD_DIM = 128
KV_HEADS = 2
GRID_W = 64
ROPE_THETA = 10000.0
EPS = 1e-6
CONV_K = 31
CONV_PAD = CONV_K // 2
POOL_WINDOWS = (2, 4, 8, 16)
N_EXPERT_GROUPS = 8
TOPK_GROUPS = 4
TOP_K = 8
ROUTED_SCALE = 2.5
LOG2_E = 1.4426950408889634

LANES = 128
SUBLANES = 8
HALO = 16
assert HALO % SUBLANES == 0 and 0 <= HALO - CONV_PAD <= 1
MOE_BLOCK_ROWS = 512
VMEM_LIMIT_BYTES = 56 * 1024 * 1024

F32 = jnp.float32
BF16 = jnp.bfloat16


def _tile(n, pref):
    t = min(n, pref)
    while n % t:
        t //= 2
    return t


def _params(*sem):
    return pltpu.CompilerParams(dimension_semantics=sem, vmem_limit_bytes=VMEM_LIMIT_BYTES)


def _const_spec(shape):
    nd = len(shape)
    return pl.BlockSpec(shape, lambda *_: (0,) * nd, pipeline_mode=pl.Buffered(1))


def _sigmoid(x):
    return 1.0 / (1.0 + jnp.exp(-x))


def _adaln(x, g, shift, scale):
    ms = jnp.mean(x * x, axis=-1, keepdims=True)
    return (x * lax.rsqrt(ms + EPS) * g) * (1.0 + scale) + shift


def _mod_kernel(c_ref, w_ref, b_ref, o_ref):
    c = c_ref[...]
    a = c * _sigmoid(c)
    o_ref[...] = jnp.dot(a.astype(BF16), w_ref[...].astype(BF16), preferred_element_type=F32) + b_ref[...]


def _modulation(cc, w_mod, b_mod):
    n_layers, d, n = w_mod.shape
    r = cc.shape[0]
    tn = _tile(n, 1024)
    return pl.pallas_call(
        _mod_kernel,
        grid=(n_layers, n // tn),
        in_specs=[
            pl.BlockSpec((r, d), lambda l, j: (0, 0)),
            pl.BlockSpec((None, d, tn), lambda l, j: (l, 0, j)),
            pl.BlockSpec((None, 1, tn), lambda l, j: (l, 0, j)),
        ],
        out_specs=pl.BlockSpec((None, r, tn), lambda l, j: (l, 0, j)),
        out_shape=jax.ShapeDtypeStruct((n_layers, r, n), F32),
        compiler_params=_params("arbitrary", "arbitrary"),
        name="modulation",
    )(cc, w_mod, b_mod.reshape(n_layers, 1, n))


def _head_norm(x, g):
    ms = jnp.mean(x * x, axis=-1, keepdims=True)
    return x * lax.rsqrt(ms + EPS) * g


def _rope(x, cos, sin_signed, first_half):
    up = pltpu.roll(x, HEAD_DIM - HEAD_DIM // 4, 1)
    dn = pltpu.roll(x, HEAD_DIM // 4, 1)
    return x * cos + jnp.where(first_half, up, dn) * sin_signed


def _inproj_kernel(x_ref, sh_ref, sc_ref, g_ref, w_ref, qg_ref, kg_ref, cos_ref, sin_ref,
                   q_ref, k_ref, v_ref, z_ref, *, n_heads, conv_c):
    h = _adaln(x_ref[...], g_ref[...], sh_ref[...], sc_ref[...]).astype(BF16)
    acc = jnp.dot(h, w_ref[...], preferred_element_type=F32)
    cos = cos_ref[...]
    sin = sin_ref[...]
    lane = lax.broadcasted_iota(jnp.int32, cos.shape, 1)
    first_half = (lane % (HEAD_DIM // 2)) < (HEAD_DIM // 4)
    q_w = n_heads * HEAD_DIM
    kv_w = KV_HEADS * HEAD_DIM
    q_scale = HEAD_DIM ** -0.5 * LOG2_E
    for hd in range(n_heads):
        sl = slice(hd * HEAD_DIM, (hd + 1) * HEAD_DIM)
        qn = _head_norm(acc[:, sl], qg_ref[...])
        q_ref[:, sl] = (_rope(qn, cos, sin, first_half) * q_scale).astype(q_ref.dtype)
    for hd in range(KV_HEADS):
        sl = slice(hd * HEAD_DIM, (hd + 1) * HEAD_DIM)
        kn = _head_norm(acc[:, q_w + hd * HEAD_DIM:q_w + (hd + 1) * HEAD_DIM], kg_ref[...])
        k_ref[:, sl] = _rope(kn, cos, sin, first_half).astype(k_ref.dtype)
    v_ref[...] = acc[:, q_w + kv_w:q_w + 2 * kv_w].astype(v_ref.dtype)
    u0 = q_w + 2 * kv_w
    a = acc[:, u0:u0 + conv_c]
    gate = acc[:, u0 + conv_c:u0 + 2 * conv_c]
    z_ref[...] = a * _sigmoid(gate)


def _inproj(x, sh, sc, g, w_in, q_g, k_g, cos, sin_signed):
    b, s, d = x.shape
    in_w = w_in.shape[1]
    kv_w = KV_HEADS * HEAD_DIM
    conv_c = d // 2
    q_w = in_w - 2 * kv_w - 2 * conv_c
    n_heads = q_w // HEAD_DIM
    tm = _tile(s, 512)
    row = lambda bi, i: (bi, i, 0)
    vec = pl.BlockSpec((None, 1, d), lambda bi, i: (bi, 0, 0))
    tab = pl.BlockSpec((tm, HEAD_DIM), lambda bi, i: (i, 0))
    return pl.pallas_call(
        functools.partial(_inproj_kernel, n_heads=n_heads, conv_c=conv_c),
        grid=(b, s // tm),
        in_specs=[
            pl.BlockSpec((None, tm, d), row), vec, vec, _const_spec((1, d)), _const_spec((d, in_w)),
            _const_spec((1, HEAD_DIM)), _const_spec((1, HEAD_DIM)), tab, tab,
        ],
        out_specs=[
            pl.BlockSpec((None, tm, q_w), row), pl.BlockSpec((None, tm, kv_w), row),
            pl.BlockSpec((None, tm, kv_w), row), pl.BlockSpec((None, tm, conv_c), row),
        ],
        out_shape=[
            jax.ShapeDtypeStruct((b, s, q_w), BF16), jax.ShapeDtypeStruct((b, s, kv_w), BF16),
            jax.ShapeDtypeStruct((b, s, kv_w), BF16), jax.ShapeDtypeStruct((b, s, conv_c), F32),
        ],
        compiler_params=_params("arbitrary", "arbitrary"),
        name="inproj",
    )(x, sh, sc, g, w_in, q_g, k_g, cos, sin_signed)


def _ctx_kv_kernel(x_ref, sh_ref, sc_ref, g_ref, w_ref, kg_ref, k_ref, v_ref):
    h = _adaln(x_ref[...], g_ref[...], sh_ref[...], sc_ref[...]).astype(BF16)
    acc = jnp.dot(h, w_ref[...], preferred_element_type=F32)
    kv_w = KV_HEADS * HEAD_DIM
    for hd in range(KV_HEADS):
        sl = slice(hd * HEAD_DIM, (hd + 1) * HEAD_DIM)
        k_ref[:, sl] = _head_norm(acc[:, sl], kg_ref[...]).astype(k_ref.dtype)
    v_ref[...] = acc[:, kv_w:2 * kv_w].astype(v_ref.dtype)


def _ctx_kv(ctx, sh, sc, g, w_kv, k_g):
    b, lc, d = ctx.shape
    kv_w = KV_HEADS * HEAD_DIM
    row = lambda bi: (bi, 0, 0)
    return pl.pallas_call(
        _ctx_kv_kernel,
        grid=(b,),
        in_specs=[
            pl.BlockSpec((None, lc, d), row), _const_spec((1, d)), _const_spec((1, d)), _const_spec((1, d)),
            _const_spec((d, 2 * kv_w)), _const_spec((1, HEAD_DIM)),
        ],
        out_specs=[pl.BlockSpec((None, lc, kv_w), row), pl.BlockSpec((None, lc, kv_w), row)],
        out_shape=[jax.ShapeDtypeStruct((b, lc, kv_w), BF16), jax.ShapeDtypeStruct((b, lc, kv_w), BF16)],
        compiler_params=_params("arbitrary"),
        name="ctx_kv",
    )(ctx, sh, sc, g, w_kv, k_g)


def _attn_kernel(q_ref, k_ref, v_ref, kc_ref, vc_ref, o_ref, *, group, tk):
    tq = q_ref.shape[0]
    rows = group * tq
    q = jnp.concatenate([q_ref[:, g * HEAD_DIM:(g + 1) * HEAD_DIM] for g in range(group)], axis=0)
    nt = (((1,), (1,)), ((), ()))
    chunks = [(k_ref, v_ref, c * tk, tk) for c in range(k_ref.shape[0] // tk)]
    chunks.append((kc_ref, vc_ref, 0, kc_ref.shape[0]))
    m = jnp.full((rows, 1), -jnp.inf, F32)
    l = jnp.zeros((rows, 1), F32)
    acc = jnp.zeros((rows, HEAD_DIM), F32)
    for kr, vr, start, size in chunks:
        s = lax.dot_general(q, kr[start:start + size, :], nt, preferred_element_type=F32)
        m_new = jnp.maximum(m, jnp.max(s, axis=-1, keepdims=True))
        alpha = jnp.exp2(m - m_new)
        p = jnp.exp2(s - m_new)
        l = alpha * l + jnp.sum(p, axis=-1, keepdims=True)
        acc = alpha * acc + jnp.dot(p.astype(BF16), vr[start:start + size, :], preferred_element_type=F32)
        m = m_new
    o = acc / l
    for g in range(group):
        o_ref[:, g * HEAD_DIM:(g + 1) * HEAD_DIM] = o[g * tq:(g + 1) * tq, :].astype(o_ref.dtype)


def _attention(q, k, v, kc, vc):
    b, s, q_w = q.shape
    lc = kc.shape[1]
    group = q_w // (KV_HEADS * HEAD_DIM)
    tq = _tile(s, 512)
    kv_spec = lambda n: pl.BlockSpec((None, n, HEAD_DIM), lambda bi, h, i: (bi, 0, h))
    q_spec = pl.BlockSpec((None, tq, group * HEAD_DIM), lambda bi, h, i: (bi, i, h))
    return pl.pallas_call(
        functools.partial(_attn_kernel, group=group, tk=_tile(s, 512)),
        grid=(b, KV_HEADS, s // tq),
        in_specs=[q_spec, kv_spec(s), kv_spec(s), kv_spec(lc), kv_spec(lc)],
        out_specs=q_spec,
        out_shape=jax.ShapeDtypeStruct((b, s, q_w), BF16),
        compiler_params=_params("arbitrary", "arbitrary", "arbitrary"),
        name="attention",
    )(q, k, v, kc, vc)


def _conv_kernel(zc_ref, zp_ref, zn_ref, w_ref, b_ref, g_ref, beta_ref, o_ref, win_ref, conv_ref,
                 *, ts, n_tiles, rb, cb):
    i = pl.program_id(1)
    c = zc_ref.shape[1]
    win_ref[HALO:HALO + ts, :] = zc_ref[...]
    win_ref[0:HALO, :] = jnp.where(i > 0, zp_ref[...], 0.0)
    win_ref[HALO + ts:2 * HALO + ts, :] = jnp.where(i < n_tiles - 1, zn_ref[...], 0.0)
    base = HALO - CONV_PAD
    n_a = -(-CONV_K // SUBLANES)
    qn = rb + SUBLANES
    for r in range(ts // rb):
        for cc in range(c // cb):
            cs = slice(cc * cb, (cc + 1) * cb)
            rows = win_ref[r * rb:r * rb + qn + SUBLANES * (n_a - 1), cs]
            acc = jnp.zeros((rb, cb), F32)
            for rr in range(SUBLANES):
                q = None
                for a in range(n_a):
                    kk = SUBLANES * a + rr
                    if kk < CONV_K:
                        term = rows[SUBLANES * a:SUBLANES * a + qn, :] * w_ref[kk:kk + 1, cs]
                        q = term if q is None else q + term
                acc = acc + pltpu.roll(q, (qn - base - rr) % qn, 0)[0:rb, :]
            conv_ref[r * rb:(r + 1) * rb, cs] = acc + b_ref[:, cs]
    z = conv_ref[...]
    mu = jnp.mean(z, axis=-1, keepdims=True)
    zc = z - mu
    var = jnp.mean(zc * zc, axis=-1, keepdims=True)
    y = zc * lax.rsqrt(var + EPS) * g_ref[...] + beta_ref[...]
    o_ref[...] = (y * _sigmoid(y)).astype(o_ref.dtype)


def _conformer_conv(z, conv_w, conv_b, cn_g, cn_b):
    b, s, c = z.shape
    ts = _tile(s, 256)
    n_tiles = s // ts
    hb = ts // HALO
    n_hb = s // HALO
    cur = lambda bi, i: (bi, i, 0)
    prev = lambda bi, i: (bi, jnp.maximum(i * hb - 1, 0), 0)
    nxt = lambda bi, i: (bi, jnp.minimum((i + 1) * hb, n_hb - 1), 0)
    return pl.pallas_call(
        functools.partial(_conv_kernel, ts=ts, n_tiles=n_tiles, rb=_tile(ts, 64), cb=_tile(c, LANES)),
        grid=(b, n_tiles),
        in_specs=[
            pl.BlockSpec((None, ts, c), cur), pl.BlockSpec((None, HALO, c), prev), pl.BlockSpec((None, HALO, c), nxt),
            _const_spec((CONV_K, c)), _const_spec((1, c)), _const_spec((1, c)), _const_spec((1, c)),
        ],
        out_specs=pl.BlockSpec((None, ts, c), cur),
        out_shape=jax.ShapeDtypeStruct((b, s, c), BF16),
        scratch_shapes=[pltpu.VMEM((ts + 2 * HALO, c), F32), pltpu.VMEM((ts, c), F32)],
        compiler_params=_params("arbitrary", "arbitrary"),
        name="conformer_conv",
    )(z, z, z, conv_w, conv_b, cn_g, cn_b)


def _token_sublanes(d):
    assert d % (2 * LANES) == 0
    return d // (2 * LANES)


def _pack_pair(lo, hi):
    lo_bits = lax.bitcast_convert_type(lo.astype(BF16).astype(F32), jnp.uint32) >> 16
    hi_bits = lax.bitcast_convert_type(hi.astype(BF16).astype(F32), jnp.uint32) & jnp.uint32(0xFFFF0000)
    return lo_bits | hi_bits


def _unpack_pair(w):
    lo = lax.bitcast_convert_type(w << 16, F32)
    hi = lax.bitcast_convert_type(w & jnp.uint32(0xFFFF0000), F32)
    return lo, hi


def _store_token_tiles(v, ref, start=0):
    m, d = v.shape
    sub = _token_sublanes(d)
    for s in range(sub):
        lo = v[:, 2 * s * LANES:(2 * s + 1) * LANES]
        hi = v[:, (2 * s + 1) * LANES:(2 * s + 2) * LANES]
        ref[pl.ds(start + s, m, stride=sub), :] = _pack_pair(lo, hi)


def _load_token_tiles(ref, start, m, sub):
    parts = []
    for s in range(sub):
        parts.extend(_unpack_pair(ref[pl.ds(start + s, m, stride=sub), :]))
    return parts


def _finish_tile(xnew, g2n_ref, sh2_ref, sc2_ref, rwt_ref, x1_ref, h2p_ref, lgt_ref):
    x1_ref[...] = xnew
    h2 = _adaln(xnew, g2n_ref[...], sh2_ref[...], sc2_ref[...])
    _store_token_tiles(h2, h2p_ref)
    lgt_ref[...] = lax.dot_general(rwt_ref[...], h2.astype(BF16), (((1,), (1,)), ((), ())),
                                   preferred_element_type=F32)


def _outproj_kernel(a_ref, c_ref, x_ref, wa_ref, wc_ref, g1_ref, g2n_ref, sh2_ref, sc2_ref, rwt_ref,
                    x1_ref, h2p_ref, lgt_ref):
    y = jnp.dot(a_ref[...], wa_ref[...], preferred_element_type=F32)
    y = y + jnp.dot(c_ref[...], wc_ref[...], preferred_element_type=F32)
    _finish_tile(x_ref[...] + g1_ref[...] * y, g2n_ref, sh2_ref, sc2_ref, rwt_ref, x1_ref, h2p_ref, lgt_ref)


def _finish_specs(b, s, d, e, tm):
    nt = s // tm
    sub = _token_sublanes(d)
    row = lambda bi, i: (bi, i, 0)
    flat = lambda bi, i: (bi * nt + i, 0)
    out_specs = [pl.BlockSpec((None, tm, d), row), pl.BlockSpec((tm * sub, LANES), flat),
                 pl.BlockSpec((e, tm), lambda bi, i: (0, bi * nt + i))]
    out_shape = [jax.ShapeDtypeStruct((b, s, d), F32), jax.ShapeDtypeStruct((b * s * sub, LANES), jnp.uint32),
                 jax.ShapeDtypeStruct((e, b * s), F32)]
    return out_specs, out_shape


def _outproj(attn, cv, x, w_a, w_c, g1, g2n, sh2, sc2, rwt):
    b, s, d = x.shape
    e = rwt.shape[0]
    tm = _tile(s, 512)
    row = lambda bi, i: (bi, i, 0)
    vec = pl.BlockSpec((None, 1, d), lambda bi, i: (bi, 0, 0))
    out_specs, out_shape = _finish_specs(b, s, d, e, tm)
    return pl.pallas_call(
        _outproj_kernel,
        grid=(b, s // tm),
        in_specs=[
            pl.BlockSpec((None, tm, attn.shape[2]), row), pl.BlockSpec((None, tm, cv.shape[2]), row),
            pl.BlockSpec((None, tm, d), row), _const_spec(w_a.shape), _const_spec(w_c.shape),
            vec, _const_spec((1, d)), vec, vec, _const_spec((e, d)),
        ],
        out_specs=out_specs,
        out_shape=out_shape,
        compiler_params=_params("arbitrary", "arbitrary"),
        name="outproj",
    )(attn, cv, x, w_a, w_c, g1, g2n, sh2, sc2, rwt)


def _pool_kernel(xc_ref, xp_ref, xn_ref, g1n_ref, sh1_ref, sc1_ref, pw_ref, pb_ref, ps_ref, g1_ref,
                 g2n_ref, sh2_ref, sc2_ref, rwt_ref, x1_ref, h2p_ref, lgt_ref, win_ref, *, tm, n_tiles, seq):
    i = pl.program_id(1)
    d = xc_ref.shape[1]
    gc = d // len(POOL_WINDOWS)
    norm = lambda x: _adaln(x, g1n_ref[...], sh1_ref[...], sc1_ref[...])
    win_ref[HALO:HALO + tm, :] = norm(xc_ref[...])
    win_ref[0:HALO, :] = jnp.where(i > 0, norm(xp_ref[...]), 0.0)
    win_ref[HALO + tm:2 * HALO + tm, :] = jnp.where(i < n_tiles - 1, norm(xn_ref[...]), 0.0)
    t = i * tm + lax.broadcasted_iota(jnp.int32, (tm, 1), 0)
    n = tm + 2 * SUBLANES
    ahead = lambda a, k: pltpu.roll(a, n - k, 0)
    ys = []
    for gi, w in enumerate(POOL_WINDOWS):
        assert w in (2, 4, 8, 16)
        cs = slice(gi * gc, (gi + 1) * gc)
        run = win_ref[HALO - SUBLANES:HALO + tm + SUBLANES, cs]
        span = 1
        while 2 * span < w:
            run = run + ahead(run, span)
            span *= 2
        first = run[0:tm, :] if span == SUBLANES else ahead(run, SUBLANES - span)[0:tm, :]
        tot = first + run[SUBLANES:SUBLANES + tm, :]
        lo = jnp.clip(t - w // 2, 0, seq)
        hi = jnp.clip(t + w - w // 2, 0, seq)
        p = tot / (hi - lo).astype(F32) - win_ref[HALO:HALO + tm, cs]
        ys.append(jnp.dot(p.astype(BF16), pw_ref[gi], preferred_element_type=F32))
    y = (jnp.concatenate(ys, axis=-1) + pb_ref[...]) * ps_ref[...]
    _finish_tile(xc_ref[...] + g1_ref[...] * y, g2n_ref, sh2_ref, sc2_ref, rwt_ref, x1_ref, h2p_ref, lgt_ref)


def _pool_layer(x, g1n, sh1, sc1, pool_w, pool_b, pool_scale, g1, g2n, sh2, sc2, rwt):
    b, s, d = x.shape
    e = rwt.shape[0]
    tm = _tile(s, 256)
    n_tiles = s // tm
    hb = tm // HALO
    n_hb = s // HALO
    cur = lambda bi, i: (bi, i, 0)
    prev = lambda bi, i: (bi, jnp.maximum(i * hb - 1, 0), 0)
    nxt = lambda bi, i: (bi, jnp.minimum((i + 1) * hb, n_hb - 1), 0)
    vec = pl.BlockSpec((None, 1, d), lambda bi, i: (bi, 0, 0))
    cvec = _const_spec((1, d))
    out_specs, out_shape = _finish_specs(b, s, d, e, tm)
    return pl.pallas_call(
        functools.partial(_pool_kernel, tm=tm, n_tiles=n_tiles, seq=s),
        grid=(b, n_tiles),
        in_specs=[
            pl.BlockSpec((None, tm, d), cur), pl.BlockSpec((None, HALO, d), prev), pl.BlockSpec((None, HALO, d), nxt),
            cvec, vec, vec, _const_spec(pool_w.shape), cvec, cvec, vec, cvec, vec, vec, _const_spec((e, d)),
        ],
        out_specs=out_specs,
        out_shape=out_shape,
        scratch_shapes=[pltpu.VMEM((tm + 2 * HALO, d), F32)],
        compiler_params=_params("arbitrary", "arbitrary"),
        name="pool_mixer",
    )(x, x, x, g1n, sh1, sc1, pool_w, pool_b, pool_scale, g1, g2n, sh2, sc2, rwt)


def _route_kernel(lgt_ref, bias_ref, tri_ref, eid_ref, pos_ref, wts_ref, cnt_ref, carry_ref):
    i = pl.program_id(0)
    e, tn = lgt_ref.shape
    eg = e // N_EXPERT_GROUPS

    @pl.when(i == 0)
    def _():
        carry_ref[...] = jnp.zeros_like(carry_ref)

    scores = _sigmoid(lgt_ref[...])
    sel = scores + bias_ref[...]
    neg = jnp.float32(-jnp.inf)
    sub = lax.broadcasted_iota(jnp.int32, (eg, tn), 0)
    group_scores = []
    for g in range(N_EXPERT_GROUPS):
        blk = sel[g * eg:(g + 1) * eg, :]
        m1 = jnp.max(blk, axis=0, keepdims=True)
        first = jnp.min(jnp.where(blk == m1, sub, eg), axis=0, keepdims=True)
        m2 = jnp.max(jnp.where(sub == first, neg, blk), axis=0, keepdims=True)
        group_scores.append(m1 + m2)
    masked = []
    for g in range(N_EXPERT_GROUPS):
        rank = jnp.zeros((1, tn), jnp.int32)
        for g2 in range(N_EXPERT_GROUPS):
            if g2 == g:
                continue
            ahead = (group_scores[g2] >= group_scores[g]) if g2 < g else (group_scores[g2] > group_scores[g])
            rank = rank + ahead.astype(jnp.int32)
        masked.append(jnp.where(rank < TOPK_GROUPS, sel[g * eg:(g + 1) * eg, :], neg))
    selm = jnp.concatenate(masked, axis=0)

    eidx = lax.broadcasted_iota(jnp.int32, (e, tn), 0)
    work = selm
    picks = []
    for _ in range(TOP_K):
        mx = jnp.max(work, axis=0, keepdims=True)
        pick = jnp.min(jnp.where(work == mx, eidx, e), axis=0, keepdims=True)
        picks.append(pick)
        work = jnp.where(eidx == pick, neg, work)
    chosen = functools.reduce(jnp.logical_or, [eidx == p for p in picks])

    m = chosen.astype(BF16)
    pos = jnp.dot(m, tri_ref[...], preferred_element_type=F32) + carry_ref[...]
    carry_ref[...] = carry_ref[...] + jnp.sum(chosen.astype(F32), axis=1, keepdims=True)
    cnt_ref[...] = carry_ref[...]

    wsum = jnp.sum(jnp.where(chosen, scores, 0.0), axis=0, keepdims=True)
    for j, pick in enumerate(picks):
        hit = eidx == pick
        eid_ref[j:j + 1, :] = pick
        pos_ref[j:j + 1, :] = jnp.sum(jnp.where(hit, pos, 0.0), axis=0, keepdims=True).astype(jnp.int32)
        wj = jnp.sum(jnp.where(hit, scores, 0.0), axis=0, keepdims=True)
        wts_ref[j:j + 1, :] = wj / wsum * ROUTED_SCALE


def _route(lgt, router_bias):
    e, t = lgt.shape
    tn = _tile(t, 512)
    tri = (lax.broadcasted_iota(jnp.int32, (tn, tn), 0) < lax.broadcasted_iota(jnp.int32, (tn, tn), 1)).astype(BF16)
    col = lambda i: (0, i)
    return pl.pallas_call(
        _route_kernel,
        grid=(t // tn,),
        in_specs=[pl.BlockSpec((e, tn), col), _const_spec((e, 1)), _const_spec((tn, tn))],
        out_specs=[pl.BlockSpec((TOP_K, tn), col), pl.BlockSpec((TOP_K, tn), col), pl.BlockSpec((TOP_K, tn), col),
                   pl.BlockSpec((e, 1), lambda i: (0, 0))],
        out_shape=[jax.ShapeDtypeStruct((TOP_K, t), jnp.int32), jax.ShapeDtypeStruct((TOP_K, t), jnp.int32),
                   jax.ShapeDtypeStruct((TOP_K, t), F32), jax.ShapeDtypeStruct((e, 1), F32)],
        scratch_shapes=[pltpu.VMEM((e, 1), F32)],
        compiler_params=_params("arbitrary"),
        name="route",
    )(lgt, router_bias.reshape(e, 1), tri)


def _dest_kernel(poff_ref, eid_ref, pos_ref, dest_ref, *, sub):
    eid = eid_ref[...]
    acc = pos_ref[...]
    for e in range(poff_ref.shape[0]):
        acc = acc + jnp.where(eid == e, poff_ref[e], 0)
    dest_ref[...] = acc * sub


def _dest_rows(poff, eid, pos, sub):
    k, t = eid.shape
    tn = _tile(t, 2048)
    col = lambda i, *_: (0, i)
    return pl.pallas_call(
        functools.partial(_dest_kernel, sub=sub),
        grid_spec=pltpu.PrefetchScalarGridSpec(
            num_scalar_prefetch=1, grid=(t // tn,),
            in_specs=[pl.BlockSpec((k, tn), col), pl.BlockSpec((k, tn), col)],
            out_specs=pl.BlockSpec((k, tn), col)),
        out_shape=jax.ShapeDtypeStruct((k, t), jnp.int32),
        compiler_params=_params("arbitrary"),
        name="dest_rows",
    )(poff, eid, pos)


def _dispatch_kernel(pend_ref, pcnt_ref, dest_ref, h_ref, xs_ref, zero_ref, sem, zsem, *, tt, rows, sub):
    i = pl.program_id(0)
    n_exp = pend_ref.shape[0]

    def zero_copy(e):
        start = pl.multiple_of((pend_ref[e] - rows) * sub, rows * sub)
        return pltpu.make_async_copy(zero_ref, xs_ref.at[pl.ds(start, rows * sub)], zsem)

    @pl.when(i == 0)
    def _():
        zero_ref[...] = jnp.zeros_like(zero_ref)

        def start_zero(e, c):
            @pl.when(pcnt_ref[e] > 0)
            def _():
                zero_copy(e).start()
            return c

        def wait_zero(e, c):
            @pl.when(pcnt_ref[e] > 0)
            def _():
                zero_copy(e).wait()
            return c

        lax.fori_loop(0, n_exp, start_zero, 0)
        lax.fori_loop(0, n_exp, wait_zero, 0)

    def row_copy(t, j):
        src = h_ref.at[pl.ds(pl.multiple_of(t * sub, sub), sub)]
        dst = xs_ref.at[pl.ds(pl.multiple_of(dest_ref[j, t], sub), sub)]
        return pltpu.make_async_copy(src, dst, sem)

    def issue(t, c):
        for j in range(TOP_K):
            row_copy(t, j).start(priority=j % 2)
        return c

    def drain(t, c):
        for j in range(TOP_K):
            row_copy(t, j).wait()
        return c

    lax.fori_loop(0, tt, issue, 0)
    lax.fori_loop(0, tt, drain, 0)


def _dispatch(h2p, dest, pend, pcnt, n_rows, rows, sub):
    t = h2p.shape[0] // sub
    tt = _tile(t, 256)
    return pl.pallas_call(
        functools.partial(_dispatch_kernel, tt=tt, rows=rows, sub=sub),
        grid_spec=pltpu.PrefetchScalarGridSpec(
            num_scalar_prefetch=2, grid=(t // tt,),
            in_specs=[pl.BlockSpec((TOP_K, tt), lambda i, *_: (0, i), memory_space=pltpu.SMEM),
                      pl.BlockSpec((tt * sub, LANES), lambda i, *_: (i, 0))],
            out_specs=pl.BlockSpec(memory_space=pl.ANY),
            scratch_shapes=[pltpu.VMEM((rows * sub, LANES), jnp.uint32), pltpu.SemaphoreType.DMA,
                            pltpu.SemaphoreType.DMA]),
        out_shape=jax.ShapeDtypeStruct((n_rows * sub, LANES), jnp.uint32),
        compiler_params=_params("arbitrary"),
        name="dispatch",
    )(pend, pcnt, dest, h2p)


def _expert_kernel(blk_e_ref, nreal_ref, first_ref, slot_ref, next_e_ref, has_next_ref, x_ref, wg_hbm, wu_hbm, wd_hbm,
                   y_ref, wg32_ref, wu32_ref, wd32_ref, wgb_ref, wub_ref, wdb_ref, wsem, *, rows, sub, layer):
    b = pl.program_id(0)

    def weight_copies(e, slot):
        return [pltpu.make_async_copy(src.at[layer, e], dst.at[slot], wsem.at[slot])
                for src, dst in ((wg_hbm, wg32_ref), (wu_hbm, wu32_ref), (wd_hbm, wd32_ref))]

    @pl.when(first_ref[b] == 1)
    def _():
        e = blk_e_ref[b]
        slot = slot_ref[b]

        @pl.when(b == 0)
        def _():
            for cp in weight_copies(e, slot):
                cp.start()

        @pl.when(has_next_ref[b] == 1)
        def _():
            for cp in weight_copies(next_e_ref[b], 1 - slot):
                cp.start()

        for cp in weight_copies(e, slot):
            cp.wait()
        wgb_ref[...] = wg32_ref[slot].astype(BF16)
        wub_ref[...] = wu32_ref[slot].astype(BF16)
        wdb_ref[...] = wd32_ref[slot].astype(BF16)

    @pl.when(b < nreal_ref[0])
    def _():
        x = jnp.concatenate([p.astype(BF16) for p in _load_token_tiles(x_ref, 0, rows, sub)], axis=1)
        g = jnp.dot(x, wgb_ref[...], preferred_element_type=F32)
        u = jnp.dot(x, wub_ref[...], preferred_element_type=F32)
        a = (g * _sigmoid(g) * u).astype(BF16)
        _store_token_tiles(jnp.dot(a, wdb_ref[...], preferred_element_type=F32), y_ref)


def _experts(xs, blk_e, n_real, pend, w_gate, w_up, w_down, layer, rows, sub):
    d, ff = w_gate.shape[2:]
    nb = xs.shape[0] // (rows * sub)
    blocks = jnp.arange(nb, dtype=jnp.int32)
    valid = blocks < n_real[0]
    prev_e = jnp.concatenate([jnp.full((1,), -1, jnp.int32), blk_e[:-1]])
    first = jnp.logical_and(valid, blk_e != prev_e)
    slot = (jnp.cumsum(first.astype(jnp.int32)) - 1) % 2
    next_blk = pend[blk_e] // rows
    has_next = jnp.logical_and(first, next_blk < n_real[0])
    next_e = blk_e[jnp.minimum(next_blk, nb - 1)]
    i32 = lambda a: a.astype(jnp.int32)

    xrow = lambda b, be, nr, *_: (jnp.minimum(b, jnp.maximum(nr[0] - 1, 0)), 0)
    hbm = pl.BlockSpec(memory_space=pl.ANY)
    return pl.pallas_call(
        functools.partial(_expert_kernel, rows=rows, sub=sub, layer=layer),
        grid_spec=pltpu.PrefetchScalarGridSpec(
            num_scalar_prefetch=6, grid=(nb,),
            in_specs=[pl.BlockSpec((rows * sub, LANES), xrow), hbm, hbm, hbm],
            out_specs=pl.BlockSpec((rows * sub, LANES), xrow),
            scratch_shapes=[pltpu.VMEM((2, d, ff), F32), pltpu.VMEM((2, d, ff), F32), pltpu.VMEM((2, ff, d), F32),
                            pltpu.VMEM((d, ff), BF16), pltpu.VMEM((d, ff), BF16), pltpu.VMEM((ff, d), BF16),
                            pltpu.SemaphoreType.DMA((2,))]),
        out_shape=jax.ShapeDtypeStruct(xs.shape, jnp.uint32),
        compiler_params=_params("arbitrary"),
        name="experts",
    )(blk_e, n_real, i32(first), i32(slot), i32(next_e), i32(has_next), xs, w_gate, w_up, w_down)


def _combine_kernel(dest_ref, dnext_ref, ys_ref, wt_ref, x1_ref, h_ref, sg_ref, su_ref, sd_ref, g2_ref, o_ref,
                    gbuf_ref, sems, *, tt, sub):
    i = pl.program_id(0)
    slot_rows = TOP_K * tt * sub
    t_chunk = tt // TOP_K

    def gather(dref, col, slot, t, j):
        src = ys_ref.at[pl.ds(pl.multiple_of(dref[j, col + t], sub), sub)]
        dst = gbuf_ref.at[pl.ds(slot * slot_rows + (j * tt + t) * sub, sub)]
        return pltpu.make_async_copy(src, dst, sems.at[slot])

    def wait_tile(slot):
        def drain(t, c):
            for j in range(TOP_K):
                gather(dest_ref, 0, slot, 0, j).wait()
            return c
        lax.fori_loop(0, tt, drain, 0)

    @pl.when(i == 0)
    def _():
        def issue(t, c):
            for j in range(TOP_K):
                gather(dest_ref, 0, 0, t, j).start(priority=j % 2)
            return c
        lax.fori_loop(0, tt, issue, 0)

    def tile(slot, row0, dref, col):
        rows = slice(row0, row0 + tt)
        h = jnp.concatenate([p.astype(BF16) for p in _load_token_tiles(h_ref, row0 * sub, tt, sub)], axis=1)
        g = jnp.dot(h, sg_ref[...], preferred_element_type=F32)
        u = jnp.dot(h, su_ref[...], preferred_element_type=F32)
        shared = jnp.dot((g * _sigmoid(g) * u).astype(BF16), sd_ref[...], preferred_element_type=F32)
        wt = wt_ref[rows, :]
        acc = [jnp.zeros((tt, LANES), F32) for _ in range(2 * sub)]
        for j in range(TOP_K):
            for t in range(j * t_chunk, (j + 1) * t_chunk):
                for jj in range(TOP_K):
                    gather(dref, col, 1 - slot, t, jj).start(priority=jj % 2)
            wj = jnp.broadcast_to(wt[:, j:j + 1], (tt, LANES))
            parts = _load_token_tiles(gbuf_ref, slot * slot_rows + j * tt * sub, tt, sub)
            acc = [a + p * wj for a, p in zip(acc, parts)]
        routed = jnp.concatenate(acc, axis=1)
        o_ref[rows, :] = x1_ref[rows, :] + g2_ref[...] * (routed + shared)

    wait_tile(0)
    tile(0, 0, dest_ref, tt)
    wait_tile(1)
    tile(1, tt, dnext_ref, 0)

    @pl.when(i == pl.num_programs(0) - 1)
    def _():
        wait_tile(0)


def _combine(ys, dest, wts_t, x1, h2p, s_gate, s_up, s_down, g2, seq, sub):
    t, d = x1.shape
    sf = s_gate.shape[1]
    tt = _tile(seq // 2, 128)
    assert tt % TOP_K == 0
    n_steps = t // (2 * tt)
    row = lambda i: (i, 0)
    return pl.pallas_call(
        functools.partial(_combine_kernel, tt=tt, sub=sub),
        grid=(n_steps,),
        in_specs=[
            pl.BlockSpec((TOP_K, 2 * tt), lambda i: (0, i), memory_space=pltpu.SMEM),
            pl.BlockSpec((TOP_K, tt), lambda i: (0, jnp.minimum(2 * i + 2, 2 * n_steps - 2)),
                         memory_space=pltpu.SMEM),
            pl.BlockSpec(memory_space=pl.ANY),
            pl.BlockSpec((2 * tt, TOP_K), row), pl.BlockSpec((2 * tt, d), row),
            pl.BlockSpec((2 * tt * sub, LANES), row),
            _const_spec((d, sf)), _const_spec((d, sf)), _const_spec((sf, d)),
            pl.BlockSpec((None, 1, d), lambda i: ((i * 2 * tt) // seq, 0, 0)),
        ],
        out_specs=pl.BlockSpec((2 * tt, d), row),
        out_shape=jax.ShapeDtypeStruct((t, d), F32),
        scratch_shapes=[pltpu.VMEM((2 * TOP_K * tt * sub, LANES), jnp.uint32), pltpu.SemaphoreType.DMA((2,))],
        compiler_params=_params("arbitrary"),
        name="combine",
    )(dest, dest, ys, wts_t, x1, h2p, s_gate, s_up, s_down, g2)


def _moe(x1, h2p, lgt, g2, router_bias, w_gate, w_up, w_down, layer, s_gate, s_up, s_down):
    b, s, d = x1.shape
    t = b * s
    e = lgt.shape[0]
    sub = _token_sublanes(d)
    rows = _tile(t, MOE_BLOCK_ROWS)
    eid, pos, wts, cnt = _route(lgt, router_bias)
    counts = cnt[:, 0].astype(jnp.int32)
    pcnt = (counts + rows - 1) // rows * rows
    pend = jnp.cumsum(pcnt)
    poff = pend - pcnt
    nb = (t * TOP_K + e * (rows - 1)) // rows
    n_real = (pend[-1] // rows).astype(jnp.int32).reshape(1)
    starts = jnp.arange(nb, dtype=jnp.int32) * rows
    blk_e = jnp.minimum(jnp.sum((pend[None, :] <= starts[:, None]).astype(jnp.int32), axis=1), e - 1)
    dest = _dest_rows(poff.astype(jnp.int32), eid, pos, sub)
    xs = _dispatch(h2p, dest, pend.astype(jnp.int32), pcnt.astype(jnp.int32), nb * rows, rows, sub)
    ys = _experts(xs, blk_e, n_real, pend.astype(jnp.int32), w_gate, w_up, w_down, layer, rows, sub)
    out = _combine(ys, dest, wts.T, x1.reshape(t, d), h2p, s_gate.astype(BF16), s_up.astype(BF16),
                   s_down.astype(BF16), g2, s, sub)
    return out.reshape(b, s, d)


def _rope_tables(n_tok):
    axis_dim = HEAD_DIM // 2
    rows = n_tok // GRID_W
    r, col = jnp.meshgrid(jnp.arange(rows), jnp.arange(GRID_W), indexing="ij")
    pos = jnp.stack([r.reshape(-1), col.reshape(-1)], axis=-1).astype(F32)
    inv = ROPE_THETA ** (-jnp.arange(0, axis_dim, 2, dtype=F32) / axis_dim)
    ang = pos[:, :, None] * inv
    ang = jnp.broadcast_to(ang[:, :, None, :], (n_tok, 2, 2, axis_dim // 2)).reshape(n_tok, HEAD_DIM)
    sign = jnp.where((jnp.arange(HEAD_DIM) % axis_dim) < axis_dim // 2, -1.0, 1.0).astype(F32)
    return jnp.cos(ang), jnp.sin(ang) * sign


def kernel(x, c, ctx, c_ctx, w_mod, b_mod, norm1_g, norm2_g, mix_w_in, q_norm_g, k_norm_g, conv_w, conv_b,
           conv_norm_g, conv_norm_b, mix_w_out, pool_w, pool_b, pool_scale, router_w, router_bias,
           moe_w_gate, moe_w_up, moe_w_down, shared_w_gate, shared_w_up, shared_w_down):
    b, s, d = x.shape
    depth = w_mod.shape[0]
    assert depth == 2, "layer schedule below is written for an attention layer followed by a pooling layer"
    kv_w = KV_HEADS * HEAD_DIM
    q_w = mix_w_in.shape[2] - 2 * kv_w - d

    pad = (-(b + 1)) % SUBLANES
    cc = jnp.concatenate([c, c_ctx[None, :], jnp.zeros((pad, d), F32)], axis=0)
    mods = _modulation(cc, w_mod, b_mod)

    def mod(layer, k, rows=slice(0, b)):
        return mods[layer, rows, k * d:(k + 1) * d][:, None, :]

    vec = lambda a: a.reshape(1, -1)
    rwt = jnp.swapaxes(router_w, 1, 2).astype(BF16)
    cos, sin_signed = _rope_tables(s)

    w_in = mix_w_in[0].astype(BF16)
    q, k, v, z = _inproj(x, mod(0, 0), mod(0, 1), vec(norm1_g[0]), w_in, vec(q_norm_g[0]), vec(k_norm_g[0]),
                         cos, sin_signed)
    ctx_row = slice(b, b + 1)
    kc, vc = _ctx_kv(ctx, mods[0, ctx_row, 0:d], mods[0, ctx_row, d:2 * d], vec(norm1_g[0]),
                     w_in[:, q_w:q_w + 2 * kv_w], vec(k_norm_g[0]))
    attn = _attention(q, k, v, kc, vc)
    cv = _conformer_conv(z, conv_w[0], vec(conv_b[0]), vec(conv_norm_g[0]), vec(conv_norm_b[0]))
    w_out = mix_w_out[0].astype(BF16)
    x1, h2p, lgt = _outproj(attn, cv, x, w_out[:q_w], w_out[q_w:], mod(0, 2), vec(norm2_g[0]), mod(0, 3),
                            mod(0, 4), rwt[0])
    x = _moe(x1, h2p, lgt, mod(0, 5), router_bias[0], moe_w_gate, moe_w_up, moe_w_down, 0,
             shared_w_gate[0], shared_w_up[0], shared_w_down[0])

    x1, h2p, lgt = _pool_layer(x, vec(norm1_g[1]), mod(1, 0), mod(1, 1), pool_w[0].astype(BF16), vec(pool_b[0]),
                               vec(pool_scale[0]), mod(1, 2), vec(norm2_g[1]), mod(1, 3), mod(1, 4), rwt[1])
    x = _moe(x1, h2p, lgt, mod(1, 5), router_bias[1], moe_w_gate, moe_w_up, moe_w_down, 1,
             shared_w_gate[1], shared_w_up[1], shared_w_down[1])
    return x
```

```python
import functools

import jax
import jax.numpy as jnp
from jax import lax
from jax.experimental import pallas as pl
from jax.experimental.pallas import tpu as pltpu

HEAD_DIM = 128
KV_HEADS = 2
GRID_W = 64
ROPE_THETA = 10000.0
EPS = 1e-6
CONV_K = 31
CONV_PAD = CONV_K // 2
POOL_WINDOWS = (2, 4, 8, 16)
N_EXPERT_GROUPS = 8
TOPK_GROUPS = 4
TOP_K = 8
ROUTED_SCALE = 2.5
LOG2_E = 1.4426950408889634

LANES = 128
SUBLANES = 8
HALO = 16
assert HALO % SUBLANES == 0 and 0 <= HALO - CONV_PAD <= 1
MOE_BLOCK_ROWS = 512
VMEM_LIMIT_BYTES = 56 * 1024 * 1024

F32 = jnp.float32
BF16 = jnp.bfloat16


def _tile(n, pref):
    t = min(n, pref)
    while n % t:
        t //= 2
    return t


def _params(*sem):
    return pltpu.CompilerParams(dimension_semantics=sem, vmem_limit_bytes=VMEM_LIMIT_BYTES)


def _const_spec(shape):
    nd = len(shape)
    return pl.BlockSpec(shape, lambda *_: (0,) * nd, pipeline_mode=pl.Buffered(1))


def _sigmoid(x):
    return 1.0 / (1.0 + jnp.exp(-x))


def _adaln(x, g, shift, scale):
    ms = jnp.mean(x * x, axis=-1, keepdims=True)
    return (x * lax.rsqrt(ms + EPS) * g) * (1.0 + scale) + shift


def _mod_kernel(c_ref, w_ref, b_ref, o_ref):
    c = c_ref[...]
    a = c * _sigmoid(c)
    o_ref[...] = jnp.dot(a.astype(BF16), w_ref[...].astype(BF16), preferred_element_type=F32) + b_ref[...]


def _modulation(cc, w_mod, b_mod):
    n_layers, d, n = w_mod.shape
    r = cc.shape[0]
    tn = _tile(n, 1024)
    return pl.pallas_call(
        _mod_kernel,
        grid=(n_layers, n // tn),
        in_specs=[
            pl.BlockSpec((r, d), lambda l, j: (0, 0)),
            pl.BlockSpec((None, d, tn), lambda l, j: (l, 0, j)),
            pl.BlockSpec((None, 1, tn), lambda l, j: (l, 0, j)),
        ],
        out_specs=pl.BlockSpec((None, r, tn), lambda l, j: (l, 0, j)),
        out_shape=jax.ShapeDtypeStruct((n_layers, r, n), F32),
        compiler_params=_params("arbitrary", "arbitrary"),
        name="modulation",
    )(cc, w_mod, b_mod.reshape(n_layers, 1, n))


def _head_norm(x, g):
    ms = jnp.mean(x * x, axis=-1, keepdims=True)
    return x * lax.rsqrt(ms + EPS) * g


def _rope(x, cos, sin_signed, first_half):
    up = pltpu.roll(x, HEAD_DIM - HEAD_DIM // 4, 1)
    dn = pltpu.roll(x, HEAD_DIM // 4, 1)
    return x * cos + jnp.where(first_half, up, dn) * sin_signed


def _inproj_kernel(x_ref, sh_ref, sc_ref, g_ref, w_ref, qg_ref, kg_ref, cos_ref, sin_ref,
                   q_ref, k_ref, v_ref, z_ref, *, n_heads, conv_c):
    h = _adaln(x_ref[...], g_ref[...], sh_ref[...], sc_ref[...]).astype(BF16)
    acc = jnp.dot(h, w_ref[...], preferred_element_type=F32)
    cos = cos_ref[...]
    sin = sin_ref[...]
    lane = lax.broadcasted_iota(jnp.int32, cos.shape, 1)
    first_half = (lane % (HEAD_DIM // 2)) < (HEAD_DIM // 4)
    q_w = n_heads * HEAD_DIM
    kv_w = KV_HEADS * HEAD_DIM
    q_scale = HEAD_DIM ** -0.5 * LOG2_E
    for hd in range(n_heads):
        sl = slice(hd * HEAD_DIM, (hd + 1) * HEAD_DIM)
        qn = _head_norm(acc[:, sl], qg_ref[...])
        q_ref[:, sl] = (_rope(qn, cos, sin, first_half) * q_scale).astype(q_ref.dtype)
    for hd in range(KV_HEADS):
        sl = slice(hd * HEAD_DIM, (hd + 1) * HEAD_DIM)
        kn = _head_norm(acc[:, q_w + hd * HEAD_DIM:q_w + (hd + 1) * HEAD_DIM], kg_ref[...])
        k_ref[:, sl] = _rope(kn, cos, sin, first_half).astype(k_ref.dtype)
    v_ref[...] = acc[:, q_w + kv_w:q_w + 2 * kv_w].astype(v_ref.dtype)
    u0 = q_w + 2 * kv_w
    a = acc[:, u0:u0 + conv_c]
    gate = acc[:, u0 + conv_c:u0 + 2 * conv_c]
    z_ref[...] = a * _sigmoid(gate)


def _inproj(x, sh, sc, g, w_in, q_g, k_g, cos, sin_signed):
    b, s, d = x.shape
    in_w = w_in.shape[1]
    kv_w = KV_HEADS * HEAD_DIM
    conv_c = d // 2
    q_w = in_w - 2 * kv_w - 2 * conv_c
    n_heads = q_w // HEAD_DIM
    tm = _tile(s, 512)
    row = lambda bi, i: (bi, i, 0)
    vec = pl.BlockSpec((None, 1, d), lambda bi, i: (bi, 0, 0))
    tab = pl.BlockSpec((tm, HEAD_DIM), lambda bi, i: (i, 0))
    return pl.pallas_call(
        functools.partial(_inproj_kernel, n_heads=n_heads, conv_c=conv_c),
        grid=(b, s // tm),
        in_specs=[
            pl.BlockSpec((None, tm, d), row), vec, vec, _const_spec((1, d)), _const_spec((d, in_w)),
            _const_spec((1, HEAD_DIM)), _const_spec((1, HEAD_DIM)), tab, tab,
        ],
        out_specs=[
            pl.BlockSpec((None, tm, q_w), row), pl.BlockSpec((None, tm, kv_w), row),
            pl.BlockSpec((None, tm, kv_w), row), pl.BlockSpec((None, tm, conv_c), row),
        ],
        out_shape=[
            jax.ShapeDtypeStruct((b, s, q_w), BF16), jax.ShapeDtypeStruct((b, s, kv_w), BF16),
            jax.ShapeDtypeStruct((b, s, kv_w), BF16), jax.ShapeDtypeStruct((b, s, conv_c), F32),
        ],
        compiler_params=_params("arbitrary", "arbitrary"),
        name="inproj",
    )(x, sh, sc, g, w_in, q_g, k_g, cos, sin_signed)


def _ctx_kv_kernel(x_ref, sh_ref, sc_ref, g_ref, w_ref, kg_ref, k_ref, v_ref):
    h = _adaln(x_ref[...], g_ref[...], sh_ref[...], sc_ref[...]).astype(BF16)
    acc = jnp.dot(h, w_ref[...], preferred_element_type=F32)
    kv_w = KV_HEADS * HEAD_DIM
    for hd in range(KV_HEADS):
        sl = slice(hd * HEAD_DIM, (hd + 1) * HEAD_DIM)
        k_ref[:, sl] = _head_norm(acc[:, sl], kg_ref[...]).astype(k_ref.dtype)
    v_ref[...] = acc[:, kv_w:2 * kv_w].astype(v_ref.dtype)


def _ctx_kv(ctx, sh, sc, g, w_kv, k_g):
    b, lc, d = ctx.shape
    kv_w = KV_HEADS * HEAD_DIM
    row = lambda bi: (bi, 0, 0)
    return pl.pallas_call(
        _ctx_kv_kernel,
        grid=(b,),
        in_specs=[
            pl.BlockSpec((None, lc, d), row), _const_spec((1, d)), _const_spec((1, d)), _const_spec((1, d)),
            _const_spec((d, 2 * kv_w)), _const_spec((1, HEAD_DIM)),
        ],
        out_specs=[pl.BlockSpec((None, lc, kv_w), row), pl.BlockSpec((None, lc, kv_w), row)],
        out_shape=[jax.ShapeDtypeStruct((b, lc, kv_w), BF16), jax.ShapeDtypeStruct((b, lc, kv_w), BF16)],
        compiler_params=_params("arbitrary"),
        name="ctx_kv",
    )(ctx, sh, sc, g, w_kv, k_g)


def _attn_kernel(q_ref, k_ref, v_ref, kc_ref, vc_ref, o_ref, *, group, tk):
    tq = q_ref.shape[0]
    rows = group * tq
    q = jnp.concatenate([q_ref[:, g * HEAD_DIM:(g + 1) * HEAD_DIM] for g in range(group)], axis=0)
    nt = (((1,), (1,)), ((), ()))
    chunks = [(k_ref, v_ref, c * tk, tk) for c in range(k_ref.shape[0] // tk)]
    chunks.append((kc_ref, vc_ref, 0, kc_ref.shape[0]))
    m = jnp.full((rows, 1), -jnp.inf, F32)
    l = jnp.zeros((rows, 1), F32)
    acc = jnp.zeros((rows, HEAD_DIM), F32)
    for kr, vr, start, size in chunks:
        s = lax.dot_general(q, kr[start:start + size, :], nt, preferred_element_type=F32)
        m_new = jnp.maximum(m, jnp.max(s, axis=-1, keepdims=True))
        alpha = jnp.exp2(m - m_new)
        p = jnp.exp2(s - m_new)
        l = alpha * l + jnp.sum(p, axis=-1, keepdims=True)
        acc = alpha * acc + jnp.dot(p.astype(BF16), vr[start:start + size, :], preferred_element_type=F32)
        m = m_new
    o = acc / l
    for g in range(group):
        o_ref[:, g * HEAD_DIM:(g + 1) * HEAD_DIM] = o[g * tq:(g + 1) * tq, :].astype(o_ref.dtype)


def _attention(q, k, v, kc, vc):
    b, s, q_w = q.shape
    lc = kc.shape[1]
    group = q_w // (KV_HEADS * HEAD_DIM)
    tq = _tile(s, 512)
    kv_spec = lambda n: pl.BlockSpec((None, n, HEAD_DIM), lambda bi, h, i: (bi, 0, h))
    q_spec = pl.BlockSpec((None, tq, group * HEAD_DIM), lambda bi, h, i: (bi, i, h))
    return pl.pallas_call(
        functools.partial(_attn_kernel, group=group, tk=_tile(s, 1024)),
        grid=(b, KV_HEADS, s // tq),
        in_specs=[q_spec, kv_spec(s), kv_spec(s), kv_spec(lc), kv_spec(lc)],
        out_specs=q_spec,
        out_shape=jax.ShapeDtypeStruct((b, s, q_w), BF16),
        compiler_params=_params("arbitrary", "arbitrary", "arbitrary"),
        name="attention",
    )(q, k, v, kc, vc)


def _conv_kernel(zc_ref, zp_ref, zn_ref, w_ref, b_ref, g_ref, beta_ref, o_ref, win_ref, conv_ref,
                 *, ts, n_tiles, rb, cb):
    i = pl.program_id(1)
    c = zc_ref.shape[1]
    win_ref[HALO:HALO + ts, :] = zc_ref[...]
    win_ref[0:HALO, :] = jnp.where(i > 0, zp_ref[...], 0.0)
    win_ref[HALO + ts:2 * HALO + ts, :] = jnp.where(i < n_tiles - 1, zn_ref[...], 0.0)
    base = HALO - CONV_PAD
    n_a = -(-CONV_K // SUBLANES)
    qn = rb + SUBLANES
    for r in range(ts // rb):
        for cc in range(c // cb):
            cs = slice(cc * cb, (cc + 1) * cb)
            rows = win_ref[r * rb:r * rb + qn + SUBLANES * (n_a - 1), cs]
            acc = jnp.zeros((rb, cb), F32)
            for rr in range(SUBLANES):
                q = None
                for a in range(n_a):
                    kk = SUBLANES * a + rr
                    if kk < CONV_K:
                        term = rows[SUBLANES * a:SUBLANES * a + qn, :] * w_ref[kk:kk + 1, cs]
                        q = term if q is None else q + term
                acc = acc + pltpu.roll(q, (qn - base - rr) % qn, 0)[0:rb, :]
            conv_ref[r * rb:(r + 1) * rb, cs] = acc + b_ref[:, cs]
    z = conv_ref[...]
    mu = jnp.mean(z, axis=-1, keepdims=True)
    zc = z - mu
    var = jnp.mean(zc * zc, axis=-1, keepdims=True)
    y = zc * lax.rsqrt(var + EPS) * g_ref[...] + beta_ref[...]
    o_ref[...] = (y * _sigmoid(y)).astype(o_ref.dtype)


def _conformer_conv(z, conv_w, conv_b, cn_g, cn_b):
    b, s, c = z.shape
    ts = _tile(s, 256)
    n_tiles = s // ts
    hb = ts // HALO
    n_hb = s // HALO
    cur = lambda bi, i: (bi, i, 0)
    prev = lambda bi, i: (bi, jnp.maximum(i * hb - 1, 0), 0)
    nxt = lambda bi, i: (bi, jnp.minimum((i + 1) * hb, n_hb - 1), 0)
    return pl.pallas_call(
        functools.partial(_conv_kernel, ts=ts, n_tiles=n_tiles, rb=_tile(ts, 64), cb=_tile(c, LANES)),
        grid=(b, n_tiles),
        in_specs=[
            pl.BlockSpec((None, ts, c), cur), pl.BlockSpec((None, HALO, c), prev), pl.BlockSpec((None, HALO, c), nxt),
            _const_spec((CONV_K, c)), _const_spec((1, c)), _const_spec((1, c)), _const_spec((1, c)),
        ],
        out_specs=pl.BlockSpec((None, ts, c), cur),
        out_shape=jax.ShapeDtypeStruct((b, s, c), BF16),
        scratch_shapes=[pltpu.VMEM((ts + 2 * HALO, c), F32), pltpu.VMEM((ts, c), F32)],
        compiler_params=_params("arbitrary", "arbitrary"),
        name="conformer_conv",
    )(z, z, z, conv_w, conv_b, cn_g, cn_b)


def _token_sublanes(d):
    assert d % (2 * LANES) == 0
    return d // (2 * LANES)


def _pack_pair(lo, hi):
    lo_bits = lax.bitcast_convert_type(lo.astype(BF16).astype(F32), jnp.uint32) >> 16
    hi_bits = lax.bitcast_convert_type(hi.astype(BF16).astype(F32), jnp.uint32) & jnp.uint32(0xFFFF0000)
    return lo_bits | hi_bits


def _unpack_pair(w):
    lo = lax.bitcast_convert_type(w << 16, F32)
    hi = lax.bitcast_convert_type(w & jnp.uint32(0xFFFF0000), F32)
    return lo, hi


def _store_token_tiles(v, ref, start=0):
    m, d = v.shape
    sub = _token_sublanes(d)
    for s in range(sub):
        lo = v[:, 2 * s * LANES:(2 * s + 1) * LANES]
        hi = v[:, (2 * s + 1) * LANES:(2 * s + 2) * LANES]
        ref[pl.ds(start + s, m, stride=sub), :] = _pack_pair(lo, hi)


def _load_token_tiles(ref, start, m, sub):
    parts = []
    for s in range(sub):
        parts.extend(_unpack_pair(ref[pl.ds(start + s, m, stride=sub), :]))
    return parts


def _finish_tile(xnew, g2n_ref, sh2_ref, sc2_ref, rwt_ref, x1_ref, h2p_ref, lgt_ref):
    x1_ref[...] = xnew
    h2 = _adaln(xnew, g2n_ref[...], sh2_ref[...], sc2_ref[...])
    _store_token_tiles(h2, h2p_ref)
    lgt_ref[...] = lax.dot_general(rwt_ref[...], h2.astype(BF16), (((1,), (1,)), ((), ())),
                                   preferred_element_type=F32)


def _outproj_kernel(a_ref, c_ref, x_ref, wa_ref, wc_ref, g1_ref, g2n_ref, sh2_ref, sc2_ref, rwt_ref,
                    x1_ref, h2p_ref, lgt_ref):
    y = jnp.dot(a_ref[...], wa_ref[...], preferred_element_type=F32)
    y = y + jnp.dot(c_ref[...], wc_ref[...], preferred_element_type=F32)
    _finish_tile(x_ref[...] + g1_ref[...] * y, g2n_ref, sh2_ref, sc2_ref, rwt_ref, x1_ref, h2p_ref, lgt_ref)


def _finish_specs(b, s, d, e, tm):
    nt = s // tm
    sub = _token_sublanes(d)
    row = lambda bi, i: (bi, i, 0)
    flat = lambda bi, i: (bi * nt + i, 0)
    out_specs = [pl.BlockSpec((None, tm, d), row), pl.BlockSpec((tm * sub, LANES), flat),
                 pl.BlockSpec((e, tm), lambda bi, i: (0, bi * nt + i))]
    out_shape = [jax.ShapeDtypeStruct((b, s, d), F32), jax.ShapeDtypeStruct((b * s * sub, LANES), jnp.uint32),
                 jax.ShapeDtypeStruct((e, b * s), F32)]
    return out_specs, out_shape


def _outproj(attn, cv, x, w_a, w_c, g1, g2n, sh2, sc2, rwt):
    b, s, d = x.shape
    e = rwt.shape[0]
    tm = _tile(s, 512)
    row = lambda bi, i: (bi, i, 0)
    vec = pl.BlockSpec((None, 1, d), lambda bi, i: (bi, 0, 0))
    out_specs, out_shape = _finish_specs(b, s, d, e, tm)
    return pl.pallas_call(
        _outproj_kernel,
        grid=(b, s // tm),
        in_specs=[
            pl.BlockSpec((None, tm, attn.shape[2]), row), pl.BlockSpec((None, tm, cv.shape[2]), row),
            pl.BlockSpec((None, tm, d), row), _const_spec(w_a.shape), _const_spec(w_c.shape),
            vec, _const_spec((1, d)), vec, vec, _const_spec((e, d)),
        ],
        out_specs=out_specs,
        out_shape=out_shape,
        compiler_params=_params("arbitrary", "arbitrary"),
        name="outproj",
    )(attn, cv, x, w_a, w_c, g1, g2n, sh2, sc2, rwt)


def _pool_kernel(xc_ref, xp_ref, xn_ref, g1n_ref, sh1_ref, sc1_ref, pw_ref, pb_ref, ps_ref, g1_ref,
                 g2n_ref, sh2_ref, sc2_ref, rwt_ref, x1_ref, h2p_ref, lgt_ref, win_ref, *, tm, n_tiles, seq):
    i = pl.program_id(1)
    d = xc_ref.shape[1]
    gc = d // len(POOL_WINDOWS)
    norm = lambda x: _adaln(x, g1n_ref[...], sh1_ref[...], sc1_ref[...])
    win_ref[HALO:HALO + tm, :] = norm(xc_ref[...])
    win_ref[0:HALO, :] = jnp.where(i > 0, norm(xp_ref[...]), 0.0)
    win_ref[HALO + tm:2 * HALO + tm, :] = jnp.where(i < n_tiles - 1, norm(xn_ref[...]), 0.0)
    t = i * tm + lax.broadcasted_iota(jnp.int32, (tm, 1), 0)
    n = tm + 2 * SUBLANES
    ahead = lambda a, k: pltpu.roll(a, n - k, 0)
    ys = []
    for gi, w in enumerate(POOL_WINDOWS):
        assert w in (2, 4, 8, 16)
        cs = slice(gi * gc, (gi + 1) * gc)
        run = win_ref[HALO - SUBLANES:HALO + tm + SUBLANES, cs]
        span = 1
        while 2 * span < w:
            run = run + ahead(run, span)
            span *= 2
        first = run[0:tm, :] if span == SUBLANES else ahead(run, SUBLANES - span)[0:tm, :]
        tot = first + run[SUBLANES:SUBLANES + tm, :]
        lo = jnp.clip(t - w // 2, 0, seq)
        hi = jnp.clip(t + w - w // 2, 0, seq)
        p = tot / (hi - lo).astype(F32) - win_ref[HALO:HALO + tm, cs]
        ys.append(jnp.dot(p.astype(BF16), pw_ref[gi], preferred_element_type=F32))
    y = (jnp.concatenate(ys, axis=-1) + pb_ref[...]) * ps_ref[...]
    _finish_tile(xc_ref[...] + g1_ref[...] * y, g2n_ref, sh2_ref, sc2_ref, rwt_ref, x1_ref, h2p_ref, lgt_ref)


def _pool_layer(x, g1n, sh1, sc1, pool_w, pool_b, pool_scale, g1, g2n, sh2, sc2, rwt):
    b, s, d = x.shape
    e = rwt.shape[0]
    tm = _tile(s, 256)
    n_tiles = s // tm
    hb = tm // HALO
    n_hb = s // HALO
    cur = lambda bi, i: (bi, i, 0)
    prev = lambda bi, i: (bi, jnp.maximum(i * hb - 1, 0), 0)
    nxt = lambda bi, i: (bi, jnp.minimum((i + 1) * hb, n_hb - 1), 0)
    vec = pl.BlockSpec((None, 1, d), lambda bi, i: (bi, 0, 0))
    cvec = _const_spec((1, d))
    out_specs, out_shape = _finish_specs(b, s, d, e, tm)
    return pl.pallas_call(
        functools.partial(_pool_kernel, tm=tm, n_tiles=n_tiles, seq=s),
        grid=(b, n_tiles),
        in_specs=[
            pl.BlockSpec((None, tm, d), cur), pl.BlockSpec((None, HALO, d), prev), pl.BlockSpec((None, HALO, d), nxt),
            cvec, vec, vec, _const_spec(pool_w.shape), cvec, cvec, vec, cvec, vec, vec, _const_spec((e, d)),
        ],
        out_specs=out_specs,
        out_shape=out_shape,
        scratch_shapes=[pltpu.VMEM((tm + 2 * HALO, d), F32)],
        compiler_params=_params("arbitrary", "arbitrary"),
        name="pool_mixer",
    )(x, x, x, g1n, sh1, sc1, pool_w, pool_b, pool_scale, g1, g2n, sh2, sc2, rwt)


def _route_kernel(lgt_ref, bias_ref, tri_ref, eid_ref, pos_ref, wts_ref, cnt_ref, carry_ref):
    i = pl.program_id(0)
    e, tn = lgt_ref.shape
    eg = e // N_EXPERT_GROUPS

    @pl.when(i == 0)
    def _():
        carry_ref[...] = jnp.zeros_like(carry_ref)

    scores = _sigmoid(lgt_ref[...])
    sel = scores + bias_ref[...]
    neg = jnp.float32(-jnp.inf)
    sub = lax.broadcasted_iota(jnp.int32, (eg, tn), 0)
    group_scores = []
    for g in range(N_EXPERT_GROUPS):
        blk = sel[g * eg:(g + 1) * eg, :]
        m1 = jnp.max(blk, axis=0, keepdims=True)
        first = jnp.min(jnp.where(blk == m1, sub, eg), axis=0, keepdims=True)
        m2 = jnp.max(jnp.where(sub == first, neg, blk), axis=0, keepdims=True)
        group_scores.append(m1 + m2)
    masked = []
    for g in range(N_EXPERT_GROUPS):
        rank = jnp.zeros((1, tn), jnp.int32)
        for g2 in range(N_EXPERT_GROUPS):
            if g2 == g:
                continue
            ahead = (group_scores[g2] >= group_scores[g]) if g2 < g else (group_scores[g2] > group_scores[g])
            rank = rank + ahead.astype(jnp.int32)
        masked.append(jnp.where(rank < TOPK_GROUPS, sel[g * eg:(g + 1) * eg, :], neg))
    selm = jnp.concatenate(masked, axis=0)

    eidx = lax.broadcasted_iota(jnp.int32, (e, tn), 0)
    work = selm
    picks = []
    for _ in range(TOP_K):
        mx = jnp.max(work, axis=0, keepdims=True)
        pick = jnp.min(jnp.where(work == mx, eidx, e), axis=0, keepdims=True)
        picks.append(pick)
        work = jnp.where(eidx == pick, neg, work)
    chosen = functools.reduce(jnp.logical_or, [eidx == p for p in picks])

    m = chosen.astype(BF16)
    pos = jnp.dot(m, tri_ref[...], preferred_element_type=F32) + carry_ref[...]
    carry_ref[...] = carry_ref[...] + jnp.sum(chosen.astype(F32), axis=1, keepdims=True)
    cnt_ref[...] = carry_ref[...]

    wsum = jnp.sum(jnp.where(chosen, scores, 0.0), axis=0, keepdims=True)
    for j, pick in enumerate(picks):
        hit = eidx == pick
        eid_ref[j:j + 1, :] = pick
        pos_ref[j:j + 1, :] = jnp.sum(jnp.where(hit, pos, 0.0), axis=0, keepdims=True).astype(jnp.int32)
        wj = jnp.sum(jnp.where(hit, scores, 0.0), axis=0, keepdims=True)
        wts_ref[j:j + 1, :] = wj / wsum * ROUTED_SCALE


def _route(lgt, router_bias):
    e, t = lgt.shape
    tn = _tile(t, 512)
    tri = (lax.broadcasted_iota(jnp.int32, (tn, tn), 0) < lax.broadcasted_iota(jnp.int32, (tn, tn), 1)).astype(BF16)
    col = lambda i: (0, i)
    return pl.pallas_call(
        _route_kernel,
        grid=(t // tn,),
        in_specs=[pl.BlockSpec((e, tn), col), _const_spec((e, 1)), _const_spec((tn, tn))],
        out_specs=[pl.BlockSpec((TOP_K, tn), col), pl.BlockSpec((TOP_K, tn), col), pl.BlockSpec((TOP_K, tn), col),
                   pl.BlockSpec((e, 1), lambda i: (0, 0))],
        out_shape=[jax.ShapeDtypeStruct((TOP_K, t), jnp.int32), jax.ShapeDtypeStruct((TOP_K, t), jnp.int32),
                   jax.ShapeDtypeStruct((TOP_K, t), F32), jax.ShapeDtypeStruct((e, 1), F32)],
        scratch_shapes=[pltpu.VMEM((e, 1), F32)],
        compiler_params=_params("arbitrary"),
        name="route",
    )(lgt, router_bias.reshape(e, 1), tri)


def _dest_kernel(poff_ref, eid_ref, pos_ref, dest_ref, *, sub):
    eid = eid_ref[...]
    acc = pos_ref[...]
    for e in range(poff_ref.shape[0]):
        acc = acc + jnp.where(eid == e, poff_ref[e], 0)
    dest_ref[...] = acc * sub


def _dest_rows(poff, eid, pos, sub):
    k, t = eid.shape
    tn = _tile(t, 2048)
    col = lambda i, *_: (0, i)
    return pl.pallas_call(
        functools.partial(_dest_kernel, sub=sub),
        grid_spec=pltpu.PrefetchScalarGridSpec(
            num_scalar_prefetch=1, grid=(t // tn,),
            in_specs=[pl.BlockSpec((k, tn), col), pl.BlockSpec((k, tn), col)],
            out_specs=pl.BlockSpec((k, tn), col)),
        out_shape=jax.ShapeDtypeStruct((k, t), jnp.int32),
        compiler_params=_params("arbitrary"),
        name="dest_rows",
    )(poff, eid, pos)


def _dispatch_kernel(pend_ref, pcnt_ref, dest_ref, h_ref, xs_ref, zero_ref, sem, zsem, *, tt, rows, sub):
    i = pl.program_id(0)
    n_exp = pend_ref.shape[0]

    def zero_copy(e):
        start = pl.multiple_of((pend_ref[e] - rows) * sub, rows * sub)
        return pltpu.make_async_copy(zero_ref, xs_ref.at[pl.ds(start, rows * sub)], zsem)

    @pl.when(i == 0)
    def _():
        zero_ref[...] = jnp.zeros_like(zero_ref)

        def start_zero(e, c):
            @pl.when(pcnt_ref[e] > 0)
            def _():
                zero_copy(e).start()
            return c

        def wait_zero(e, c):
            @pl.when(pcnt_ref[e] > 0)
            def _():
                zero_copy(e).wait()
            return c

        lax.fori_loop(0, n_exp, start_zero, 0)
        lax.fori_loop(0, n_exp, wait_zero, 0)

    def row_copy(t, j):
        src = h_ref.at[pl.ds(pl.multiple_of(t * sub, sub), sub)]
        dst = xs_ref.at[pl.ds(pl.multiple_of(dest_ref[j, t], sub), sub)]
        return pltpu.make_async_copy(src, dst, sem)

    def issue(t, c):
        for j in range(TOP_K):
            row_copy(t, j).start(priority=j % 2)
        return c

    def drain(t, c):
        for j in range(TOP_K):
            row_copy(t, j).wait()
        return c

    lax.fori_loop(0, tt, issue, 0)
    lax.fori_loop(0, tt, drain, 0)


def _dispatch(h2p, dest, pend, pcnt, n_rows, rows, sub):
    t = h2p.shape[0] // sub
    tt = _tile(t, 256)
    return pl.pallas_call(
        functools.partial(_dispatch_kernel, tt=tt, rows=rows, sub=sub),
        grid_spec=pltpu.PrefetchScalarGridSpec(
            num_scalar_prefetch=2, grid=(t // tt,),
            in_specs=[pl.BlockSpec((TOP_K, tt), lambda i, *_: (0, i), memory_space=pltpu.SMEM),
                      pl.BlockSpec((tt * sub, LANES), lambda i, *_: (i, 0))],
            out_specs=pl.BlockSpec(memory_space=pl.ANY),
            scratch_shapes=[pltpu.VMEM((rows * sub, LANES), jnp.uint32), pltpu.SemaphoreType.DMA,
                            pltpu.SemaphoreType.DMA]),
        out_shape=jax.ShapeDtypeStruct((n_rows * sub, LANES), jnp.uint32),
        compiler_params=_params("arbitrary"),
        name="dispatch",
    )(pend, pcnt, dest, h2p)


def _expert_kernel(blk_e_ref, nreal_ref, first_ref, slot_ref, next_e_ref, has_next_ref, x_ref, wg_hbm, wu_hbm, wd_hbm,
                   y_ref, wg32_ref, wu32_ref, wd32_ref, wgb_ref, wub_ref, wdb_ref, wsem, *, rows, sub, layer):
    b = pl.program_id(0)

    def weight_copies(e, slot):
        return [pltpu.make_async_copy(src.at[layer, e], dst.at[slot], wsem.at[slot])
                for src, dst in ((wg_hbm, wg32_ref), (wu_hbm, wu32_ref), (wd_hbm, wd32_ref))]

    @pl.when(first_ref[b] == 1)
    def _():
        e = blk_e_ref[b]
        slot = slot_ref[b]

        @pl.when(b == 0)
        def _():
            for cp in weight_copies(e, slot):
                cp.start()

        @pl.when(has_next_ref[b] == 1)
        def _():
            for cp in weight_copies(next_e_ref[b], 1 - slot):
                cp.start()

        for cp in weight_copies(e, slot):
            cp.wait()
        wgb_ref[...] = wg32_ref[slot].astype(BF16)
        wub_ref[...] = wu32_ref[slot].astype(BF16)
        wdb_ref[...] = wd32_ref[slot].astype(BF16)

    @pl.when(b < nreal_ref[0])
    def _():
        x = jnp.concatenate([p.astype(BF16) for p in _load_token_tiles(x_ref, 0, rows, sub)], axis=1)
        g = jnp.dot(x, wgb_ref[...], preferred_element_type=F32)
        u = jnp.dot(x, wub_ref[...], preferred_element_type=F32)
        a = (g * _sigmoid(g) * u).astype(BF16)
        _store_token_tiles(jnp.dot(a, wdb_ref[...], preferred_element_type=F32), y_ref)


def _experts(xs, blk_e, n_real, pend, w_gate, w_up, w_down, layer, rows, sub):
    d, ff = w_gate.shape[2:]
    nb = xs.shape[0] // (rows * sub)
    blocks = jnp.arange(nb, dtype=jnp.int32)
    valid = blocks < n_real[0]
    prev_e = jnp.concatenate([jnp.full((1,), -1, jnp.int32), blk_e[:-1]])
    first = jnp.logical_and(valid, blk_e != prev_e)
    slot = (jnp.cumsum(first.astype(jnp.int32)) - 1) % 2
    next_blk = pend[blk_e] // rows
    has_next = jnp.logical_and(first, next_blk < n_real[0])
    next_e = blk_e[jnp.minimum(next_blk, nb - 1)]
    i32 = lambda a: a.astype(jnp.int32)

    xrow = lambda b, be, nr, *_: (jnp.minimum(b, jnp.maximum(nr[0] - 1, 0)), 0)
    hbm = pl.BlockSpec(memory_space=pl.ANY)
    return pl.pallas_call(
        functools.partial(_expert_kernel, rows=rows, sub=sub, layer=layer),
        grid_spec=pltpu.PrefetchScalarGridSpec(
            num_scalar_prefetch=6, grid=(nb,),
            in_specs=[pl.BlockSpec((rows * sub, LANES), xrow), hbm, hbm, hbm],
            out_specs=pl.BlockSpec((rows * sub, LANES), xrow),
            scratch_shapes=[pltpu.VMEM((2, d, ff), F32), pltpu.VMEM((2, d, ff), F32), pltpu.VMEM((2, ff, d), F32),
                            pltpu.VMEM((d, ff), BF16), pltpu.VMEM((d, ff), BF16), pltpu.VMEM((ff, d), BF16),
                            pltpu.SemaphoreType.DMA((2,))]),
        out_shape=jax.ShapeDtypeStruct(xs.shape, jnp.uint32),
        compiler_params=_params("arbitrary"),
        name="experts",
    )(blk_e, n_real, i32(first), i32(slot), i32(next_e), i32(has_next), xs, w_gate, w_up, w_down)


def _combine_kernel(dest_ref, dnext_ref, ys_ref, wt_ref, x1_ref, h_ref, sg_ref, su_ref, sd_ref, g2_ref, o_ref,
                    gbuf_ref, sems, *, tt, sub):
    i = pl.program_id(0)
    slot_rows = TOP_K * tt * sub
    t_chunk = tt // TOP_K

    def gather(dref, col, slot, t, j):
        src = ys_ref.at[pl.ds(pl.multiple_of(dref[j, col + t], sub), sub)]
        dst = gbuf_ref.at[pl.ds(slot * slot_rows + (j * tt + t) * sub, sub)]
        return pltpu.make_async_copy(src, dst, sems.at[slot])

    def wait_tile(slot):
        def drain(t, c):
            for j in range(TOP_K):
                gather(dest_ref, 0, slot, 0, j).wait()
            return c
        lax.fori_loop(0, tt, drain, 0)

    @pl.when(i == 0)
    def _():
        def issue(t, c):
            for j in range(TOP_K):
                gather(dest_ref, 0, 0, t, j).start(priority=j % 2)
            return c
        lax.fori_loop(0, tt, issue, 0)

    def tile(slot, row0, dref, col):
        rows = slice(row0, row0 + tt)
        h = jnp.concatenate([p.astype(BF16) for p in _load_token_tiles(h_ref, row0 * sub, tt, sub)], axis=1)
        g = jnp.dot(h, sg_ref[...], preferred_element_type=F32)
        u = jnp.dot(h, su_ref[...], preferred_element_type=F32)
        shared = jnp.dot((g * _sigmoid(g) * u).astype(BF16), sd_ref[...], preferred_element_type=F32)
        wt = wt_ref[rows, :]
        acc = [jnp.zeros((tt, LANES), F32) for _ in range(2 * sub)]
        for j in range(TOP_K):
            for t in range(j * t_chunk, (j + 1) * t_chunk):
                for jj in range(TOP_K):
                    gather(dref, col, 1 - slot, t, jj).start(priority=jj % 2)
            wj = jnp.broadcast_to(wt[:, j:j + 1], (tt, LANES))
            parts = _load_token_tiles(gbuf_ref, slot * slot_rows + j * tt * sub, tt, sub)
            acc = [a + p * wj for a, p in zip(acc, parts)]
        routed = jnp.concatenate(acc, axis=1)
        o_ref[rows, :] = x1_ref[rows, :] + g2_ref[...] * (routed + shared)

    wait_tile(0)
    tile(0, 0, dest_ref, tt)
    wait_tile(1)
    tile(1, tt, dnext_ref, 0)

    @pl.when(i == pl.num_programs(0) - 1)
    def _():
        wait_tile(0)


def _combine(ys, dest, wts_t, x1, h2p, s_gate, s_up, s_down, g2, seq, sub):
    t, d = x1.shape
    sf = s_gate.shape[1]
    tt = _tile(seq // 2, 128)
    assert tt % TOP_K == 0
    n_steps = t // (2 * tt)
    row = lambda i: (i, 0)
    return pl.pallas_call(
        functools.partial(_combine_kernel, tt=tt, sub=sub),
        grid=(n_steps,),
        in_specs=[
            pl.BlockSpec((TOP_K, 2 * tt), lambda i: (0, i), memory_space=pltpu.SMEM),
            pl.BlockSpec((TOP_K, tt), lambda i: (0, jnp.minimum(2 * i + 2, 2 * n_steps - 2)),
                         memory_space=pltpu.SMEM),
            pl.BlockSpec(memory_space=pl.ANY),
            pl.BlockSpec((2 * tt, TOP_K), row), pl.BlockSpec((2 * tt, d), row),
            pl.BlockSpec((2 * tt * sub, LANES), row),
            _const_spec((d, sf)), _const_spec((d, sf)), _const_spec((sf, d)),
            pl.BlockSpec((None, 1, d), lambda i: ((i * 2 * tt) // seq, 0, 0)),
        ],
        out_specs=pl.BlockSpec((2 * tt, d), row),
        out_shape=jax.ShapeDtypeStruct((t, d), F32),
        scratch_shapes=[pltpu.VMEM((2 * TOP_K * tt * sub, LANES), jnp.uint32), pltpu.SemaphoreType.DMA((2,))],
        compiler_params=_params("arbitrary"),
        name="combine",
    )(dest, dest, ys, wts_t, x1, h2p, s_gate, s_up, s_down, g2)


def _moe(x1, h2p, lgt, g2, router_bias, w_gate, w_up, w_down, layer, s_gate, s_up, s_down):
    b, s, d = x1.shape
    t = b * s
    e = lgt.shape[0]
    sub = _token_sublanes(d)
    rows = _tile(t, MOE_BLOCK_ROWS)
    eid, pos, wts, cnt = _route(lgt, router_bias)
    counts = cnt[:, 0].astype(jnp.int32)
    pcnt = (counts + rows - 1) // rows * rows
    pend = jnp.cumsum(pcnt)
    poff = pend - pcnt
    nb = (t * TOP_K + e * (rows - 1)) // rows
    n_real = (pend[-1] // rows).astype(jnp.int32).reshape(1)
    starts = jnp.arange(nb, dtype=jnp.int32) * rows
    blk_e = jnp.minimum(jnp.sum((pend[None, :] <= starts[:, None]).astype(jnp.int32), axis=1), e - 1)
    dest = _dest_rows(poff.astype(jnp.int32), eid, pos, sub)
    xs = _dispatch(h2p, dest, pend.astype(jnp.int32), pcnt.astype(jnp.int32), nb * rows, rows, sub)
    ys = _experts(xs, blk_e, n_real, pend.astype(jnp.int32), w_gate, w_up, w_down, layer, rows, sub)
    out = _combine(ys, dest, wts.T, x1.reshape(t, d), h2p, s_gate.astype(BF16), s_up.astype(BF16),
                   s_down.astype(BF16), g2, s, sub)
    return out.reshape(b, s, d)


def _rope_tables(n_tok):
    axis_dim = HEAD_DIM // 2
    rows = n_tok // GRID_W
    r, col = jnp.meshgrid(jnp.arange(rows), jnp.arange(GRID_W), indexing="ij")
    pos = jnp.stack([r.reshape(-1), col.reshape(-1)], axis=-1).astype(F32)
    inv = ROPE_THETA ** (-jnp.arange(0, axis_dim, 2, dtype=F32) / axis_dim)
    ang = pos[:, :, None] * inv
    ang = jnp.broadcast_to(ang[:, :, None, :], (n_tok, 2, 2, axis_dim // 2)).reshape(n_tok, HEAD_DIM)
    sign = jnp.where((jnp.arange(HEAD_DIM) % axis_dim) < axis_dim // 2, -1.0, 1.0).astype(F32)
    return jnp.cos(ang), jnp.sin(ang) * sign


def kernel(x, c, ctx, c_ctx, w_mod, b_mod, norm1_g, norm2_g, mix_w_in, q_norm_g, k_norm_g, conv_w, conv_b,
           conv_norm_g, conv_norm_b, mix_w_out, pool_w, pool_b, pool_scale, router_w, router_bias,
           moe_w_gate, moe_w_up, moe_w_down, shared_w_gate, shared_w_up, shared_w_down):
    b, s, d = x.shape
    depth = w_mod.shape[0]
    assert depth == 2, "layer schedule below is written for an attention layer followed by a pooling layer"
    kv_w = KV_HEADS * HEAD_DIM
    q_w = mix_w_in.shape[2] - 2 * kv_w - d

    pad = (-(b + 1)) % SUBLANES
    cc = jnp.concatenate([c, c_ctx[None, :], jnp.zeros((pad, d), F32)], axis=0)
    mods = _modulation(cc, w_mod, b_mod)

    def mod(layer, k, rows=slice(0, b)):
        return mods[layer, rows, k * d:(k + 1) * d][:, None, :]

    vec = lambda a: a.reshape(1, -1)
    rwt = jnp.swapaxes(router_w, 1, 2).astype(BF16)
    cos, sin_signed = _rope_tables(s)

    w_in = mix_w_in[0].astype(BF16)
    q, k, v, z = _inproj(x, mod(0, 0), mod(0, 1), vec(norm1_g[0]), w_in, vec(q_norm_g[0]), vec(k_norm_g[0]),
                         cos, sin_signed)
    ctx_row = slice(b, b + 1)
    kc, vc = _ctx_kv(ctx, mods[0, ctx_row, 0:d], mods[0, ctx_row, d:2 * d], vec(norm1_g[0]),
                     w_in[:, q_w:q_w + 2 * kv_w], vec(k_norm_g[0]))
    attn = _attention(q, k, v, kc, vc)
    cv = _conformer_conv(z, conv_w[0], vec(conv_b[0]), vec(conv_norm_g[0]), vec(conv_norm_b[0]))
    w_out = mix_w_out[0].astype(BF16)
    x1, h2p, lgt = _outproj(attn, cv, x, w_out[:q_w], w_out[q_w:], mod(0, 2), vec(norm2_g[0]), mod(0, 3),
                            mod(0, 4), rwt[0])
    x = _moe(x1, h2p, lgt, mod(0, 5), router_bias[0], moe_w_gate, moe_w_up, moe_w_down, 0,
             shared_w_gate[0], shared_w_up[0], shared_w_down[0])

    x1, h2p, lgt = _pool_layer(x, vec(norm1_g[1]), mod(1, 0), mod(1, 1), pool_w[0].astype(BF16), vec(pool_b[0]),
                               vec(pool_scale[0]), mod(1, 2), vec(norm2_g[1]), mod(1, 3), mod(1, 4), rwt[1])
    x = _moe(x1, h2p, lgt, mod(1, 5), router_bias[1], moe_w_gate, moe_w_up, moe_w_down, 1,
             shared_w_gate[1], shared_w_up[1], shared_w_down[1])
    return x
```

```python
import functools

import jax
import jax.numpy as jnp
from jax import lax
from jax.experimental import pallas as pl
from jax.experimental.pallas import tpu as pltpu

HEAD_DIM = 128
KV_HEADS = 2
GRID_W = 64
ROPE_THETA = 10000.0
EPS = 1e-6
CONV_K = 31
CONV_PAD = CONV_K // 2
POOL_WINDOWS = (2, 4, 8, 16)
N_EXPERT_GROUPS = 8
TOPK_GROUPS = 4
TOP_K = 8
ROUTED_SCALE = 2.5
LOG2_E = 1.4426950408889634

LANES = 128
SUBLANES = 8
HALO = 16
assert HALO % SUBLANES == 0 and 0 <= HALO - CONV_PAD <= 1
MOE_BLOCK_ROWS = 512
VMEM_LIMIT_BYTES = 56 * 1024 * 1024

F32 = jnp.float32
BF16 = jnp.bfloat16


def _tile(n, pref):
    t = min(n, pref)
    while n % t:
        t //= 2
    return t


def _params(*sem):
    return pltpu.CompilerParams(dimension_semantics=sem, vmem_limit_bytes=VMEM_LIMIT_BYTES)


def _const_spec(shape):
    nd = len(shape)
    return pl.BlockSpec(shape, lambda *_: (0,) * nd, pipeline_mode=pl.Buffered(1))


def _sigmoid(x):
    return 1.0 / (1.0 + jnp.exp(-x))


def _adaln(x, g, shift, scale):
    ms = jnp.mean(x * x, axis=-1, keepdims=True)
    return (x * lax.rsqrt(ms + EPS) * g) * (1.0 + scale) + shift


def _mod_kernel(c_ref, w_ref, b_ref, o_ref):
    c = c_ref[...]
    a = c * _sigmoid(c)
    o_ref[...] = jnp.dot(a.astype(BF16), w_ref[...].astype(BF16), preferred_element_type=F32) + b_ref[...]


def _modulation(cc, w_mod, b_mod):
    n_layers, d, n = w_mod.shape
    r = cc.shape[0]
    tn = _tile(n, 1024)
    return pl.pallas_call(
        _mod_kernel,
        grid=(n_layers, n // tn),
        in_specs=[
            pl.BlockSpec((r, d), lambda l, j: (0, 0)),
            pl.BlockSpec((None, d, tn), lambda l, j: (l, 0, j)),
            pl.BlockSpec((None, 1, tn), lambda l, j: (l, 0, j)),
        ],
        out_specs=pl.BlockSpec((None, r, tn), lambda l, j: (l, 0, j)),
        out_shape=jax.ShapeDtypeStruct((n_layers, r, n), F32),
        compiler_params=_params("arbitrary", "arbitrary"),
        name="modulation",
    )(cc, w_mod, b_mod.reshape(n_layers, 1, n))


def _head_norm(x, g):
    ms = jnp.mean(x * x, axis=-1, keepdims=True)
    return x * lax.rsqrt(ms + EPS) * g


def _rope(x, cos, sin_signed, first_half):
    up = pltpu.roll(x, HEAD_DIM - HEAD_DIM // 4, 1)
    dn = pltpu.roll(x, HEAD_DIM // 4, 1)
    return x * cos + jnp.where(first_half, up, dn) * sin_signed


def _inproj_kernel(x_ref, sh_ref, sc_ref, g_ref, w_ref, qg_ref, kg_ref, cos_ref, sin_ref,
                   q_ref, k_ref, v_ref, z_ref, *, n_heads, conv_c):
    h = _adaln(x_ref[...], g_ref[...], sh_ref[...], sc_ref[...]).astype(BF16)
    acc = jnp.dot(h, w_ref[...], preferred_element_type=F32)
    cos = cos_ref[...]
    sin = sin_ref[...]
    lane = lax.broadcasted_iota(jnp.int32, cos.shape, 1)
    first_half = (lane % (HEAD_DIM // 2)) < (HEAD_DIM // 4)
    q_w = n_heads * HEAD_DIM
    kv_w = KV_HEADS * HEAD_DIM
    q_scale = HEAD_DIM ** -0.5 * LOG2_E
    for hd in range(n_heads):
        sl = slice(hd * HEAD_DIM, (hd + 1) * HEAD_DIM)
        qn = _head_norm(acc[:, sl], qg_ref[...])
        q_ref[:, sl] = (_rope(qn, cos, sin, first_half) * q_scale).astype(q_ref.dtype)
    for hd in range(KV_HEADS):
        sl = slice(hd * HEAD_DIM, (hd + 1) * HEAD_DIM)
        kn = _head_norm(acc[:, q_w + hd * HEAD_DIM:q_w + (hd + 1) * HEAD_DIM], kg_ref[...])
        k_ref[:, sl] = _rope(kn, cos, sin, first_half).astype(k_ref.dtype)
    v_ref[...] = acc[:, q_w + kv_w:q_w + 2 * kv_w].astype(v_ref.dtype)
    u0 = q_w + 2 * kv_w
    a = acc[:, u0:u0 + conv_c]
    gate = acc[:, u0 + conv_c:u0 + 2 * conv_c]
    z_ref[...] = a * _sigmoid(gate)


def _inproj(x, sh, sc, g, w_in, q_g, k_g, cos, sin_signed):
    b, s, d = x.shape
    in_w = w_in.shape[1]
    kv_w = KV_HEADS * HEAD_DIM
    conv_c = d // 2
    q_w = in_w - 2 * kv_w - 2 * conv_c
    n_heads = q_w // HEAD_DIM
    tm = _tile(s, 512)
    row = lambda bi, i: (bi, i, 0)
    vec = pl.BlockSpec((None, 1, d), lambda bi, i: (bi, 0, 0))
    tab = pl.BlockSpec((tm, HEAD_DIM), lambda bi, i: (i, 0))
    return pl.pallas_call(
        functools.partial(_inproj_kernel, n_heads=n_heads, conv_c=conv_c),
        grid=(b, s // tm),
        in_specs=[
            pl.BlockSpec((None, tm, d), row), vec, vec, _const_spec((1, d)), _const_spec((d, in_w)),
            _const_spec((1, HEAD_DIM)), _const_spec((1, HEAD_DIM)), tab, tab,
        ],
        out_specs=[
            pl.BlockSpec((None, tm, q_w), row), pl.BlockSpec((None, tm, kv_w), row),
            pl.BlockSpec((None, tm, kv_w), row), pl.BlockSpec((None, tm, conv_c), row),
        ],
        out_shape=[
            jax.ShapeDtypeStruct((b, s, q_w), BF16), jax.ShapeDtypeStruct((b, s, kv_w), BF16),
            jax.ShapeDtypeStruct((b, s, kv_w), BF16), jax.ShapeDtypeStruct((b, s, conv_c), F32),
        ],
        compiler_params=_params("arbitrary", "arbitrary"),
        name="inproj",
    )(x, sh, sc, g, w_in, q_g, k_g, cos, sin_signed)


def _ctx_kv_kernel(x_ref, sh_ref, sc_ref, g_ref, w_ref, kg_ref, k_ref, v_ref):
    h = _adaln(x_ref[...], g_ref[...], sh_ref[...], sc_ref[...]).astype(BF16)
    acc = jnp.dot(h, w_ref[...], preferred_element_type=F32)
    kv_w = KV_HEADS * HEAD_DIM
    for hd in range(KV_HEADS):
        sl = slice(hd * HEAD_DIM, (hd + 1) * HEAD_DIM)
        k_ref[:, sl] = _head_norm(acc[:, sl], kg_ref[...]).astype(k_ref.dtype)
    v_ref[...] = acc[:, kv_w:2 * kv_w].astype(v_ref.dtype)


def _ctx_kv(ctx, sh, sc, g, w_kv, k_g):
    b, lc, d = ctx.shape
    kv_w = KV_HEADS * HEAD_DIM
    row = lambda bi: (bi, 0, 0)
    return pl.pallas_call(
        _ctx_kv_kernel,
        grid=(b,),
        in_specs=[
            pl.BlockSpec((None, lc, d), row), _const_spec((1, d)), _const_spec((1, d)), _const_spec((1, d)),
            _const_spec((d, 2 * kv_w)), _const_spec((1, HEAD_DIM)),
        ],
        out_specs=[pl.BlockSpec((None, lc, kv_w), row), pl.BlockSpec((None, lc, kv_w), row)],
        out_shape=[jax.ShapeDtypeStruct((b, lc, kv_w), BF16), jax.ShapeDtypeStruct((b, lc, kv_w), BF16)],
        compiler_params=_params("arbitrary"),
        name="ctx_kv",
    )(ctx, sh, sc, g, w_kv, k_g)


def _attn_kernel(q_ref, k_ref, v_ref, kc_ref, vc_ref, o_ref, *, group, tk):
    tq = q_ref.shape[0]
    rows = group * tq
    q = jnp.concatenate([q_ref[:, g * HEAD_DIM:(g + 1) * HEAD_DIM] for g in range(group)], axis=0)
    nt = (((1,), (1,)), ((), ()))
    chunks = [(k_ref, v_ref, c * tk, tk) for c in range(k_ref.shape[0] // tk)]
    chunks.append((kc_ref, vc_ref, 0, kc_ref.shape[0]))
    m = jnp.full((rows, 1), -jnp.inf, F32)
    l = jnp.zeros((rows, 1), F32)
    acc = jnp.zeros((rows, HEAD_DIM), F32)
    for kr, vr, start, size in chunks:
        s = lax.dot_general(q, kr[start:start + size, :], nt, preferred_element_type=F32)
        m_new = jnp.maximum(m, jnp.max(s, axis=-1, keepdims=True))
        alpha = jnp.exp2(m - m_new)
        p = jnp.exp2(s - m_new)
        l = alpha * l + jnp.sum(p, axis=-1, keepdims=True)
        acc = alpha * acc + jnp.dot(p.astype(BF16), vr[start:start + size, :], preferred_element_type=F32)
        m = m_new
    o = acc / l
    for g in range(group):
        o_ref[:, g * HEAD_DIM:(g + 1) * HEAD_DIM] = o[g * tq:(g + 1) * tq, :].astype(o_ref.dtype)


def _attention(q, k, v, kc, vc):
    b, s, q_w = q.shape
    lc = kc.shape[1]
    group = q_w // (KV_HEADS * HEAD_DIM)
    tq = _tile(s, 512)
    kv_spec = lambda n: pl.BlockSpec((None, n, HEAD_DIM), lambda bi, h, i: (bi, 0, h))
    q_spec = pl.BlockSpec((None, tq, group * HEAD_DIM), lambda bi, h, i: (bi, i, h))
    return pl.pallas_call(
        functools.partial(_attn_kernel, group=group, tk=_tile(s, 1024)),
        grid=(b, KV_HEADS, s // tq),
        in_specs=[q_spec, kv_spec(s), kv_spec(s), kv_spec(lc), kv_spec(lc)],
        out_specs=q_spec,
        out_shape=jax.ShapeDtypeStruct((b, s, q_w), BF16),
        compiler_params=_params("arbitrary", "arbitrary", "arbitrary"),
        name="attention",
    )(q, k, v, kc, vc)


def _conv_kernel(zc_ref, zp_ref, zn_ref, w_ref, b_ref, g_ref, beta_ref, o_ref, win_ref, conv_ref,
                 *, ts, n_tiles, rb, cb):
    i = pl.program_id(1)
    c = zc_ref.shape[1]
    win_ref[HALO:HALO + ts, :] = zc_ref[...]
    win_ref[0:HALO, :] = jnp.where(i > 0, zp_ref[...], 0.0)
    win_ref[HALO + ts:2 * HALO + ts, :] = jnp.where(i < n_tiles - 1, zn_ref[...], 0.0)
    base = HALO - CONV_PAD
    n_a = -(-CONV_K // SUBLANES)
    qn = rb + SUBLANES
    for r in range(ts // rb):
        for cc in range(c // cb):
            cs = slice(cc * cb, (cc + 1) * cb)
            rows = win_ref[r * rb:r * rb + qn + SUBLANES * (n_a - 1), cs]
            acc = jnp.zeros((rb, cb), F32)
            for rr in range(SUBLANES):
                q = None
                for a in range(n_a):
                    kk = SUBLANES * a + rr
                    if kk < CONV_K:
                        term = rows[SUBLANES * a:SUBLANES * a + qn, :] * w_ref[kk:kk + 1, cs]
                        q = term if q is None else q + term
                acc = acc + pltpu.roll(q, (qn - base - rr) % qn, 0)[0:rb, :]
            conv_ref[r * rb:(r + 1) * rb, cs] = acc + b_ref[:, cs]
    z = conv_ref[...]
    mu = jnp.mean(z, axis=-1, keepdims=True)
    zc = z - mu
    var = jnp.mean(zc * zc, axis=-1, keepdims=True)
    y = zc * lax.rsqrt(var + EPS) * g_ref[...] + beta_ref[...]
    o_ref[...] = (y * _sigmoid(y)).astype(o_ref.dtype)


def _conformer_conv(z, conv_w, conv_b, cn_g, cn_b):
    b, s, c = z.shape
    ts = _tile(s, 256)
    n_tiles = s // ts
    hb = ts // HALO
    n_hb = s // HALO
    cur = lambda bi, i: (bi, i, 0)
    prev = lambda bi, i: (bi, jnp.maximum(i * hb - 1, 0), 0)
    nxt = lambda bi, i: (bi, jnp.minimum((i + 1) * hb, n_hb - 1), 0)
    return pl.pallas_call(
        functools.partial(_conv_kernel, ts=ts, n_tiles=n_tiles, rb=_tile(ts, 64), cb=_tile(c, LANES)),
        grid=(b, n_tiles),
        in_specs=[
            pl.BlockSpec((None, ts, c), cur), pl.BlockSpec((None, HALO, c), prev), pl.BlockSpec((None, HALO, c), nxt),
            _const_spec((CONV_K, c)), _const_spec((1, c)), _const_spec((1, c)), _const_spec((1, c)),
        ],
        out_specs=pl.BlockSpec((None, ts, c), cur),
        out_shape=jax.ShapeDtypeStruct((b, s, c), BF16),
        scratch_shapes=[pltpu.VMEM((ts + 2 * HALO, c), F32), pltpu.VMEM((ts, c), F32)],
        compiler_params=_params("arbitrary", "arbitrary"),
        name="conformer_conv",
    )(z, z, z, conv_w, conv_b, cn_g, cn_b)


def _token_sublanes(d):
    assert d % (2 * LANES) == 0
    return d // (2 * LANES)


def _pack_pair(lo, hi):
    lo_bits = lax.bitcast_convert_type(lo.astype(BF16).astype(F32), jnp.uint32) >> 16
    hi_bits = lax.bitcast_convert_type(hi.astype(BF16).astype(F32), jnp.uint32) & jnp.uint32(0xFFFF0000)
    return lo_bits | hi_bits


def _unpack_pair(w):
    lo = lax.bitcast_convert_type(w << 16, F32)
    hi = lax.bitcast_convert_type(w & jnp.uint32(0xFFFF0000), F32)
    return lo, hi


def _store_token_tiles(v, ref, start=0):
    m, d = v.shape
    sub = _token_sublanes(d)
    for s in range(sub):
        lo = v[:, 2 * s * LANES:(2 * s + 1) * LANES]
        hi = v[:, (2 * s + 1) * LANES:(2 * s + 2) * LANES]
        ref[pl.ds(start + s, m, stride=sub), :] = _pack_pair(lo, hi)


def _load_token_tiles(ref, start, m, sub):
    parts = []
    for s in range(sub):
        parts.extend(_unpack_pair(ref[pl.ds(start + s, m, stride=sub), :]))
    return parts


def _finish_tile(xnew, g2n_ref, sh2_ref, sc2_ref, rwt_ref, x1_ref, h2p_ref, lgt_ref):
    x1_ref[...] = xnew
    h2 = _adaln(xnew, g2n_ref[...], sh2_ref[...], sc2_ref[...])
    _store_token_tiles(h2, h2p_ref)
    lgt_ref[...] = lax.dot_general(rwt_ref[...], h2.astype(BF16), (((1,), (1,)), ((), ())),
                                   preferred_element_type=F32)


def _outproj_kernel(a_ref, c_ref, x_ref, wa_ref, wc_ref, g1_ref, g2n_ref, sh2_ref, sc2_ref, rwt_ref,
                    x1_ref, h2p_ref, lgt_ref):
    y = jnp.dot(a_ref[...], wa_ref[...], preferred_element_type=F32)
    y = y + jnp.dot(c_ref[...], wc_ref[...], preferred_element_type=F32)
    _finish_tile(x_ref[...] + g1_ref[...] * y, g2n_ref, sh2_ref, sc2_ref, rwt_ref, x1_ref, h2p_ref, lgt_ref)


def _finish_specs(b, s, d, e, tm):
    nt = s // tm
    sub = _token_sublanes(d)
    row = lambda bi, i: (bi, i, 0)
    flat = lambda bi, i: (bi * nt + i, 0)
    out_specs = [pl.BlockSpec((None, tm, d), row), pl.BlockSpec((tm * sub, LANES), flat),
                 pl.BlockSpec((e, tm), lambda bi, i: (0, bi * nt + i))]
    out_shape = [jax.ShapeDtypeStruct((b, s, d), F32), jax.ShapeDtypeStruct((b * s * sub, LANES), jnp.uint32),
                 jax.ShapeDtypeStruct((e, b * s), F32)]
    return out_specs, out_shape


def _outproj(attn, cv, x, w_a, w_c, g1, g2n, sh2, sc2, rwt):
    b, s, d = x.shape
    e = rwt.shape[0]
    tm = _tile(s, 512)
    row = lambda bi, i: (bi, i, 0)
    vec = pl.BlockSpec((None, 1, d), lambda bi, i: (bi, 0, 0))
    out_specs, out_shape = _finish_specs(b, s, d, e, tm)
    return pl.pallas_call(
        _outproj_kernel,
        grid=(b, s // tm),
        in_specs=[
            pl.BlockSpec((None, tm, attn.shape[2]), row), pl.BlockSpec((None, tm, cv.shape[2]), row),
            pl.BlockSpec((None, tm, d), row), _const_spec(w_a.shape), _const_spec(w_c.shape),
            vec, _const_spec((1, d)), vec, vec, _const_spec((e, d)),
        ],
        out_specs=out_specs,
        out_shape=out_shape,
        compiler_params=_params("arbitrary", "arbitrary"),
        name="outproj",
    )(attn, cv, x, w_a, w_c, g1, g2n, sh2, sc2, rwt)


def _pool_kernel(xc_ref, xp_ref, xn_ref, g1n_ref, sh1_ref, sc1_ref, pw_ref, pb_ref, ps_ref, g1_ref,
                 g2n_ref, sh2_ref, sc2_ref, rwt_ref, x1_ref, h2p_ref, lgt_ref, win_ref, *, tm, n_tiles, seq):
    i = pl.program_id(1)
    d = xc_ref.shape[1]
    gc = d // len(POOL_WINDOWS)
    norm = lambda x: _adaln(x, g1n_ref[...], sh1_ref[...], sc1_ref[...])
    win_ref[HALO:HALO + tm, :] = norm(xc_ref[...])
    win_ref[0:HALO, :] = jnp.where(i > 0, norm(xp_ref[...]), 0.0)
    win_ref[HALO + tm:2 * HALO + tm, :] = jnp.where(i < n_tiles - 1, norm(xn_ref[...]), 0.0)
    t = i * tm + lax.broadcasted_iota(jnp.int32, (tm, 1), 0)
    n = tm + 2 * SUBLANES
    ahead = lambda a, k: pltpu.roll(a, n - k, 0)
    ys = []
    for gi, w in enumerate(POOL_WINDOWS):
        assert w in (2, 4, 8, 16)
        cs = slice(gi * gc, (gi + 1) * gc)
        run = win_ref[HALO - SUBLANES:HALO + tm + SUBLANES, cs]
        span = 1
        while 2 * span < w:
            run = run + ahead(run, span)
            span *= 2
        first = run[0:tm, :] if span == SUBLANES else ahead(run, SUBLANES - span)[0:tm, :]
        tot = first + run[SUBLANES:SUBLANES + tm, :]
        lo = jnp.clip(t - w // 2, 0, seq)
        hi = jnp.clip(t + w - w // 2, 0, seq)
        p = tot / (hi - lo).astype(F32) - win_ref[HALO:HALO + tm, cs]
        ys.append(jnp.dot(p.astype(BF16), pw_ref[gi], preferred_element_type=F32))
    y = (jnp.concatenate(ys, axis=-1) + pb_ref[...]) * ps_ref[...]
    _finish_tile(xc_ref[...] + g1_ref[...] * y, g2n_ref, sh2_ref, sc2_ref, rwt_ref, x1_ref, h2p_ref, lgt_ref)


def _pool_layer(x, g1n, sh1, sc1, pool_w, pool_b, pool_scale, g1, g2n, sh2, sc2, rwt):
    b, s, d = x.shape
    e = rwt.shape[0]
    tm = _tile(s, 512)
    n_tiles = s // tm
    hb = tm // HALO
    n_hb = s // HALO
    cur = lambda bi, i: (bi, i, 0)
    prev = lambda bi, i: (bi, jnp.maximum(i * hb - 1, 0), 0)
    nxt = lambda bi, i: (bi, jnp.minimum((i + 1) * hb, n_hb - 1), 0)
    vec = pl.BlockSpec((None, 1, d), lambda bi, i: (bi, 0, 0))
    cvec = _const_spec((1, d))
    out_specs, out_shape = _finish_specs(b, s, d, e, tm)
    return pl.pallas_call(
        functools.partial(_pool_kernel, tm=tm, n_tiles=n_tiles, seq=s),
        grid=(b, n_tiles),
        in_specs=[
            pl.BlockSpec((None, tm, d), cur), pl.BlockSpec((None, HALO, d), prev), pl.BlockSpec((None, HALO, d), nxt),
            cvec, vec, vec, _const_spec(pool_w.shape), cvec, cvec, vec, cvec, vec, vec, _const_spec((e, d)),
        ],
        out_specs=out_specs,
        out_shape=out_shape,
        scratch_shapes=[pltpu.VMEM((tm + 2 * HALO, d), F32)],
        compiler_params=_params("arbitrary", "arbitrary"),
        name="pool_mixer",
    )(x, x, x, g1n, sh1, sc1, pool_w, pool_b, pool_scale, g1, g2n, sh2, sc2, rwt)


def _route_kernel(lgt_ref, bias_ref, tri_ref, eid_ref, pos_ref, wts_ref, cnt_ref, carry_ref):
    i = pl.program_id(0)
    e, tn = lgt_ref.shape
    eg = e // N_EXPERT_GROUPS

    @pl.when(i == 0)
    def _():
        carry_ref[...] = jnp.zeros_like(carry_ref)

    scores = _sigmoid(lgt_ref[...])
    sel = scores + bias_ref[...]
    neg = jnp.float32(-jnp.inf)
    sub = lax.broadcasted_iota(jnp.int32, (eg, tn), 0)
    group_scores = []
    for g in range(N_EXPERT_GROUPS):
        blk = sel[g * eg:(g + 1) * eg, :]
        m1 = jnp.max(blk, axis=0, keepdims=True)
        first = jnp.min(jnp.where(blk == m1, sub, eg), axis=0, keepdims=True)
        m2 = jnp.max(jnp.where(sub == first, neg, blk), axis=0, keepdims=True)
        group_scores.append(m1 + m2)
    masked = []
    for g in range(N_EXPERT_GROUPS):
        rank = jnp.zeros((1, tn), jnp.int32)
        for g2 in range(N_EXPERT_GROUPS):
            if g2 == g:
                continue
            ahead = (group_scores[g2] >= group_scores[g]) if g2 < g else (group_scores[g2] > group_scores[g])
            rank = rank + ahead.astype(jnp.int32)
        masked.append(jnp.where(rank < TOPK_GROUPS, sel[g * eg:(g + 1) * eg, :], neg))
    selm = jnp.concatenate(masked, axis=0)

    eidx = lax.broadcasted_iota(jnp.int32, (e, tn), 0)
    work = selm
    picks = []
    for _ in range(TOP_K):
        mx = jnp.max(work, axis=0, keepdims=True)
        pick = jnp.min(jnp.where(work == mx, eidx, e), axis=0, keepdims=True)
        picks.append(pick)
        work = jnp.where(eidx == pick, neg, work)
    chosen = functools.reduce(jnp.logical_or, [eidx == p for p in picks])

    m = chosen.astype(BF16)
    pos = jnp.dot(m, tri_ref[...], preferred_element_type=F32) + carry_ref[...]
    carry_ref[...] = carry_ref[...] + jnp.sum(chosen.astype(F32), axis=1, keepdims=True)
    cnt_ref[...] = carry_ref[...]

    wsum = jnp.sum(jnp.where(chosen, scores, 0.0), axis=0, keepdims=True)
    for j, pick in enumerate(picks):
        hit = eidx == pick
        eid_ref[j:j + 1, :] = pick
        pos_ref[j:j + 1, :] = jnp.sum(jnp.where(hit, pos, 0.0), axis=0, keepdims=True).astype(jnp.int32)
        wj = jnp.sum(jnp.where(hit, scores, 0.0), axis=0, keepdims=True)
        wts_ref[j:j + 1, :] = wj / wsum * ROUTED_SCALE


def _route(lgt, router_bias):
    e, t = lgt.shape
    tn = _tile(t, 512)
    tri = (lax.broadcasted_iota(jnp.int32, (tn, tn), 0) < lax.broadcasted_iota(jnp.int32, (tn, tn), 1)).astype(BF16)
    col = lambda i: (0, i)
    return pl.pallas_call(
        _route_kernel,
        grid=(t // tn,),
        in_specs=[pl.BlockSpec((e, tn), col), _const_spec((e, 1)), _const_spec((tn, tn))],
        out_specs=[pl.BlockSpec((TOP_K, tn), col), pl.BlockSpec((TOP_K, tn), col), pl.BlockSpec((TOP_K, tn), col),
                   pl.BlockSpec((e, 1), lambda i: (0, 0))],
        out_shape=[jax.ShapeDtypeStruct((TOP_K, t), jnp.int32), jax.ShapeDtypeStruct((TOP_K, t), jnp.int32),
                   jax.ShapeDtypeStruct((TOP_K, t), F32), jax.ShapeDtypeStruct((e, 1), F32)],
        scratch_shapes=[pltpu.VMEM((e, 1), F32)],
        compiler_params=_params("arbitrary"),
        name="route",
    )(lgt, router_bias.reshape(e, 1), tri)


def _dest_kernel(poff_ref, eid_ref, pos_ref, dest_ref, *, sub):
    eid = eid_ref[...]
    acc = pos_ref[...]
    for e in range(poff_ref.shape[0]):
        acc = acc + jnp.where(eid == e, poff_ref[e], 0)
    dest_ref[...] = acc * sub


def _dest_rows(poff, eid, pos, sub):
    k, t = eid.shape
    tn = _tile(t, 2048)
    col = lambda i, *_: (0, i)
    return pl.pallas_call(
        functools.partial(_dest_kernel, sub=sub),
        grid_spec=pltpu.PrefetchScalarGridSpec(
            num_scalar_prefetch=1, grid=(t // tn,),
            in_specs=[pl.BlockSpec((k, tn), col), pl.BlockSpec((k, tn), col)],
            out_specs=pl.BlockSpec((k, tn), col)),
        out_shape=jax.ShapeDtypeStruct((k, t), jnp.int32),
        compiler_params=_params("arbitrary"),
        name="dest_rows",
    )(poff, eid, pos)


def _dispatch_kernel(pend_ref, pcnt_ref, dest_ref, h_ref, xs_ref, zero_ref, sem, zsem, *, tt, rows, sub):
    i = pl.program_id(0)
    n_exp = pend_ref.shape[0]

    def zero_copy(e):
        start = pl.multiple_of((pend_ref[e] - rows) * sub, rows * sub)
        return pltpu.make_async_copy(zero_ref, xs_ref.at[pl.ds(start, rows * sub)], zsem)

    @pl.when(i == 0)
    def _():
        zero_ref[...] = jnp.zeros_like(zero_ref)

        def start_zero(e, c):
            @pl.when(pcnt_ref[e] > 0)
            def _():
                zero_copy(e).start()
            return c

        def wait_zero(e, c):
            @pl.when(pcnt_ref[e] > 0)
            def _():
                zero_copy(e).wait()
            return c

        lax.fori_loop(0, n_exp, start_zero, 0)
        lax.fori_loop(0, n_exp, wait_zero, 0)

    def row_copy(t, j):
        src = h_ref.at[pl.ds(pl.multiple_of(t * sub, sub), sub)]
        dst = xs_ref.at[pl.ds(pl.multiple_of(dest_ref[j, t], sub), sub)]
        return pltpu.make_async_copy(src, dst, sem)

    def issue(t, c):
        for j in range(TOP_K):
            row_copy(t, j).start(priority=j % 2)
        return c

    def drain(t, c):
        for j in range(TOP_K):
            row_copy(t, j).wait()
        return c

    lax.fori_loop(0, tt, issue, 0)
    lax.fori_loop(0, tt, drain, 0)


def _dispatch(h2p, dest, pend, pcnt, n_rows, rows, sub):
    t = h2p.shape[0] // sub
    tt = _tile(t, 256)
    return pl.pallas_call(
        functools.partial(_dispatch_kernel, tt=tt, rows=rows, sub=sub),
        grid_spec=pltpu.PrefetchScalarGridSpec(
            num_scalar_prefetch=2, grid=(t // tt,),
            in_specs=[pl.BlockSpec((TOP_K, tt), lambda i, *_: (0, i), memory_space=pltpu.SMEM),
                      pl.BlockSpec((tt * sub, LANES), lambda i, *_: (i, 0))],
            out_specs=pl.BlockSpec(memory_space=pl.ANY),
            scratch_shapes=[pltpu.VMEM((rows * sub, LANES), jnp.uint32), pltpu.SemaphoreType.DMA,
                            pltpu.SemaphoreType.DMA]),
        out_shape=jax.ShapeDtypeStruct((n_rows * sub, LANES), jnp.uint32),
        compiler_params=_params("arbitrary"),
        name="dispatch",
    )(pend, pcnt, dest, h2p)


def _expert_kernel(blk_e_ref, nreal_ref, first_ref, slot_ref, next_e_ref, has_next_ref, x_ref, wg_hbm, wu_hbm, wd_hbm,
                   y_ref, wg32_ref, wu32_ref, wd32_ref, wgb_ref, wub_ref, wdb_ref, wsem, *, rows, sub, layer):
    b = pl.program_id(0)

    def weight_copies(e, slot):
        return [pltpu.make_async_copy(src.at[layer, e], dst.at[slot], wsem.at[slot])
                for src, dst in ((wg_hbm, wg32_ref), (wu_hbm, wu32_ref), (wd_hbm, wd32_ref))]

    @pl.when(first_ref[b] == 1)
    def _():
        e = blk_e_ref[b]
        slot = slot_ref[b]

        @pl.when(b == 0)
        def _():
            for cp in weight_copies(e, slot):
                cp.start()

        @pl.when(has_next_ref[b] == 1)
        def _():
            for cp in weight_copies(next_e_ref[b], 1 - slot):
                cp.start()

        for cp in weight_copies(e, slot):
            cp.wait()
        wgb_ref[...] = wg32_ref[slot].astype(BF16)
        wub_ref[...] = wu32_ref[slot].astype(BF16)
        wdb_ref[...] = wd32_ref[slot].astype(BF16)

    @pl.when(b < nreal_ref[0])
    def _():
        x = jnp.concatenate([p.astype(BF16) for p in _load_token_tiles(x_ref, 0, rows, sub)], axis=1)
        g = jnp.dot(x, wgb_ref[...], preferred_element_type=F32)
        u = jnp.dot(x, wub_ref[...], preferred_element_type=F32)
        a = (g * _sigmoid(g) * u).astype(BF16)
        _store_token_tiles(jnp.dot(a, wdb_ref[...], preferred_element_type=F32), y_ref)


def _experts(xs, blk_e, n_real, pend, w_gate, w_up, w_down, layer, rows, sub):
    d, ff = w_gate.shape[2:]
    nb = xs.shape[0] // (rows * sub)
    blocks = jnp.arange(nb, dtype=jnp.int32)
    valid = blocks < n_real[0]
    prev_e = jnp.concatenate([jnp.full((1,), -1, jnp.int32), blk_e[:-1]])
    first = jnp.logical_and(valid, blk_e != prev_e)
    slot = (jnp.cumsum(first.astype(jnp.int32)) - 1) % 2
    next_blk = pend[blk_e] // rows
    has_next = jnp.logical_and(first, next_blk < n_real[0])
    next_e = blk_e[jnp.minimum(next_blk, nb - 1)]
    i32 = lambda a: a.astype(jnp.int32)

    xrow = lambda b, be, nr, *_: (jnp.minimum(b, jnp.maximum(nr[0] - 1, 0)), 0)
    hbm = pl.BlockSpec(memory_space=pl.ANY)
    return pl.pallas_call(
        functools.partial(_expert_kernel, rows=rows, sub=sub, layer=layer),
        grid_spec=pltpu.PrefetchScalarGridSpec(
            num_scalar_prefetch=6, grid=(nb,),
            in_specs=[pl.BlockSpec((rows * sub, LANES), xrow), hbm, hbm, hbm],
            out_specs=pl.BlockSpec((rows * sub, LANES), xrow),
            scratch_shapes=[pltpu.VMEM((2, d, ff), F32), pltpu.VMEM((2, d, ff), F32), pltpu.VMEM((2, ff, d), F32),
                            pltpu.VMEM((d, ff), BF16), pltpu.VMEM((d, ff), BF16), pltpu.VMEM((ff, d), BF16),
                            pltpu.SemaphoreType.DMA((2,))]),
        out_shape=jax.ShapeDtypeStruct(xs.shape, jnp.uint32),
        compiler_params=_params("arbitrary"),
        name="experts",
    )(blk_e, n_real, i32(first), i32(slot), i32(next_e), i32(has_next), xs, w_gate, w_up, w_down)


def _combine_kernel(dest_ref, dnext_ref, ys_ref, wt_ref, x1_ref, h_ref, sg_ref, su_ref, sd_ref, g2_ref, o_ref,
                    gbuf_ref, sems, *, tt, sub):
    i = pl.program_id(0)
    slot_rows = TOP_K * tt * sub
    t_chunk = tt // TOP_K

    def gather(dref, col, slot, t, j):
        src = ys_ref.at[pl.ds(pl.multiple_of(dref[j, col + t], sub), sub)]
        dst = gbuf_ref.at[pl.ds(slot * slot_rows + (j * tt + t) * sub, sub)]
        return pltpu.make_async_copy(src, dst, sems.at[slot])

    def wait_tile(slot):
        def drain(t, c):
            for j in range(TOP_K):
                gather(dest_ref, 0, slot, 0, j).wait()
            return c
        lax.fori_loop(0, tt, drain, 0)

    @pl.when(i == 0)
    def _():
        def issue(t, c):
            for j in range(TOP_K):
                gather(dest_ref, 0, 0, t, j).start(priority=j % 2)
            return c
        lax.fori_loop(0, tt, issue, 0)

    def tile(slot, row0, dref, col):
        rows = slice(row0, row0 + tt)
        h = jnp.concatenate([p.astype(BF16) for p in _load_token_tiles(h_ref, row0 * sub, tt, sub)], axis=1)
        g = jnp.dot(h, sg_ref[...], preferred_element_type=F32)
        u = jnp.dot(h, su_ref[...], preferred_element_type=F32)
        shared = jnp.dot((g * _sigmoid(g) * u).astype(BF16), sd_ref[...], preferred_element_type=F32)
        wt = wt_ref[rows, :]
        acc = [jnp.zeros((tt, LANES), F32) for _ in range(2 * sub)]
        for j in range(TOP_K):
            for t in range(j * t_chunk, (j + 1) * t_chunk):
                for jj in range(TOP_K):
                    gather(dref, col, 1 - slot, t, jj).start(priority=jj % 2)
            wj = jnp.broadcast_to(wt[:, j:j + 1], (tt, LANES))
            parts = _load_token_tiles(gbuf_ref, slot * slot_rows + j * tt * sub, tt, sub)
            acc = [a + p * wj for a, p in zip(acc, parts)]
        routed = jnp.concatenate(acc, axis=1)
        o_ref[rows, :] = x1_ref[rows, :] + g2_ref[...] * (routed + shared)

    wait_tile(0)
    tile(0, 0, dest_ref, tt)
    wait_tile(1)
    tile(1, tt, dnext_ref, 0)

    @pl.when(i == pl.num_programs(0) - 1)
    def _():
        wait_tile(0)


def _combine(ys, dest, wts_t, x1, h2p, s_gate, s_up, s_down, g2, seq, sub):
    t, d = x1.shape
    sf = s_gate.shape[1]
    tt = _tile(seq // 2, 128)
    assert tt % TOP_K == 0
    n_steps = t // (2 * tt)
    row = lambda i: (i, 0)
    return pl.pallas_call(
        functools.partial(_combine_kernel, tt=tt, sub=sub),
        grid=(n_steps,),
        in_specs=[
            pl.BlockSpec((TOP_K, 2 * tt), lambda i: (0, i), memory_space=pltpu.SMEM),
            pl.BlockSpec((TOP_K, tt), lambda i: (0, jnp.minimum(2 * i + 2, 2 * n_steps - 2)),
                         memory_space=pltpu.SMEM),
            pl.BlockSpec(memory_space=pl.ANY),
            pl.BlockSpec((2 * tt, TOP_K), row), pl.BlockSpec((2 * tt, d), row),
            pl.BlockSpec((2 * tt * sub, LANES), row),
            _const_spec((d, sf)), _const_spec((d, sf)), _const_spec((sf, d)),
            pl.BlockSpec((None, 1, d), lambda i: ((i * 2 * tt) // seq, 0, 0)),
        ],
        out_specs=pl.BlockSpec((2 * tt, d), row),
        out_shape=jax.ShapeDtypeStruct((t, d), F32),
        scratch_shapes=[pltpu.VMEM((2 * TOP_K * tt * sub, LANES), jnp.uint32), pltpu.SemaphoreType.DMA((2,))],
        compiler_params=_params("arbitrary"),
        name="combine",
    )(dest, dest, ys, wts_t, x1, h2p, s_gate, s_up, s_down, g2)


def _moe(x1, h2p, lgt, g2, router_bias, w_gate, w_up, w_down, layer, s_gate, s_up, s_down):
    b, s, d = x1.shape
    t = b * s
    e = lgt.shape[0]
    sub = _token_sublanes(d)
    rows = _tile(t, MOE_BLOCK_ROWS)
    eid, pos, wts, cnt = _route(lgt, router_bias)
    counts = cnt[:, 0].astype(jnp.int32)
    pcnt = (counts + rows - 1) // rows * rows
    pend = jnp.cumsum(pcnt)
    poff = pend - pcnt
    nb = (t * TOP_K + e * (rows - 1)) // rows
    n_real = (pend[-1] // rows).astype(jnp.int32).reshape(1)
    starts = jnp.arange(nb, dtype=jnp.int32) * rows
    blk_e = jnp.minimum(jnp.sum((pend[None, :] <= starts[:, None]).astype(jnp.int32), axis=1), e - 1)
    dest = _dest_rows(poff.astype(jnp.int32), eid, pos, sub)
    xs = _dispatch(h2p, dest, pend.astype(jnp.int32), pcnt.astype(jnp.int32), nb * rows, rows, sub)
    ys = _experts(xs, blk_e, n_real, pend.astype(jnp.int32), w_gate, w_up, w_down, layer, rows, sub)
    out = _combine(ys, dest, wts.T, x1.reshape(t, d), h2p, s_gate.astype(BF16), s_up.astype(BF16),
                   s_down.astype(BF16), g2, s, sub)
    return out.reshape(b, s, d)


def _rope_tables(n_tok):
    axis_dim = HEAD_DIM // 2
    rows = n_tok // GRID_W
    r, col = jnp.meshgrid(jnp.arange(rows), jnp.arange(GRID_W), indexing="ij")
    pos = jnp.stack([r.reshape(-1), col.reshape(-1)], axis=-1).astype(F32)
    inv = ROPE_THETA ** (-jnp.arange(0, axis_dim, 2, dtype=F32) / axis_dim)
    ang = pos[:, :, None] * inv
    ang = jnp.broadcast_to(ang[:, :, None, :], (n_tok, 2, 2, axis_dim // 2)).reshape(n_tok, HEAD_DIM)
    sign = jnp.where((jnp.arange(HEAD_DIM) % axis_dim) < axis_dim // 2, -1.0, 1.0).astype(F32)
    return jnp.cos(ang), jnp.sin(ang) * sign


def kernel(x, c, ctx, c_ctx, w_mod, b_mod, norm1_g, norm2_g, mix_w_in, q_norm_g, k_norm_g, conv_w, conv_b,
           conv_norm_g, conv_norm_b, mix_w_out, pool_w, pool_b, pool_scale, router_w, router_bias,
           moe_w_gate, moe_w_up, moe_w_down, shared_w_gate, shared_w_up, shared_w_down):
    b, s, d = x.shape
    depth = w_mod.shape[0]
    assert depth == 2, "layer schedule below is written for an attention layer followed by a pooling layer"
    kv_w = KV_HEADS * HEAD_DIM
    q_w = mix_w_in.shape[2] - 2 * kv_w - d

    pad = (-(b + 1)) % SUBLANES
    cc = jnp.concatenate([c, c_ctx[None, :], jnp.zeros((pad, d), F32)], axis=0)
    mods = _modulation(cc, w_mod, b_mod)

    def mod(layer, k, rows=slice(0, b)):
        return mods[layer, rows, k * d:(k + 1) * d][:, None, :]

    vec = lambda a: a.reshape(1, -1)
    rwt = jnp.swapaxes(router_w, 1, 2).astype(BF16)
    cos, sin_signed = _rope_tables(s)

    w_in = mix_w_in[0].astype(BF16)
    q, k, v, z = _inproj(x, mod(0, 0), mod(0, 1), vec(norm1_g[0]), w_in, vec(q_norm_g[0]), vec(k_norm_g[0]),
                         cos, sin_signed)
    ctx_row = slice(b, b + 1)
    kc, vc = _ctx_kv(ctx, mods[0, ctx_row, 0:d], mods[0, ctx_row, d:2 * d], vec(norm1_g[0]),
                     w_in[:, q_w:q_w + 2 * kv_w], vec(k_norm_g[0]))
    attn = _attention(q, k, v, kc, vc)
    cv = _conformer_conv(z, conv_w[0], vec(conv_b[0]), vec(conv_norm_g[0]), vec(conv_norm_b[0]))
    w_out = mix_w_out[0].astype(BF16)
    x1, h2p, lgt = _outproj(attn, cv, x, w_out[:q_w], w_out[q_w:], mod(0, 2), vec(norm2_g[0]), mod(0, 3),
                            mod(0, 4), rwt[0])
    x = _moe(x1, h2p, lgt, mod(0, 5), router_bias[0], moe_w_gate, moe_w_up, moe_w_down, 0,
             shared_w_gate[0], shared_w_up[0], shared_w_down[0])

    x1, h2p, lgt = _pool_layer(x, vec(norm1_g[1]), mod(1, 0), mod(1, 1), pool_w[0].astype(BF16), vec(pool_b[0]),
                               vec(pool_scale[0]), mod(1, 2), vec(norm2_g[1]), mod(1, 3), mod(1, 4), rwt[1])
    x = _moe(x1, h2p, lgt, mod(1, 5), router_bias[1], moe_w_gate, moe_w_up, moe_w_down, 1,
             shared_w_gate[1], shared_w_up[1], shared_w_down[1])
    return x
```
